```python
import math
import jax, jax.numpy as jnp
from jax import lax
import numpy as np


D_MODEL = 1024
BATCH = 16
SEQ = 2048
DEPTH = 2

HEAD_DIM = 64
ROPE_THETA = 10000.0
BLOCK = 128
NORM_EPS = 1e-6
NEG_INF = -1e30
DA_HEADS = D_MODEL // (2 * HEAD_DIM)
DA_QK = DA_HEADS * 2 * HEAD_DIM
DA_V = DA_HEADS * 2 * HEAD_DIM
WINDOW = 128
SW_HEADS = D_MODEL // HEAD_DIM
SW_KV_HEADS = 4
SW_GROUP = SW_HEADS // SW_KV_HEADS
FFN_HIDDEN = ((8 * D_MODEL // 3 + 255) // 256) * 256
N_A = DEPTH // 2
N_B = DEPTH - N_A

kernel_name = 'yoco_diffattn_swa_sink_hybrid'


def rms_norm(x, g):
    xf = x.astype(jnp.float32)
    y = xf * lax.rsqrt(jnp.mean(xf * xf, axis=-1, keepdims=True) + NORM_EPS)
    return (y * g.astype(jnp.float32)).astype(x.dtype)


def rope_tables(positions):
    inv = 1.0 / (ROPE_THETA ** (jnp.arange(0, HEAD_DIM, 2, dtype=jnp.float32) / HEAD_DIM))
    ang = positions.astype(jnp.float32)[..., None] * inv
    return jnp.cos(ang), jnp.sin(ang)


def apply_rope(x, cos, sin):
    extra = x.ndim - 3
    shp = cos.shape[:2] + (1,) * extra + cos.shape[-1:]
    c = cos.reshape(shp)
    s = sin.reshape(shp)
    x1, x2 = jnp.split(x.astype(jnp.float32), 2, axis=-1)
    return jnp.concatenate([x1 * c - x2 * s, x2 * c + x1 * s], axis=-1).astype(x.dtype)


def swiglu(h, w_gate_up, w_down):
    g, u = jnp.split(h @ w_gate_up, 2, axis=-1)
    return (jax.nn.silu(g) * u) @ w_down


def diff_attention(h, cos, sin, w_qkv, q_norm, k_norm, lam_p, subln, w_o, layer_idx):
    B, S, _ = h.shape
    lambda_init = 0.8 - 0.6 * math.exp(-0.3 * layer_idx)
    q, k, v = jnp.split(h @ w_qkv, [DA_QK, 2 * DA_QK], axis=-1)
    q = apply_rope(rms_norm(q.reshape(B, S, DA_HEADS, 2, HEAD_DIM), q_norm), cos, sin)
    k = apply_rope(rms_norm(k.reshape(B, S, DA_HEADS, 2, HEAD_DIM), k_norm), cos, sin)
    v = v.reshape(B, S, DA_HEADS, 2 * HEAD_DIM)
    lp = lam_p.astype(jnp.float32)
    lam = jnp.exp(jnp.sum(lp[0] * lp[1])) - jnp.exp(jnp.sum(lp[2] * lp[3])) + lambda_init
    scale = 1.0 / math.sqrt(HEAD_DIM)
    nb = S // BLOCK
    qb = jnp.moveaxis(q.reshape(B, nb, BLOCK, DA_HEADS, 2, HEAD_DIM), 1, 0)
    kpos = jnp.arange(S)

    def one_block(args):
        qblk, i = args
        s = jnp.einsum('bqhcd,bkhcd->bhcqk', qblk, k).astype(jnp.float32) * scale
        qpos = i * BLOCK + jnp.arange(BLOCK)
        causal = kpos[None, :] <= qpos[:, None]
        p = jax.nn.softmax(jnp.where(causal, s, NEG_INF), axis=-1)
        a = p[:, :, 0] - lam * p[:, :, 1]
        return jnp.einsum('bhqk,bkhe->bqhe', a.astype(v.dtype), v)

    o = lax.map(one_block, (qb, jnp.arange(nb)))
    o = jnp.moveaxis(o, 0, 1).reshape(B, S, DA_HEADS, 2 * HEAD_DIM)
    o = rms_norm(o, subln) * (1.0 - lambda_init)
    return o.reshape(B, S, DA_V) @ w_o


def shared_kv(x, cos, sin, kv_norm, w_kv, k_norm):
    B, S, _ = x.shape
    k, v = jnp.split(rms_norm(x, kv_norm) @ w_kv, 2, axis=-1)
    k = apply_rope(rms_norm(k.reshape(B, S, SW_KV_HEADS, HEAD_DIM), k_norm), cos, sin)
    v = v.reshape(B, S, SW_KV_HEADS, HEAD_DIM)
    nb = S // WINDOW

    def band(t):
        tp = jnp.pad(t, ((0, 0), (WINDOW, 0), (0, 0), (0, 0))).reshape(B, nb + 1, WINDOW, SW_KV_HEADS, HEAD_DIM)
        return jnp.concatenate([tp[:, :-1], tp[:, 1:]], axis=2)

    return band(k), band(v)


def swa_sink_attention(h, cos, sin, w_q, q_norm, sinks, w_o, kw, vw):
    B, S, _ = h.shape
    nb = S // WINDOW
    q = (h @ w_q).reshape(B, S, SW_KV_HEADS, SW_GROUP, HEAD_DIM)
    q = apply_rope(rms_norm(q, q_norm), cos, sin)
    qb = q.reshape(B, nb, WINDOW, SW_KV_HEADS, SW_GROUP, HEAD_DIM)
    s = jnp.einsum('bnqhgd,bnkhd->bnhgqk', qb, kw).astype(jnp.float32) * (1.0 / math.sqrt(HEAD_DIM))
    qi = jnp.arange(WINDOW)[:, None]
    kj = jnp.arange(2 * WINDOW)[None, :]
    in_band = (kj > qi) & (kj <= qi + WINDOW)
    valid = in_band[None] & ((jnp.arange(nb)[:, None, None] > 0) | (kj >= WINDOW)[None])
    s = jnp.where(valid[None, :, None, None], s, NEG_INF)
    sink = sinks.astype(jnp.float32).reshape(SW_KV_HEADS, SW_GROUP)[None, None, :, :, None, None]
    m = jnp.maximum(jnp.max(s, axis=-1, keepdims=True), sink)
    e = jnp.exp(s - m)
    p = e / (jnp.sum(e, axis=-1, keepdims=True) + jnp.exp(sink - m))
    o = jnp.einsum('bnhgqk,bnkhd->bnqhgd', p.astype(vw.dtype), vw)
    return o.reshape(B, S, SW_HEADS * HEAD_DIM) @ w_o


def setup_inputs(seed: int = 0) -> dict:
    key = jax.random.key(seed)
    ks = jax.random.split(key, 24)
    f32 = jnp.float32

    def w(k, shape, fan_in):
        return jax.random.normal(k, shape, f32) * (fan_in ** -0.5)

    def gain(k, shape):
        return 1.0 + 0.02 * jax.random.normal(k, shape, f32)

    x = jax.random.normal(ks[0], (BATCH, SEQ, D_MODEL), f32)
    offset = jax.random.randint(ks[1], (BATCH, 1), 0, 4096, dtype=jnp.int32)
    positions = jnp.arange(SEQ, dtype=jnp.int32)[None, :] + offset
    return {
        'x': x,
        'positions': positions,
        'attn_norm': gain(ks[2], (DEPTH, D_MODEL)),
        'ffn_norm': gain(ks[3], (DEPTH, D_MODEL)),
        'w_gate_up': w(ks[4], (DEPTH, D_MODEL, 2 * FFN_HIDDEN), D_MODEL),
        'w_down': w(ks[5], (DEPTH, FFN_HIDDEN, D_MODEL), FFN_HIDDEN),
        'da_w_qkv': w(ks[6], (N_A, D_MODEL, 2 * DA_QK + DA_V), D_MODEL),
        'da_q_norm': gain(ks[7], (N_A, 2, HEAD_DIM)),
        'da_k_norm': gain(ks[8], (N_A, 2, HEAD_DIM)),
        'da_lambda': 0.1 * jax.random.normal(ks[9], (N_A, 4, HEAD_DIM), f32),
        'da_subln': gain(ks[10], (N_A, 2 * HEAD_DIM)),
        'da_w_o': w(ks[11], (N_A, DA_V, D_MODEL), DA_V),
        'kv_norm': gain(ks[12], (D_MODEL,)),
        'w_kv': w(ks[13], (D_MODEL, 2 * SW_KV_HEADS * HEAD_DIM), D_MODEL),
        'k_norm': gain(ks[14], (HEAD_DIM,)),
        'sw_w_q': w(ks[15], (N_B, D_MODEL, SW_HEADS * HEAD_DIM), D_MODEL),
        'sw_q_norm': gain(ks[16], (N_B, HEAD_DIM)),
        'sw_sinks': 0.5 * jax.random.normal(ks[17], (N_B, SW_HEADS), f32),
        'sw_w_o': w(ks[18], (N_B, SW_HEADS * HEAD_DIM, D_MODEL), SW_HEADS * HEAD_DIM),
    }


def reference(x, positions, attn_norm, ffn_norm, w_gate_up, w_down, da_w_qkv, da_q_norm, da_k_norm,
              da_lambda, da_subln, da_w_o, kv_norm, w_kv, k_norm, sw_w_q, sw_q_norm, sw_sinks, sw_w_o):
    cos, sin = rope_tables(positions)
    kw = vw = None
    for l in range(DEPTH):
        h = rms_norm(x, attn_norm[l])
        if l < N_A:
            x = x + diff_attention(h, cos, sin, da_w_qkv[l], da_q_norm[l], da_k_norm[l],
                                   da_lambda[l], da_subln[l], da_w_o[l], l)
        else:
            if l == N_A:
                kw, vw = shared_kv(x, cos, sin, kv_norm, w_kv, k_norm)
                h = rms_norm(x, attn_norm[l])
            j = l - N_A
            x = x + swa_sink_attention(h, cos, sin, sw_w_q[j], sw_q_norm[j], sw_sinks[j], sw_w_o[j], kw, vw)
        x = x + swiglu(rms_norm(x, ffn_norm[l]), w_gate_up[l], w_down[l])
    return x
```

```python
import functools
import math

import jax
import jax.numpy as jnp
from jax import lax
from jax.experimental import pallas as pl
from jax.experimental.pallas import tpu as pltpu

HEAD_DIM = 64
ROPE_THETA = 10000.0
NORM_EPS = 1e-6
NEG_INF = -1e30
WINDOW = 128
SW_KV_HEADS = 4
SW_GROUP = 4

F32 = jnp.float32
BF16 = jnp.bfloat16

V7X_VMEM_LIMIT_BYTES = 56 * 1024 * 1024

ROW_TILE = 512
FEATURE_CHUNK = 512
FFN_CHUNK = 256
ATTN_BLOCK = 256

NT_DIMS = (((1,), (1,)), ((), ()))
TN_DIMS = (((0,), (0,)), ((), ()))


def _resident(shape):
    nd = len(shape)
    return pl.BlockSpec(shape, lambda *_: (0,) * nd, pipeline_mode=pl.Buffered(1))


def _rms_scale(x):
    return lax.rsqrt(jnp.mean(x * x, axis=-1, keepdims=True) + NORM_EPS)


def _norm_rope_t(t, gain, cos, sin):
    r = lax.rsqrt(jnp.mean(t * t, axis=0, keepdims=True) + NORM_EPS)
    tn = t * r * gain
    x1 = tn[: HEAD_DIM // 2]
    x2 = tn[HEAD_DIM // 2:]
    return jnp.concatenate([x1 * cos - x2 * sin, x2 * cos + x1 * sin], axis=0)


def _qkv0_kernel(x_ref, g_ref, wt_ref, qg_ref, kg_ref, cos_ref, sin_ref,
                 qt_ref, k_ref, vt_ref):
    d_model = x_ref.shape[-1]
    x = x_ref[0]
    h = (x * _rms_scale(x) * g_ref[...]).astype(BF16)
    cos = cos_ref[0]
    sin = sin_ref[0]
    n_chunks = d_model // FEATURE_CHUNK
    for c in range(3 * n_chunks):
        rows = slice(c * FEATURE_CHUNK, (c + 1) * FEATURE_CHUNK)
        res = lax.dot_general(wt_ref[rows, :], h, NT_DIMS, preferred_element_type=F32)
        kind, cc = divmod(c, n_chunks)
        out_rows = slice(cc * FEATURE_CHUNK, (cc + 1) * FEATURE_CHUNK)
        if kind == 2:
            vt_ref[0, out_rows, :] = res.astype(BF16)
            continue
        gain_ref = qg_ref if kind == 0 else kg_ref
        parts = []
        for g in range(FEATURE_CHUNK // HEAD_DIM):
            half = g % 2
            gain = gain_ref[half * HEAD_DIM:(half + 1) * HEAD_DIM, :]
            parts.append(_norm_rope_t(res[g * HEAD_DIM:(g + 1) * HEAD_DIM], gain, cos, sin))
        out = jnp.concatenate(parts, axis=0)
        if kind == 0:
            qt_ref[0, out_rows, :] = out.astype(BF16)
        else:
            k_ref[0, :, out_rows] = out.T.astype(BF16)


def _qkv0(x, g, wt, qg, kg, cos_t, sin_t):
    b, s, d = x.shape
    tm = ROW_TILE
    grid = (b, s // tm)
    feat = pl.BlockSpec((1, d, tm), lambda i, j: (i, 0, j))
    tok = pl.BlockSpec((1, tm, d), lambda i, j: (i, j, 0))
    rope = pl.BlockSpec((1, HEAD_DIM // 2, tm), lambda i, j: (i, 0, j))
    return pl.pallas_call(
        _qkv0_kernel,
        grid=grid,
        in_specs=[tok, _resident((1, d)), _resident(wt.shape), _resident(qg.shape),
                  _resident(kg.shape), rope, rope],
        out_specs=[feat, tok, feat],
        out_shape=[jax.ShapeDtypeStruct((b, d, s), BF16),
                   jax.ShapeDtypeStruct((b, s, d), BF16),
                   jax.ShapeDtypeStruct((b, d, s), BF16)],
        compiler_params=pltpu.CompilerParams(
            dimension_semantics=("parallel", "parallel"),
            vmem_limit_bytes=V7X_VMEM_LIMIT_BYTES),
        name="qkv0_proj",
    )(x, g, wt, qg, kg, cos_t, sin_t)


def _diff_attn_kernel(lam_ref, qt_ref, k_ref, vt_ref, sg_ref, ot_ref, *, lambda_init):
    bq = qt_ref.shape[-1]
    bk = bq
    hd2 = qt_ref.shape[1]
    i = pl.program_id(2)

    lp = lam_ref[...]
    lam = (jnp.exp(jnp.sum(lp[0:1] * lp[1:2], axis=-1, keepdims=True))
           - jnp.exp(jnp.sum(lp[2:3] * lp[3:4], axis=-1, keepdims=True)) + lambda_init)

    qt = qt_ref[0]
    zero = jnp.zeros((HEAD_DIM, bq), BF16)
    qz = (jnp.concatenate([qt[:HEAD_DIM], zero], axis=0),
          jnp.concatenate([zero, qt[HEAD_DIM:]], axis=0))

    def block(j, carry, masked):
        kblk = k_ref[0, pl.ds(pl.multiple_of(j * bk, bk), bk), :]
        vblk = vt_ref[0, :, pl.ds(pl.multiple_of(j * bk, bk), bk)]
        out = []
        for c in range(2):
            m, l, acc = carry[c]
            s = jnp.dot(kblk, qz[c], preferred_element_type=F32)
            if masked:
                kidx = lax.broadcasted_iota(jnp.int32, (bk, bq), 0)
                qidx = lax.broadcasted_iota(jnp.int32, (bk, bq), 1)
                s = jnp.where(kidx <= qidx, s, NEG_INF)
            m_new = jnp.maximum(m, jnp.max(s, axis=0, keepdims=True))
            alpha = jnp.exp(m - m_new)
            p = jnp.exp(s - m_new)
            l = alpha * l + jnp.sum(p, axis=0, keepdims=True)
            acc = alpha * acc + jnp.dot(vblk, p.astype(BF16), preferred_element_type=F32)
            out.append((m_new, l, acc))
        return tuple(out)

    init_half = (jnp.full((1, bq), NEG_INF, F32), jnp.zeros((1, bq), F32),
                 jnp.zeros((hd2, bq), F32))
    carry = lax.fori_loop(0, i, lambda j, cr: block(j, cr, False), (init_half, init_half))
    (_, l0, a0), (_, l1, a1) = block(i, carry, True)

    o = a0 / l0 - lam * (a1 / l1)
    r = lax.rsqrt(jnp.mean(o * o, axis=0, keepdims=True) + NORM_EPS)
    ot_ref[0] = (o * r * sg_ref[...] * (1.0 - lambda_init)).astype(BF16)


def _diff_attn(lam_p, qt, k, vt, sg, lambda_init):
    b, d, s = qt.shape
    hd2 = 2 * HEAD_DIM
    bq = ATTN_BLOCK
    grid = (b, d // hd2, s // bq)
    return pl.pallas_call(
        functools.partial(_diff_attn_kernel, lambda_init=lambda_init),
        grid=grid,
        in_specs=[_resident(lam_p.shape),
                  pl.BlockSpec((1, hd2, bq), lambda bi, h, i: (bi, h, i)),
                  pl.BlockSpec((1, s, hd2), lambda bi, h, i: (bi, 0, h)),
                  pl.BlockSpec((1, hd2, s), lambda bi, h, i: (bi, h, 0)),
                  _resident(sg.shape)],
        out_specs=pl.BlockSpec((1, hd2, bq), lambda bi, h, i: (bi, h, i)),
        out_shape=jax.ShapeDtypeStruct((b, d, s), BF16),
        compiler_params=pltpu.CompilerParams(
            dimension_semantics=("parallel", "parallel", "arbitrary"),
            vmem_limit_bytes=V7X_VMEM_LIMIT_BYTES),
        name="diff_attn",
    )(lam_p, qt, k, vt, sg)


def _proj_ffn_kernel(x_ref, at_ref, wo_ref, g_ref, wgu_ref, wd_ref, o_ref, hid_ref):
    x = x_ref[0] + lax.dot_general(at_ref[0], wo_ref[...], TN_DIMS,
                                   preferred_element_type=F32)
    h = (x * _rms_scale(x) * g_ref[...]).astype(BF16)
    n_chunks = hid_ref.shape[-1] // FFN_CHUNK
    for c in range(n_chunks):
        gu = jnp.dot(h, wgu_ref[:, 2 * c * FFN_CHUNK:2 * (c + 1) * FFN_CHUNK],
                     preferred_element_type=F32)
        gate = gu[:, :FFN_CHUNK]
        up = gu[:, FFN_CHUNK:]
        hid_ref[:, c * FFN_CHUNK:(c + 1) * FFN_CHUNK] = (
            gate * jax.nn.sigmoid(gate) * up).astype(BF16)
    o_ref[0] = x + jnp.dot(hid_ref[...], wd_ref[...], preferred_element_type=F32)


def _proj_ffn(x, at, wo, g, wgu, wd):
    b, s, d = x.shape
    tm = ROW_TILE
    hidden = wd.shape[0]
    tok = pl.BlockSpec((1, tm, d), lambda i, j: (i, j, 0))
    feat = pl.BlockSpec((1, d, tm), lambda i, j: (i, 0, j))
    return pl.pallas_call(
        _proj_ffn_kernel,
        grid=(b, s // tm),
        in_specs=[tok, feat, _resident(wo.shape), _resident((1, d)), _resident(wgu.shape),
                  _resident(wd.shape)],
        out_specs=tok,
        out_shape=jax.ShapeDtypeStruct((b, s, d), F32),
        scratch_shapes=[pltpu.VMEM((tm, hidden), BF16)],
        compiler_params=pltpu.CompilerParams(
            dimension_semantics=("parallel", "parallel"),
            vmem_limit_bytes=V7X_VMEM_LIMIT_BYTES),
        name="proj_ffn",
    )(x, at, wo, g, wgu, wd)


def _qkv1_kernel(x_ref, ga_ref, gkv_ref, wqt_ref, wkvt_ref, qg_ref, kg_ref, cos_ref, sin_ref,
                 qt_ref, k_ref, vt_ref):
    d_model = x_ref.shape[-1]
    kv_dim = k_ref.shape[-1]
    x = x_ref[0]
    xn = x * _rms_scale(x)
    h_a = (xn * ga_ref[...]).astype(BF16)
    h_kv = (xn * gkv_ref[...]).astype(BF16)
    cos = cos_ref[0]
    sin = sin_ref[0]

    kv = lax.dot_general(wkvt_ref[...], h_kv, NT_DIMS, preferred_element_type=F32)
    kparts = [_norm_rope_t(kv[g * HEAD_DIM:(g + 1) * HEAD_DIM], kg_ref[...], cos, sin)
              for g in range(kv_dim // HEAD_DIM)]
    k_ref[0] = jnp.concatenate(kparts, axis=0).T.astype(BF16)
    vt_ref[0] = kv[kv_dim:].astype(BF16)

    for c in range(d_model // FEATURE_CHUNK):
        rows = slice(c * FEATURE_CHUNK, (c + 1) * FEATURE_CHUNK)
        res = lax.dot_general(wqt_ref[rows, :], h_a, NT_DIMS, preferred_element_type=F32)
        parts = [_norm_rope_t(res[g * HEAD_DIM:(g + 1) * HEAD_DIM], qg_ref[...], cos, sin)
                 for g in range(FEATURE_CHUNK // HEAD_DIM)]
        qt_ref[0, rows, :] = jnp.concatenate(parts, axis=0).astype(BF16)


def _qkv1(x, ga, gkv, wqt, wkvt, qg, kg, cos_t, sin_t):
    b, s, d = x.shape
    tm = ROW_TILE
    kv_dim = wkvt.shape[0] // 2
    tok = pl.BlockSpec((1, tm, d), lambda i, j: (i, j, 0))
    rope = pl.BlockSpec((1, HEAD_DIM // 2, tm), lambda i, j: (i, 0, j))
    return pl.pallas_call(
        _qkv1_kernel,
        grid=(b, s // tm),
        in_specs=[tok, _resident((1, d)), _resident((1, d)), _resident(wqt.shape),
                  _resident(wkvt.shape), _resident(qg.shape), _resident(kg.shape), rope, rope],
        out_specs=[pl.BlockSpec((1, d, tm), lambda i, j: (i, 0, j)),
                   pl.BlockSpec((1, tm, kv_dim), lambda i, j: (i, j, 0)),
                   pl.BlockSpec((1, kv_dim, tm), lambda i, j: (i, 0, j))],
        out_shape=[jax.ShapeDtypeStruct((b, d, s), BF16),
                   jax.ShapeDtypeStruct((b, s, kv_dim), BF16),
                   jax.ShapeDtypeStruct((b, kv_dim, s), BF16)],
        compiler_params=pltpu.CompilerParams(
            dimension_semantics=("parallel", "parallel"),
            vmem_limit_bytes=V7X_VMEM_LIMIT_BYTES),
        name="qkv1_proj",
    )(x, ga, gkv, wqt, wkvt, qg, kg, cos_t, sin_t)


def _swa_kernel(qt_ref, k_ref, vt_ref, sink_ref, ot_ref):
    w = WINDOW
    s_len = qt_ref.shape[-1]
    kv_dim = k_ref.shape[-1]
    gw = SW_GROUP * w

    def one_block(n, j, first):
        nk = w if first else 2 * w
        k0 = 0 if first else pl.multiple_of((n - 1) * w, w)
        q0 = 0 if first else pl.multiple_of(n * w, w)
        kwin = k_ref[0, pl.ds(k0, nk), :]
        vwin = vt_ref[0, j * HEAD_DIM:(j + 1) * HEAD_DIM, pl.ds(k0, nk)]
        qcat = jnp.concatenate(
            [qt_ref[0, (SW_GROUP * j + g) * HEAD_DIM:(SW_GROUP * j + g + 1) * HEAD_DIM,
                    pl.ds(q0, w)] for g in range(SW_GROUP)], axis=1)
        pieces = []
        if j > 0:
            pieces.append(jnp.zeros((j * HEAD_DIM, gw), BF16))
        pieces.append(qcat)
        if (j + 1) * HEAD_DIM < kv_dim:
            pieces.append(jnp.zeros((kv_dim - (j + 1) * HEAD_DIM, gw), BF16))
        qz = jnp.concatenate(pieces, axis=0) if len(pieces) > 1 else qcat
        s = jnp.dot(kwin, qz, preferred_element_type=F32)
        kidx = lax.broadcasted_iota(jnp.int32, (nk, gw), 0)
        qidx = lax.broadcasted_iota(jnp.int32, (nk, gw), 1) & (w - 1)
        if first:
            valid = kidx <= qidx
        else:
            valid = (kidx > qidx) & (kidx <= qidx + w)
        s = jnp.where(valid, s, NEG_INF)
        sink = sink_ref[j:j + 1, :]
        m = jnp.maximum(jnp.max(s, axis=0, keepdims=True), sink)
        e = jnp.exp(s - m)
        den = jnp.sum(e, axis=0, keepdims=True) + jnp.exp(sink - m)
        p = (e * (1.0 / den)).astype(BF16)
        o = jnp.dot(vwin, p, preferred_element_type=F32)
        for g in range(SW_GROUP):
            h = SW_GROUP * j + g
            ot_ref[0, h * HEAD_DIM:(h + 1) * HEAD_DIM, pl.ds(q0, w)] = (
                o[:, g * w:(g + 1) * w].astype(BF16))

    for j in range(kv_dim // HEAD_DIM):
        one_block(0, j, True)

    def body(n, carry):
        for j in range(kv_dim // HEAD_DIM):
            one_block(n, j, False)
        return carry

    lax.fori_loop(1, s_len // w, body, 0)


def _swa(qt, k, vt, sink_rows):
    b, d, s = qt.shape
    kv_dim = k.shape[-1]
    return pl.pallas_call(
        _swa_kernel,
        grid=(b,),
        in_specs=[pl.BlockSpec((1, d, s), lambda i: (i, 0, 0)),
                  pl.BlockSpec((1, s, kv_dim), lambda i: (i, 0, 0)),
                  pl.BlockSpec((1, kv_dim, s), lambda i: (i, 0, 0)),
                  _resident(sink_rows.shape)],
        out_specs=pl.BlockSpec((1, d, s), lambda i: (i, 0, 0)),
        out_shape=jax.ShapeDtypeStruct((b, d, s), BF16),
        compiler_params=pltpu.CompilerParams(
            dimension_semantics=("parallel",),
            vmem_limit_bytes=V7X_VMEM_LIMIT_BYTES),
        name="swa_attn",
    )(qt, k, vt, sink_rows)


def _interleave_gate_up(w_gate_up):
    d, two_h = w_gate_up.shape
    hidden = two_h // 2
    gate = w_gate_up[:, :hidden].reshape(d, hidden // FFN_CHUNK, FFN_CHUNK)
    up = w_gate_up[:, hidden:].reshape(d, hidden // FFN_CHUNK, FFN_CHUNK)
    return jnp.concatenate([gate, up], axis=2).reshape(d, two_h).astype(BF16)


def _lane_bcast(col, n):
    return jnp.broadcast_to(col.astype(F32).reshape(-1, 1), (col.size, n))


def kernel(x, positions, attn_norm, ffn_norm, w_gate_up, w_down, da_w_qkv, da_q_norm, da_k_norm,
           da_lambda, da_subln, da_w_o, kv_norm, w_kv, k_norm, sw_w_q, sw_q_norm, sw_sinks, sw_w_o):
    b, s, d = x.shape
    scale = 1.0 / math.sqrt(HEAD_DIM)

    inv = 1.0 / (ROPE_THETA ** (jnp.arange(0, HEAD_DIM, 2, dtype=F32) / HEAD_DIM))
    ang = positions.astype(F32)[..., None] * inv
    cos_t = jnp.swapaxes(jnp.cos(ang), 1, 2)
    sin_t = jnp.swapaxes(jnp.sin(ang), 1, 2)

    lambda_init = 0.8 - 0.6 * math.exp(-0.3 * 0)
    xq = _qkv0(x, attn_norm[0].reshape(1, d), da_w_qkv[0].T.astype(BF16),
               _lane_bcast(da_q_norm[0] * scale, ROW_TILE), _lane_bcast(da_k_norm[0], ROW_TILE),
               cos_t, sin_t)
    at = _diff_attn(da_lambda[0].astype(F32), xq[0], xq[1], xq[2],
                    _lane_bcast(da_subln[0], ATTN_BLOCK), lambda_init)
    x = _proj_ffn(x, at, da_w_o[0].astype(BF16), ffn_norm[0].reshape(1, d),
                  _interleave_gate_up(w_gate_up[0]), w_down[0].astype(BF16))

    q1t, k1, v1t = _qkv1(x, attn_norm[1].reshape(1, d), kv_norm.reshape(1, d),
                         sw_w_q[0].T.astype(BF16), w_kv.T.astype(BF16),
                         _lane_bcast(sw_q_norm[0] * scale, ROW_TILE), _lane_bcast(k_norm, ROW_TILE),
                         cos_t, sin_t)
    sink_rows = jnp.repeat(sw_sinks[0].astype(F32).reshape(SW_KV_HEADS, SW_GROUP), WINDOW, axis=1)
    at = _swa(q1t, k1, v1t, sink_rows)
    x = _proj_ffn(x, at, sw_w_o[0].astype(BF16), ffn_norm[1].reshape(1, d),
                  _interleave_gate_up(w_gate_up[1]), w_down[1].astype(BF16))
    return x
```

```python
import functools
import math

import jax
import jax.numpy as jnp
from jax import lax
from jax.experimental import pallas as pl
from jax.experimental.pallas import tpu as pltpu

HEAD_DIM = 64
ROPE_THETA = 10000.0
NORM_EPS = 1e-6
NEG_INF = -1e30
WINDOW = 128
SW_KV_HEADS = 4
SW_GROUP = 4

F32 = jnp.float32
BF16 = jnp.bfloat16

V7X_VMEM_LIMIT_BYTES = 56 * 1024 * 1024

ROW_TILE = 512
FEATURE_CHUNK = 512
FFN_CHUNK = 256
ATTN_BLOCK = 256
LOOKAHEAD = 4

NT_DIMS = (((1,), (1,)), ((), ()))
TN_DIMS = (((0,), (0,)), ((), ()))


def _resident(shape):
    nd = len(shape)
    return pl.BlockSpec(shape, lambda *_: (0,) * nd, pipeline_mode=pl.Buffered(1))


def _rms_scale(x):
    return lax.rsqrt(jnp.mean(x * x, axis=-1, keepdims=True) + NORM_EPS)


def _norm_rope_t(t, gain, cos, sin):
    r = lax.rsqrt(jnp.mean(t * t, axis=0, keepdims=True) + NORM_EPS)
    tn = t * r * gain
    x1 = tn[: HEAD_DIM // 2]
    x2 = tn[HEAD_DIM // 2:]
    return jnp.concatenate([x1 * cos - x2 * sin, x2 * cos + x1 * sin], axis=0)


def _qkv0_kernel(x_ref, g_ref, wt_ref, qg_ref, kg_ref, cos_ref, sin_ref,
                 qt_ref, k_ref, vt_ref):
    d_model = x_ref.shape[-1]
    x = x_ref[0]
    h = (x * _rms_scale(x) * g_ref[...]).astype(BF16)
    cos = cos_ref[0]
    sin = sin_ref[0]
    n_chunks = d_model // FEATURE_CHUNK
    for c in range(3 * n_chunks):
        rows = slice(c * FEATURE_CHUNK, (c + 1) * FEATURE_CHUNK)
        res = lax.dot_general(wt_ref[rows, :], h, NT_DIMS, preferred_element_type=F32)
        kind, cc = divmod(c, n_chunks)
        out_rows = slice(cc * FEATURE_CHUNK, (cc + 1) * FEATURE_CHUNK)
        if kind == 2:
            vt_ref[0, out_rows, :] = res.astype(BF16)
            continue
        gain_ref = qg_ref if kind == 0 else kg_ref
        parts = []
        for g in range(FEATURE_CHUNK // HEAD_DIM):
            half = g % 2
            gain = gain_ref[half * HEAD_DIM:(half + 1) * HEAD_DIM, :]
            parts.append(_norm_rope_t(res[g * HEAD_DIM:(g + 1) * HEAD_DIM], gain, cos, sin))
        out = jnp.concatenate(parts, axis=0)
        if kind == 0:
            qt_ref[0, out_rows, :] = out.astype(BF16)
        else:
            k_ref[0, :, out_rows] = out.T.astype(BF16)


def _qkv0(x, g, wt, qg, kg, cos_t, sin_t):
    b, s, d = x.shape
    tm = ROW_TILE
    grid = (b, s // tm)
    feat = pl.BlockSpec((1, d, tm), lambda i, j: (i, 0, j))
    tok = pl.BlockSpec((1, tm, d), lambda i, j: (i, j, 0))
    rope = pl.BlockSpec((1, HEAD_DIM // 2, tm), lambda i, j: (i, 0, j))
    return pl.pallas_call(
        _qkv0_kernel,
        grid=grid,
        in_specs=[tok, _resident((1, d)), _resident(wt.shape), _resident(qg.shape),
                  _resident(kg.shape), rope, rope],
        out_specs=[feat, tok, feat],
        out_shape=[jax.ShapeDtypeStruct((b, d, s), BF16),
                   jax.ShapeDtypeStruct((b, s, d), BF16),
                   jax.ShapeDtypeStruct((b, d, s), BF16)],
        compiler_params=pltpu.CompilerParams(
            dimension_semantics=("parallel", "parallel"),
            vmem_limit_bytes=V7X_VMEM_LIMIT_BYTES),
        name="qkv0_proj",
    )(x, g, wt, qg, kg, cos_t, sin_t)


def _diff_attn_kernel(lam_ref, qt_ref, k_ref, vt_ref, sg_ref, ot_ref,
                      qz_ref, m_ref, l_ref, acc_ref, *, lambda_init):
    bq = qt_ref.shape[-1]
    bk = bq
    hd2 = 2 * HEAD_DIM
    n_heads = qt_ref.shape[1] // hd2
    i = pl.program_id(1)

    lp = lam_ref[...]
    lam = (jnp.exp(jnp.sum(lp[0:1] * lp[1:2], axis=-1, keepdims=True))
           - jnp.exp(jnp.sum(lp[2:3] * lp[3:4], axis=-1, keepdims=True)) + lambda_init)

    zero = jnp.zeros((HEAD_DIM, bq), BF16)
    for h in range(n_heads):
        q0 = qt_ref[0, h * hd2:h * hd2 + HEAD_DIM, :]
        q1 = qt_ref[0, h * hd2 + HEAD_DIM:(h + 1) * hd2, :]
        qz_ref[2 * h] = jnp.concatenate([q0, zero], axis=0)
        qz_ref[2 * h + 1] = jnp.concatenate([zero, q1], axis=0)
    m_ref[...] = jnp.full(m_ref.shape, NEG_INF, F32)
    l_ref[...] = jnp.zeros(l_ref.shape, F32)
    acc_ref[...] = jnp.zeros(acc_ref.shape, F32)

    def block(j, masked):
        k0 = pl.multiple_of(j * bk, bk)
        if masked:
            kidx = lax.broadcasted_iota(jnp.int32, (bk, bq), 0)
            qidx = lax.broadcasted_iota(jnp.int32, (bk, bq), 1)
            keep = kidx <= qidx
        def scores(hc):
            h = hc // 2
            kblk = k_ref[0, pl.ds(k0, bk), h * hd2:(h + 1) * hd2]
            return jnp.dot(kblk, qz_ref[hc], preferred_element_type=F32)

        pending = [scores(hc) for hc in range(LOOKAHEAD)]
        for hc in range(2 * n_heads):
            h = hc // 2
            s = pending.pop(0)
            if hc + LOOKAHEAD < 2 * n_heads:
                pending.append(scores(hc + LOOKAHEAD))
            vblk = vt_ref[0, h * hd2:(h + 1) * hd2, pl.ds(k0, bk)]
            if masked:
                s = jnp.where(keep, s, NEG_INF)
            m_old = m_ref[hc]
            m_new = jnp.maximum(m_old, jnp.max(s, axis=0, keepdims=True))
            alpha = jnp.exp(m_old - m_new)
            p = jnp.exp(s - m_new)
            l_ref[hc] = alpha * l_ref[hc] + jnp.sum(p, axis=0, keepdims=True)
            acc_ref[hc] = alpha * acc_ref[hc] + jnp.dot(
                vblk, p.astype(BF16), preferred_element_type=F32)
            m_ref[hc] = m_new

    def body(j, carry):
        block(j, False)
        return carry

    lax.fori_loop(0, i, body, 0)
    block(i, True)

    for h in range(n_heads):
        o = (acc_ref[2 * h] / l_ref[2 * h]
             - lam * (acc_ref[2 * h + 1] / l_ref[2 * h + 1]))
        r = lax.rsqrt(jnp.mean(o * o, axis=0, keepdims=True) + NORM_EPS)
        ot_ref[0, h * hd2:(h + 1) * hd2, :] = (
            o * r * sg_ref[...] * (1.0 - lambda_init)).astype(BF16)


def _diff_attn(lam_p, qt, k, vt, sg, lambda_init):
    b, d, s = qt.shape
    hd2 = 2 * HEAD_DIM
    bq = ATTN_BLOCK
    n_half = d // HEAD_DIM
    return pl.pallas_call(
        functools.partial(_diff_attn_kernel, lambda_init=lambda_init),
        grid=(b, s // bq),
        in_specs=[_resident(lam_p.shape),
                  pl.BlockSpec((1, d, bq), lambda bi, i: (bi, 0, i)),
                  pl.BlockSpec((1, s, d), lambda bi, i: (bi, 0, 0)),
                  pl.BlockSpec((1, d, s), lambda bi, i: (bi, 0, 0)),
                  _resident(sg.shape)],
        out_specs=pl.BlockSpec((1, d, bq), lambda bi, i: (bi, 0, i)),
        out_shape=jax.ShapeDtypeStruct((b, d, s), BF16),
        scratch_shapes=[pltpu.VMEM((n_half, hd2, bq), BF16),
                        pltpu.VMEM((n_half, 1, bq), F32),
                        pltpu.VMEM((n_half, 1, bq), F32),
                        pltpu.VMEM((n_half, hd2, bq), F32)],
        compiler_params=pltpu.CompilerParams(
            dimension_semantics=("parallel", "arbitrary"),
            vmem_limit_bytes=V7X_VMEM_LIMIT_BYTES),
        name="diff_attn",
    )(lam_p, qt, k, vt, sg)


def _proj_ffn_kernel(x_ref, at_ref, wo_ref, g_ref, wgu_ref, wd_ref, o_ref, hid_ref):
    x = x_ref[0] + lax.dot_general(at_ref[0], wo_ref[...], TN_DIMS,
                                   preferred_element_type=F32)
    h = (x * _rms_scale(x) * g_ref[...]).astype(BF16)
    n_chunks = hid_ref.shape[-1] // FFN_CHUNK
    for c in range(n_chunks):
        gu = jnp.dot(h, wgu_ref[:, 2 * c * FFN_CHUNK:2 * (c + 1) * FFN_CHUNK],
                     preferred_element_type=F32)
        gate = gu[:, :FFN_CHUNK]
        up = gu[:, FFN_CHUNK:]
        hid_ref[:, c * FFN_CHUNK:(c + 1) * FFN_CHUNK] = (
            gate * jax.nn.sigmoid(gate) * up).astype(BF16)
    o_ref[0] = x + jnp.dot(hid_ref[...], wd_ref[...], preferred_element_type=F32)


def _proj_ffn(x, at, wo, g, wgu, wd):
    b, s, d = x.shape
    tm = ROW_TILE
    hidden = wd.shape[0]
    tok = pl.BlockSpec((1, tm, d), lambda i, j: (i, j, 0))
    feat = pl.BlockSpec((1, d, tm), lambda i, j: (i, 0, j))
    return pl.pallas_call(
        _proj_ffn_kernel,
        grid=(b, s // tm),
        in_specs=[tok, feat, _resident(wo.shape), _resident((1, d)), _resident(wgu.shape),
                  _resident(wd.shape)],
        out_specs=tok,
        out_shape=jax.ShapeDtypeStruct((b, s, d), F32),
        scratch_shapes=[pltpu.VMEM((tm, hidden), BF16)],
        compiler_params=pltpu.CompilerParams(
            dimension_semantics=("parallel", "parallel"),
            vmem_limit_bytes=V7X_VMEM_LIMIT_BYTES),
        name="proj_ffn",
    )(x, at, wo, g, wgu, wd)


def _qkv1_kernel(x_ref, ga_ref, gkv_ref, wqt_ref, wkvt_ref, qg_ref, kg_ref, cos_ref, sin_ref,
                 qt_ref, k_ref, vt_ref):
    d_model = x_ref.shape[-1]
    kv_dim = k_ref.shape[-1]
    x = x_ref[0]
    xn = x * _rms_scale(x)
    h_a = (xn * ga_ref[...]).astype(BF16)
    h_kv = (xn * gkv_ref[...]).astype(BF16)
    cos = cos_ref[0]
    sin = sin_ref[0]

    kv = lax.dot_general(wkvt_ref[...], h_kv, NT_DIMS, preferred_element_type=F32)
    kparts = [_norm_rope_t(kv[g * HEAD_DIM:(g + 1) * HEAD_DIM], kg_ref[...], cos, sin)
              for g in range(kv_dim // HEAD_DIM)]
    k_ref[0] = jnp.concatenate(kparts, axis=0).T.astype(BF16)
    vt_ref[0] = kv[kv_dim:].astype(BF16)

    for c in range(d_model // FEATURE_CHUNK):
        rows = slice(c * FEATURE_CHUNK, (c + 1) * FEATURE_CHUNK)
        res = lax.dot_general(wqt_ref[rows, :], h_a, NT_DIMS, preferred_element_type=F32)
        parts = [_norm_rope_t(res[g * HEAD_DIM:(g + 1) * HEAD_DIM], qg_ref[...], cos, sin)
                 for g in range(FEATURE_CHUNK // HEAD_DIM)]
        qt_ref[0, rows, :] = jnp.concatenate(parts, axis=0).astype(BF16)


def _qkv1(x, ga, gkv, wqt, wkvt, qg, kg, cos_t, sin_t):
    b, s, d = x.shape
    tm = ROW_TILE
    kv_dim = wkvt.shape[0] // 2
    tok = pl.BlockSpec((1, tm, d), lambda i, j: (i, j, 0))
    rope = pl.BlockSpec((1, HEAD_DIM // 2, tm), lambda i, j: (i, 0, j))
    return pl.pallas_call(
        _qkv1_kernel,
        grid=(b, s // tm),
        in_specs=[tok, _resident((1, d)), _resident((1, d)), _resident(wqt.shape),
                  _resident(wkvt.shape), _resident(qg.shape), _resident(kg.shape), rope, rope],
        out_specs=[pl.BlockSpec((1, d, tm), lambda i, j: (i, 0, j)),
                   pl.BlockSpec((1, tm, kv_dim), lambda i, j: (i, j, 0)),
                   pl.BlockSpec((1, kv_dim, tm), lambda i, j: (i, 0, j))],
        out_shape=[jax.ShapeDtypeStruct((b, d, s), BF16),
                   jax.ShapeDtypeStruct((b, s, kv_dim), BF16),
                   jax.ShapeDtypeStruct((b, kv_dim, s), BF16)],
        compiler_params=pltpu.CompilerParams(
            dimension_semantics=("parallel", "parallel"),
            vmem_limit_bytes=V7X_VMEM_LIMIT_BYTES),
        name="qkv1_proj",
    )(x, ga, gkv, wqt, wkvt, qg, kg, cos_t, sin_t)


def _swa_kernel(qt_ref, k_ref, vt_ref, sink_ref, ot_ref):
    w = WINDOW
    s_len = qt_ref.shape[-1]
    kv_dim = k_ref.shape[-1]
    gw = SW_GROUP * w

    def one_block(n, j, first):
        nk = w if first else 2 * w
        k0 = 0 if first else pl.multiple_of((n - 1) * w, w)
        q0 = 0 if first else pl.multiple_of(n * w, w)
        kwin = k_ref[0, pl.ds(k0, nk), :]
        vwin = vt_ref[0, j * HEAD_DIM:(j + 1) * HEAD_DIM, pl.ds(k0, nk)]
        qcat = jnp.concatenate(
            [qt_ref[0, (SW_GROUP * j + g) * HEAD_DIM:(SW_GROUP * j + g + 1) * HEAD_DIM,
                    pl.ds(q0, w)] for g in range(SW_GROUP)], axis=1)
        pieces = []
        if j > 0:
            pieces.append(jnp.zeros((j * HEAD_DIM, gw), BF16))
        pieces.append(qcat)
        if (j + 1) * HEAD_DIM < kv_dim:
            pieces.append(jnp.zeros((kv_dim - (j + 1) * HEAD_DIM, gw), BF16))
        qz = jnp.concatenate(pieces, axis=0) if len(pieces) > 1 else qcat
        s = jnp.dot(kwin, qz, preferred_element_type=F32)
        kidx = lax.broadcasted_iota(jnp.int32, (nk, gw), 0)
        qidx = lax.broadcasted_iota(jnp.int32, (nk, gw), 1) & (w - 1)
        if first:
            valid = kidx <= qidx
        else:
            valid = (kidx > qidx) & (kidx <= qidx + w)
        s = jnp.where(valid, s, NEG_INF)
        sink = sink_ref[j:j + 1, :]
        m = jnp.maximum(jnp.max(s, axis=0, keepdims=True), sink)
        e = jnp.exp(s - m)
        den = jnp.sum(e, axis=0, keepdims=True) + jnp.exp(sink - m)
        p = (e * (1.0 / den)).astype(BF16)
        o = jnp.dot(vwin, p, preferred_element_type=F32)
        for g in range(SW_GROUP):
            h = SW_GROUP * j + g
            ot_ref[0, h * HEAD_DIM:(h + 1) * HEAD_DIM, pl.ds(q0, w)] = (
                o[:, g * w:(g + 1) * w].astype(BF16))

    for j in range(kv_dim // HEAD_DIM):
        one_block(0, j, True)

    def body(n, carry):
        for j in range(kv_dim // HEAD_DIM):
            one_block(n, j, False)
        return carry

    lax.fori_loop(1, s_len // w, body, 0)


def _swa(qt, k, vt, sink_rows):
    b, d, s = qt.shape
    kv_dim = k.shape[-1]
    return pl.pallas_call(
        _swa_kernel,
        grid=(b,),
        in_specs=[pl.BlockSpec((1, d, s), lambda i: (i, 0, 0)),
                  pl.BlockSpec((1, s, kv_dim), lambda i: (i, 0, 0)),
                  pl.BlockSpec((1, kv_dim, s), lambda i: (i, 0, 0)),
                  _resident(sink_rows.shape)],
        out_specs=pl.BlockSpec((1, d, s), lambda i: (i, 0, 0)),
        out_shape=jax.ShapeDtypeStruct((b, d, s), BF16),
        compiler_params=pltpu.CompilerParams(
            dimension_semantics=("parallel",),
            vmem_limit_bytes=V7X_VMEM_LIMIT_BYTES),
        name="swa_attn",
    )(qt, k, vt, sink_rows)


def _interleave_gate_up(w_gate_up):
    d, two_h = w_gate_up.shape
    hidden = two_h // 2
    gate = w_gate_up[:, :hidden].reshape(d, hidden // FFN_CHUNK, FFN_CHUNK)
    up = w_gate_up[:, hidden:].reshape(d, hidden // FFN_CHUNK, FFN_CHUNK)
    return jnp.concatenate([gate, up], axis=2).reshape(d, two_h).astype(BF16)


def _lane_bcast(col, n):
    return jnp.broadcast_to(col.astype(F32).reshape(-1, 1), (col.size, n))


def kernel(x, positions, attn_norm, ffn_norm, w_gate_up, w_down, da_w_qkv, da_q_norm, da_k_norm,
           da_lambda, da_subln, da_w_o, kv_norm, w_kv, k_norm, sw_w_q, sw_q_norm, sw_sinks, sw_w_o):
    b, s, d = x.shape
    scale = 1.0 / math.sqrt(HEAD_DIM)

    inv = 1.0 / (ROPE_THETA ** (jnp.arange(0, HEAD_DIM, 2, dtype=F32) / HEAD_DIM))
    ang = positions.astype(F32)[..., None] * inv
    cos_t = jnp.swapaxes(jnp.cos(ang), 1, 2)
    sin_t = jnp.swapaxes(jnp.sin(ang), 1, 2)

    lambda_init = 0.8 - 0.6 * math.exp(-0.3 * 0)
    xq = _qkv0(x, attn_norm[0].reshape(1, d), da_w_qkv[0].T.astype(BF16),
               _lane_bcast(da_q_norm[0] * scale, ROW_TILE), _lane_bcast(da_k_norm[0], ROW_TILE),
               cos_t, sin_t)
    at = _diff_attn(da_lambda[0].astype(F32), xq[0], xq[1], xq[2],
                    _lane_bcast(da_subln[0], ATTN_BLOCK), lambda_init)
    x = _proj_ffn(x, at, da_w_o[0].astype(BF16), ffn_norm[0].reshape(1, d),
                  _interleave_gate_up(w_gate_up[0]), w_down[0].astype(BF16))

    q1t, k1, v1t = _qkv1(x, attn_norm[1].reshape(1, d), kv_norm.reshape(1, d),
                         sw_w_q[0].T.astype(BF16), w_kv.T.astype(BF16),
                         _lane_bcast(sw_q_norm[0] * scale, ROW_TILE), _lane_bcast(k_norm, ROW_TILE),
                         cos_t, sin_t)
    sink_rows = jnp.repeat(sw_sinks[0].astype(F32).reshape(SW_KV_HEADS, SW_GROUP), WINDOW, axis=1)
    at = _swa(q1t, k1, v1t, sink_rows)
    x = _proj_ffn(x, at, sw_w_o[0].astype(BF16), ffn_norm[1].reshape(1, d),
                  _interleave_gate_up(w_gate_up[1]), w_down[1].astype(BF16))
    return x
```

```python
import functools
import math

import jax
import jax.numpy as jnp
from jax import lax
from jax.experimental import pallas as pl
from jax.experimental.pallas import tpu as pltpu

HEAD_DIM = 64
ROPE_THETA = 10000.0
NORM_EPS = 1e-6
NEG_INF = -1e30
LOG2_E = math.log2(math.e)
WINDOW = 128
SW_KV_HEADS = 4
SW_GROUP = 4
SW_PACK = 2

F32 = jnp.float32
BF16 = jnp.bfloat16

V7X_VMEM_LIMIT_BYTES = 56 * 1024 * 1024

ROW_TILE = 512
FEATURE_CHUNK = 512
FFN_CHUNK = 256
ATTN_BLOCK = 256
ONES_ROWS = 16
LOOKAHEAD = 4
SWA_UNROLL = 3

NT_DIMS = (((1,), (1,)), ((), ()))
TN_DIMS = (((0,), (0,)), ((), ()))


def _resident(shape):
    nd = len(shape)
    return pl.BlockSpec(shape, lambda *_: (0,) * nd, pipeline_mode=pl.Buffered(1))


def _rms_scale(x):
    return lax.rsqrt(jnp.mean(x * x, axis=-1, keepdims=True) + NORM_EPS)


def _norm_rope_t(t, gain, cos, sin):
    r = lax.rsqrt(jnp.mean(t * t, axis=0, keepdims=True) + NORM_EPS)
    tn = t * r * gain
    x1 = tn[: HEAD_DIM // 2]
    x2 = tn[HEAD_DIM // 2:]
    return jnp.concatenate([x1 * cos - x2 * sin, x2 * cos + x1 * sin], axis=0)


def _qkv0_kernel(x_ref, g_ref, wt_ref, qg_ref, kg_ref, cos_ref, sin_ref,
                 qt_ref, k_ref, vt_ref, res_ref):
    d_model = x_ref.shape[-1]
    x = x_ref[0]
    h = (x * _rms_scale(x) * g_ref[...]).astype(BF16)
    cos = cos_ref[0]
    sin = sin_ref[0]
    n_chunks = d_model // FEATURE_CHUNK

    def project(c):
        rows = slice(c * FEATURE_CHUNK, (c + 1) * FEATURE_CHUNK)
        res_ref[c % 2] = lax.dot_general(wt_ref[rows, :], h, NT_DIMS,
                                         preferred_element_type=F32)

    project(0)
    for c in range(3 * n_chunks):
        if c + 1 < 3 * n_chunks:
            project(c + 1)
        res = res_ref.at[c % 2]
        kind, cc = divmod(c, n_chunks)
        out_rows = slice(cc * FEATURE_CHUNK, (cc + 1) * FEATURE_CHUNK)
        if kind == 2:
            vt_ref[0, out_rows, :] = res[...].astype(BF16)
            continue
        gain_ref = qg_ref if kind == 0 else kg_ref
        parts = []
        for g in range(FEATURE_CHUNK // HEAD_DIM):
            half = g % 2
            gain = gain_ref[half * HEAD_DIM:(half + 1) * HEAD_DIM, :]
            parts.append(_norm_rope_t(res[g * HEAD_DIM:(g + 1) * HEAD_DIM, :], gain, cos, sin))
        out = jnp.concatenate(parts, axis=0)
        if kind == 0:
            qt_ref[0, out_rows, :] = out.astype(BF16)
        else:
            k_ref[0, :, out_rows] = out.T.astype(BF16)


def _qkv0(x, g, wt, qg, kg, cos_t, sin_t):
    b, s, d = x.shape
    tm = ROW_TILE
    grid = (b, s // tm)
    feat = pl.BlockSpec((1, d, tm), lambda i, j: (i, 0, j))
    tok = pl.BlockSpec((1, tm, d), lambda i, j: (i, j, 0))
    rope = pl.BlockSpec((1, HEAD_DIM // 2, tm), lambda i, j: (i, 0, j))
    return pl.pallas_call(
        _qkv0_kernel,
        grid=grid,
        in_specs=[tok, _resident((1, d)), _resident(wt.shape), _resident(qg.shape),
                  _resident(kg.shape), rope, rope],
        out_specs=[feat, tok, feat],
        out_shape=[jax.ShapeDtypeStruct((b, d, s), BF16),
                   jax.ShapeDtypeStruct((b, s, d), BF16),
                   jax.ShapeDtypeStruct((b, d, s), BF16)],
        scratch_shapes=[pltpu.VMEM((2, FEATURE_CHUNK, tm), F32)],
        compiler_params=pltpu.CompilerParams(
            dimension_semantics=("parallel", "parallel"),
            vmem_limit_bytes=V7X_VMEM_LIMIT_BYTES),
        name="qkv0_proj",
    )(x, g, wt, qg, kg, cos_t, sin_t)


def _diff_attn_kernel(lam_ref, qt_ref, k_ref, vt_ref, sg_ref, ot_ref,
                      qz_ref, m_ref, acc_ref, s_ref, *, lambda_init):
    bq = qt_ref.shape[-1]
    hd2 = 2 * HEAD_DIM
    n_heads = qt_ref.shape[1] // hd2
    i = pl.program_id(1)

    lp = lam_ref[...]
    lam = (jnp.exp(jnp.sum(lp[0:1] * lp[1:2], axis=-1, keepdims=True))
           - jnp.exp(jnp.sum(lp[2:3] * lp[3:4], axis=-1, keepdims=True)) + lambda_init)

    zero = jnp.zeros((HEAD_DIM, bq), BF16)
    for h in range(n_heads):
        q0 = qt_ref[0, h * hd2:h * hd2 + HEAD_DIM, :]
        q1 = qt_ref[0, h * hd2 + HEAD_DIM:(h + 1) * hd2, :]
        qz_ref[2 * h] = jnp.concatenate([q0, zero], axis=0)
        qz_ref[2 * h + 1] = jnp.concatenate([zero, q1], axis=0)
    m_ref[...] = jnp.full(m_ref.shape, NEG_INF, F32)
    acc_ref[...] = jnp.zeros(acc_ref.shape, F32)

    def block(k0, bk, masked):
        ones = jnp.ones((ONES_ROWS, bk), BF16)
        if masked:
            kidx = lax.broadcasted_iota(jnp.int32, (bk, bq), 0)
            qidx = lax.broadcasted_iota(jnp.int32, (bk, bq), 1)
            keep = kidx <= qidx

        def scores(hc):
            h = hc // 2
            kblk = k_ref[0, pl.ds(k0, bk), h * hd2:(h + 1) * hd2]
            return jnp.dot(kblk, qz_ref[hc], preferred_element_type=F32)

        n_slots = s_ref.shape[0]

        def park(hc):
            s_ref[hc % n_slots, :bk, :] = scores(hc)

        for hc in range(LOOKAHEAD):
            park(hc)
        for hc in range(2 * n_heads):
            h = hc // 2
            if hc + LOOKAHEAD < 2 * n_heads:
                park(hc + LOOKAHEAD)
            s = s_ref[hc % n_slots, :bk, :]
            vext = jnp.concatenate(
                [vt_ref[0, h * hd2:(h + 1) * hd2, pl.ds(k0, bk)], ones], axis=0)
            if masked:
                s = jnp.where(keep, s, NEG_INF)
            m_old = m_ref[hc]
            m_new = jnp.maximum(m_old, jnp.max(s, axis=0, keepdims=True))
            alpha = jnp.exp2(m_old - m_new)
            p = jnp.exp2(s - m_new)
            acc_ref[hc] = alpha * acc_ref[hc] + jnp.dot(
                vext, p.astype(BF16), preferred_element_type=F32)
            m_ref[hc] = m_new

    def body(j, carry):
        block(pl.multiple_of(j * 2 * bq, 2 * bq), 2 * bq, False)
        return carry

    lax.fori_loop(0, lax.shift_right_logical(i, 1), body, 0)

    @pl.when((i & 1) == 1)
    def _():
        block(pl.multiple_of((i - 1) * bq, bq), bq, False)

    block(pl.multiple_of(i * bq, bq), bq, True)

    for h in range(n_heads):
        a0 = acc_ref[2 * h]
        a1 = acc_ref[2 * h + 1]
        o = (a0[:hd2] / a0[hd2:hd2 + 1]
             - lam * (a1[:hd2] / a1[hd2:hd2 + 1]))
        r = lax.rsqrt(jnp.mean(o * o, axis=0, keepdims=True) + NORM_EPS)
        ot_ref[0, h * hd2:(h + 1) * hd2, :] = (
            o * r * sg_ref[...] * (1.0 - lambda_init)).astype(BF16)


def _diff_attn(lam_p, qt, k, vt, sg, lambda_init):
    b, d, s = qt.shape
    hd2 = 2 * HEAD_DIM
    bq = ATTN_BLOCK
    n_half = d // HEAD_DIM
    return pl.pallas_call(
        functools.partial(_diff_attn_kernel, lambda_init=lambda_init),
        grid=(b, s // bq),
        in_specs=[_resident(lam_p.shape),
                  pl.BlockSpec((1, d, bq), lambda bi, i: (bi, 0, i)),
                  pl.BlockSpec((1, s, d), lambda bi, i: (bi, 0, 0)),
                  pl.BlockSpec((1, d, s), lambda bi, i: (bi, 0, 0)),
                  _resident(sg.shape)],
        out_specs=pl.BlockSpec((1, d, bq), lambda bi, i: (bi, 0, i)),
        out_shape=jax.ShapeDtypeStruct((b, d, s), BF16),
        scratch_shapes=[pltpu.VMEM((n_half, hd2, bq), BF16),
                        pltpu.VMEM((n_half, 1, bq), F32),
                        pltpu.VMEM((n_half, hd2 + ONES_ROWS, bq), F32),
                        pltpu.VMEM((LOOKAHEAD + 1, 2 * bq, bq), F32)],
        compiler_params=pltpu.CompilerParams(
            dimension_semantics=("parallel", "arbitrary"),
            vmem_limit_bytes=V7X_VMEM_LIMIT_BYTES),
        name="diff_attn",
    )(lam_p, qt, k, vt, sg)


def _proj_ffn_kernel(x_ref, at_ref, wo_ref, g_ref, wgu_ref, wd_ref, o_ref, hid_ref):
    x = x_ref[0] + lax.dot_general(at_ref[0], wo_ref[...], TN_DIMS,
                                   preferred_element_type=F32)
    h = (x * _rms_scale(x) * g_ref[...]).astype(BF16)
    n_chunks = hid_ref.shape[-1] // FFN_CHUNK
    for c in range(n_chunks):
        gu = jnp.dot(h, wgu_ref[:, 2 * c * FFN_CHUNK:2 * (c + 1) * FFN_CHUNK],
                     preferred_element_type=F32)
        gate = gu[:, :FFN_CHUNK]
        up = gu[:, FFN_CHUNK:]
        hid_ref[:, c * FFN_CHUNK:(c + 1) * FFN_CHUNK] = (
            gate * jax.nn.sigmoid(gate) * up).astype(BF16)
    o_ref[0] = x + jnp.dot(hid_ref[...], wd_ref[...], preferred_element_type=F32)


def _proj_ffn(x, at, wo, g, wgu, wd):
    b, s, d = x.shape
    tm = ROW_TILE
    hidden = wd.shape[0]
    tok = pl.BlockSpec((1, tm, d), lambda i, j: (i, j, 0))
    feat = pl.BlockSpec((1, d, tm), lambda i, j: (i, 0, j))
    return pl.pallas_call(
        _proj_ffn_kernel,
        grid=(b, s // tm),
        in_specs=[tok, feat, _resident(wo.shape), _resident((1, d)), _resident(wgu.shape),
                  _resident(wd.shape)],
        out_specs=tok,
        out_shape=jax.ShapeDtypeStruct((b, s, d), F32),
        scratch_shapes=[pltpu.VMEM((tm, hidden), BF16)],
        compiler_params=pltpu.CompilerParams(
            dimension_semantics=("parallel", "parallel"),
            vmem_limit_bytes=V7X_VMEM_LIMIT_BYTES),
        name="proj_ffn",
    )(x, at, wo, g, wgu, wd)


def _qkv1_kernel(x_ref, ga_ref, gkv_ref, wqt_ref, wkvt_ref, qg_ref, kg_ref, cos_ref, sin_ref,
                 qt_ref, k_ref, vt_ref, kv_ref, res_ref):
    d_model = x_ref.shape[-1]
    kv_dim = k_ref.shape[-1]
    x = x_ref[0]
    xn = x * _rms_scale(x)
    h_a = (xn * ga_ref[...]).astype(BF16)
    h_kv = (xn * gkv_ref[...]).astype(BF16)
    cos = cos_ref[0]
    sin = sin_ref[0]

    n_chunks = d_model // FEATURE_CHUNK

    def project_q(c):
        rows = slice(c * FEATURE_CHUNK, (c + 1) * FEATURE_CHUNK)
        res_ref[c % 2] = lax.dot_general(wqt_ref[rows, :], h_a, NT_DIMS,
                                         preferred_element_type=F32)

    kv_ref[...] = lax.dot_general(wkvt_ref[...], h_kv, NT_DIMS, preferred_element_type=F32)
    project_q(0)
    kparts = [_norm_rope_t(kv_ref[g * HEAD_DIM:(g + 1) * HEAD_DIM, :], kg_ref[...], cos, sin)
              for g in range(kv_dim // HEAD_DIM)]
    k_ref[0] = jnp.concatenate(kparts, axis=0).T.astype(BF16)
    vt_ref[0] = kv_ref[kv_dim:, :].astype(BF16)

    for c in range(n_chunks):
        if c + 1 < n_chunks:
            project_q(c + 1)
        res = res_ref.at[c % 2]
        rows = slice(c * FEATURE_CHUNK, (c + 1) * FEATURE_CHUNK)
        parts = [_norm_rope_t(res[g * HEAD_DIM:(g + 1) * HEAD_DIM, :], qg_ref[...], cos, sin)
                 for g in range(FEATURE_CHUNK // HEAD_DIM)]
        qt_ref[0, rows, :] = jnp.concatenate(parts, axis=0).astype(BF16)


def _qkv1(x, ga, gkv, wqt, wkvt, qg, kg, cos_t, sin_t):
    b, s, d = x.shape
    tm = ROW_TILE
    kv_dim = wkvt.shape[0] // 2
    tok = pl.BlockSpec((1, tm, d), lambda i, j: (i, j, 0))
    rope = pl.BlockSpec((1, HEAD_DIM // 2, tm), lambda i, j: (i, 0, j))
    return pl.pallas_call(
        _qkv1_kernel,
        grid=(b, s // tm),
        in_specs=[tok, _resident((1, d)), _resident((1, d)), _resident(wqt.shape),
                  _resident(wkvt.shape), _resident(qg.shape), _resident(kg.shape), rope, rope],
        out_specs=[pl.BlockSpec((1, d, tm), lambda i, j: (i, 0, j)),
                   pl.BlockSpec((1, tm, kv_dim), lambda i, j: (i, j, 0)),
                   pl.BlockSpec((1, kv_dim, tm), lambda i, j: (i, 0, j))],
        out_shape=[jax.ShapeDtypeStruct((b, d, s), BF16),
                   jax.ShapeDtypeStruct((b, s, kv_dim), BF16),
                   jax.ShapeDtypeStruct((b, kv_dim, s), BF16)],
        scratch_shapes=[pltpu.VMEM((2 * kv_dim, tm), F32),
                        pltpu.VMEM((2, FEATURE_CHUNK, tm), F32)],
        compiler_params=pltpu.CompilerParams(
            dimension_semantics=("parallel", "parallel"),
            vmem_limit_bytes=V7X_VMEM_LIMIT_BYTES),
        name="qkv1_proj",
    )(x, ga, gkv, wqt, wkvt, qg, kg, cos_t, sin_t)


def _swa_kernel(qt_ref, k_ref, vt_ref, sink_ref, ot_ref, bias_ref, s_ref):
    w = WINDOW
    s_len = qt_ref.shape[-1]
    kv_dim = k_ref.shape[-1]
    gw = SW_PACK * w

    n_kv = kv_dim // HEAD_DIM
    n_blocks = s_len // w

    kidx = lax.broadcasted_iota(jnp.int32, (2 * w, gw), 0)
    qidx = lax.broadcasted_iota(jnp.int32, (2 * w, gw), 1) & (w - 1)
    bias_ref[...] = jnp.where((kidx > qidx) & (kidx <= qidx + w), 0.0, NEG_INF).astype(F32)

    def window(n, first):
        nk = w if first else 2 * w
        k0 = 0 if first else pl.multiple_of((n - 1) * w, w)
        q0 = 0 if first else pl.multiple_of(n * w, w)
        return nk, k0, q0

    def heads(j, part):
        return [SW_GROUP * j + SW_PACK * part + u for u in range(SW_PACK)]

    def scores(n, j, part, first):
        nk, k0, q0 = window(n, first)
        kwin = k_ref[0, pl.ds(k0, nk), :]
        qcat = jnp.concatenate(
            [qt_ref[0, h * HEAD_DIM:(h + 1) * HEAD_DIM, pl.ds(q0, w)]
             for h in heads(j, part)], axis=1)
        pieces = []
        if j > 0:
            pieces.append(jnp.zeros((j * HEAD_DIM, gw), BF16))
        pieces.append(qcat)
        if j + 1 < n_kv:
            pieces.append(jnp.zeros(((n_kv - j - 1) * HEAD_DIM, gw), BF16))
        qz = jnp.concatenate(pieces, axis=0) if len(pieces) > 1 else qcat
        return jnp.dot(kwin, qz, preferred_element_type=F32)

    def finish(n, j, part, first, s):
        nk, k0, q0 = window(n, first)
        s = s + bias_ref[2 * w - nk:, :]
        sink = sink_ref[j:j + 1, part * gw:(part + 1) * gw]
        m = jnp.maximum(jnp.max(s, axis=0, keepdims=True), sink)
        e = jnp.exp2(s - m).astype(BF16)
        vext = jnp.concatenate([vt_ref[0, j * HEAD_DIM:(j + 1) * HEAD_DIM, pl.ds(k0, nk)],
                                jnp.ones((ONES_ROWS, nk), BF16)], axis=0)
        o = jnp.dot(vext, e, preferred_element_type=F32)
        den = o[HEAD_DIM:HEAD_DIM + 1] + jnp.exp2(sink - m)
        o = o[:HEAD_DIM] * (1.0 / den)
        for u, h in enumerate(heads(j, part)):
            ot_ref[0, h * HEAD_DIM:(h + 1) * HEAD_DIM, pl.ds(q0, w)] = (
                o[:, u * w:(u + 1) * w].astype(BF16))

    def run(chains):
        n_slots = s_ref.shape[0]

        def park(t):
            nk = window(chains[t][0], chains[t][3])[0]
            s_ref[t % n_slots, :nk, :] = scores(*chains[t])

        for t in range(min(LOOKAHEAD, len(chains))):
            park(t)
        for t, ch in enumerate(chains):
            if t + LOOKAHEAD < len(chains):
                park(t + LOOKAHEAD)
            nk = window(ch[0], ch[3])[0]
            finish(*ch, s_ref[t % n_slots, :nk, :])

    parts = range(SW_GROUP // SW_PACK)
    run([(0, j, part, True) for j in range(n_kv) for part in parts])

    def body(t, carry):
        n0 = 1 + t * SWA_UNROLL
        run([(n0 + u, j, part, False)
             for u in range(SWA_UNROLL) for j in range(n_kv) for part in parts])
        return carry

    lax.fori_loop(0, (n_blocks - 1) // SWA_UNROLL, body, 0)


def _swa(qt, k, vt, sink_rows):
    b, d, s = qt.shape
    kv_dim = k.shape[-1]
    assert (s // WINDOW - 1) % SWA_UNROLL == 0
    return pl.pallas_call(
        _swa_kernel,
        grid=(b,),
        in_specs=[pl.BlockSpec((1, d, s), lambda i: (i, 0, 0)),
                  pl.BlockSpec((1, s, kv_dim), lambda i: (i, 0, 0)),
                  pl.BlockSpec((1, kv_dim, s), lambda i: (i, 0, 0)),
                  _resident(sink_rows.shape)],
        out_specs=pl.BlockSpec((1, d, s), lambda i: (i, 0, 0)),
        out_shape=jax.ShapeDtypeStruct((b, d, s), BF16),
        scratch_shapes=[pltpu.VMEM((2 * WINDOW, SW_PACK * WINDOW), F32),
                        pltpu.VMEM((LOOKAHEAD + 1, 2 * WINDOW, SW_PACK * WINDOW), F32)],
        compiler_params=pltpu.CompilerParams(
            dimension_semantics=("parallel",),
            vmem_limit_bytes=V7X_VMEM_LIMIT_BYTES),
        name="swa_attn",
    )(qt, k, vt, sink_rows)


def _interleave_gate_up(w_gate_up):
    d, two_h = w_gate_up.shape
    hidden = two_h // 2
    gate = w_gate_up[:, :hidden].reshape(d, hidden // FFN_CHUNK, FFN_CHUNK)
    up = w_gate_up[:, hidden:].reshape(d, hidden // FFN_CHUNK, FFN_CHUNK)
    return jnp.concatenate([gate, up], axis=2).reshape(d, two_h).astype(BF16)


def _lane_bcast(col, n):
    return jnp.broadcast_to(col.astype(F32).reshape(-1, 1), (col.size, n))


def kernel(x, positions, attn_norm, ffn_norm, w_gate_up, w_down, da_w_qkv, da_q_norm, da_k_norm,
           da_lambda, da_subln, da_w_o, kv_norm, w_kv, k_norm, sw_w_q, sw_q_norm, sw_sinks, sw_w_o):
    b, s, d = x.shape
    scale = 1.0 / math.sqrt(HEAD_DIM)

    inv = 1.0 / (ROPE_THETA ** (jnp.arange(0, HEAD_DIM, 2, dtype=F32) / HEAD_DIM))
    ang = positions.astype(F32)[..., None] * inv
    cos_t = jnp.swapaxes(jnp.cos(ang), 1, 2)
    sin_t = jnp.swapaxes(jnp.sin(ang), 1, 2)

    lambda_init = 0.8 - 0.6 * math.exp(-0.3 * 0)
    xq = _qkv0(x, attn_norm[0].reshape(1, d), da_w_qkv[0].T.astype(BF16),
               _lane_bcast(da_q_norm[0] * (scale * LOG2_E), ROW_TILE),
               _lane_bcast(da_k_norm[0], ROW_TILE),
               cos_t, sin_t)
    at = _diff_attn(da_lambda[0].astype(F32), xq[0], xq[1], xq[2],
                    _lane_bcast(da_subln[0], ATTN_BLOCK), lambda_init)
    x = _proj_ffn(x, at, da_w_o[0].astype(BF16), ffn_norm[0].reshape(1, d),
                  _interleave_gate_up(w_gate_up[0]), w_down[0].astype(BF16))

    q1t, k1, v1t = _qkv1(x, attn_norm[1].reshape(1, d), kv_norm.reshape(1, d),
                         sw_w_q[0].T.astype(BF16), w_kv.T.astype(BF16),
                         _lane_bcast(sw_q_norm[0] * (scale * LOG2_E), ROW_TILE),
                         _lane_bcast(k_norm, ROW_TILE), cos_t, sin_t)
    sink_rows = jnp.repeat((sw_sinks[0].astype(F32) * LOG2_E).reshape(SW_KV_HEADS, SW_GROUP),
                           WINDOW, axis=1)
    at = _swa(q1t, k1, v1t, sink_rows)
    x = _proj_ffn(x, at, sw_w_o[0].astype(BF16), ffn_norm[1].reshape(1, d),
                  _interleave_gate_up(w_gate_up[1]), w_down[1].astype(BF16))
    return x
```

```python
import functools
import math

import jax
import jax.numpy as jnp
from jax import lax
from jax.experimental import pallas as pl
from jax.experimental.pallas import tpu as pltpu

HEAD_DIM = 64
ROPE_THETA = 10000.0
NORM_EPS = 1e-6
NEG_INF = -1e30
LOG2_E = math.log2(math.e)
WINDOW = 128
SW_KV_HEADS = 4
SW_GROUP = 4
SW_PACK = 2

F32 = jnp.float32
BF16 = jnp.bfloat16

V7X_VMEM_LIMIT_BYTES = 56 * 1024 * 1024

ROW_TILE = 512
FEATURE_CHUNK = 512
FFN_CHUNK = 256
ATTN_BLOCK = 256
ONES_ROWS = 16
LOOKAHEAD = 4
SWA_UNROLL = 3

NT_DIMS = (((1,), (1,)), ((), ()))
TN_DIMS = (((0,), (0,)), ((), ()))


def _resident(shape):
    nd = len(shape)
    return pl.BlockSpec(shape, lambda *_: (0,) * nd, pipeline_mode=pl.Buffered(1))


def _rms_scale(x):
    return lax.rsqrt(jnp.mean(x * x, axis=-1, keepdims=True) + NORM_EPS)


def _pack_rows(x):
    return pltpu.bitcast(x, jnp.uint32)


def _unpack_rows(x):
    return pltpu.bitcast(x, BF16)


def _norm_rope_t(t, gain, cos, sin):
    r = lax.rsqrt(jnp.mean(t * t, axis=0, keepdims=True) + NORM_EPS)
    tn = t * r * gain
    x1 = tn[: HEAD_DIM // 2]
    x2 = tn[HEAD_DIM // 2:]
    return jnp.concatenate([x1 * cos - x2 * sin, x2 * cos + x1 * sin], axis=0)


def _qkv0_kernel(x_ref, g_ref, wt_ref, qg_ref, kg_ref, cos_ref, sin_ref,
                 qt_ref, k_ref, vt_ref, res_ref):
    d_model = x_ref.shape[-1]
    x = x_ref[0]
    h = (x * _rms_scale(x) * g_ref[...]).astype(BF16)
    cos = cos_ref[0]
    sin = sin_ref[0]
    n_chunks = d_model // FEATURE_CHUNK

    def project(c):
        rows = slice(c * FEATURE_CHUNK // 2, (c + 1) * FEATURE_CHUNK // 2)
        w_rows = _unpack_rows(wt_ref[rows, :])
        res_ref[c % 2] = lax.dot_general(w_rows, h, NT_DIMS, preferred_element_type=F32)

    project(0)
    for c in range(3 * n_chunks):
        if c + 1 < 3 * n_chunks:
            project(c + 1)
        res = res_ref.at[c % 2]
        kind, cc = divmod(c, n_chunks)
        out_rows = slice(cc * FEATURE_CHUNK, (cc + 1) * FEATURE_CHUNK)
        half_rows = slice(cc * FEATURE_CHUNK // 2, (cc + 1) * FEATURE_CHUNK // 2)
        if kind == 2:
            vt_ref[0, half_rows, :] = _pack_rows(res[...].astype(BF16))
            continue
        gain_ref = qg_ref if kind == 0 else kg_ref
        parts = []
        for g in range(FEATURE_CHUNK // HEAD_DIM):
            half = g % 2
            gain = gain_ref[half * HEAD_DIM:(half + 1) * HEAD_DIM, :]
            parts.append(_norm_rope_t(res[g * HEAD_DIM:(g + 1) * HEAD_DIM, :], gain, cos, sin))
        out = jnp.concatenate(parts, axis=0)
        if kind == 0:
            qt_ref[0, out_rows, :] = out.astype(BF16)
        else:
            k_ref[0, :, out_rows] = _pack_rows(out.T.astype(BF16))


def _qkv0(x, g, wt, qg, kg, cos_t, sin_t):
    b, s, d = x.shape
    tm = ROW_TILE
    grid = (b, s // tm)
    feat = pl.BlockSpec((1, d, tm), lambda i, j: (i, 0, j))
    feat_packed = pl.BlockSpec((1, d // 2, tm), lambda i, j: (i, 0, j))
    tok = pl.BlockSpec((1, tm, d), lambda i, j: (i, j, 0))
    tok_packed = pl.BlockSpec((1, tm // 2, d), lambda i, j: (i, j, 0))
    rope = pl.BlockSpec((1, HEAD_DIM // 2, tm), lambda i, j: (i, 0, j))
    return pl.pallas_call(
        _qkv0_kernel,
        grid=grid,
        in_specs=[tok, _resident((1, d)), _resident(wt.shape), _resident(qg.shape),
                  _resident(kg.shape), rope, rope],
        out_specs=[feat, tok_packed, feat_packed],
        out_shape=[jax.ShapeDtypeStruct((b, d, s), BF16),
                   jax.ShapeDtypeStruct((b, s // 2, d), jnp.uint32),
                   jax.ShapeDtypeStruct((b, d // 2, s), jnp.uint32)],
        scratch_shapes=[pltpu.VMEM((2, FEATURE_CHUNK, tm), F32)],
        compiler_params=pltpu.CompilerParams(
            dimension_semantics=("parallel", "parallel"),
            vmem_limit_bytes=V7X_VMEM_LIMIT_BYTES),
        name="qkv0_proj",
    )(x, g, wt, qg, kg, cos_t, sin_t)


def _diff_attn_kernel(lam_ref, qt_ref, k_ref, vt_ref, sg_ref, ot_ref,
                      qz_ref, m_ref, acc_ref, s_ref, *, lambda_init):
    bq = qt_ref.shape[-1]
    hd2 = 2 * HEAD_DIM
    n_heads = qt_ref.shape[1] // hd2
    i = pl.program_id(1)

    lp = lam_ref[...]
    lam = (jnp.exp(jnp.sum(lp[0:1] * lp[1:2], axis=-1, keepdims=True))
           - jnp.exp(jnp.sum(lp[2:3] * lp[3:4], axis=-1, keepdims=True)) + lambda_init)

    zero = jnp.zeros((HEAD_DIM, bq), BF16)
    for h in range(n_heads):
        q0 = qt_ref[0, h * hd2:h * hd2 + HEAD_DIM, :]
        q1 = qt_ref[0, h * hd2 + HEAD_DIM:(h + 1) * hd2, :]
        qz_ref[2 * h] = jnp.concatenate([q0, zero], axis=0)
        qz_ref[2 * h + 1] = jnp.concatenate([zero, q1], axis=0)
    m_ref[...] = jnp.full(m_ref.shape, NEG_INF, F32)
    acc_ref[...] = jnp.zeros(acc_ref.shape, F32)

    def block(k0, bk, masked):
        ones = jnp.ones((ONES_ROWS, bk), BF16)
        k0_packed = pl.multiple_of(lax.shift_right_logical(k0, 1), bk // 2)
        if masked:
            kidx = lax.broadcasted_iota(jnp.int32, (bk, bq), 0)
            qidx = lax.broadcasted_iota(jnp.int32, (bk, bq), 1)
            keep = kidx <= qidx

        def scores(hc):
            h = hc // 2
            kblk = _unpack_rows(k_ref[0, pl.ds(k0_packed, bk // 2), h * hd2:(h + 1) * hd2])
            return jnp.dot(kblk, qz_ref[hc], preferred_element_type=F32)

        n_slots = s_ref.shape[0]

        def park(hc):
            s_ref[hc % n_slots, :bk, :] = scores(hc)

        for hc in range(LOOKAHEAD):
            park(hc)
        for hc in range(2 * n_heads):
            h = hc // 2
            if hc + LOOKAHEAD < 2 * n_heads:
                park(hc + LOOKAHEAD)
            s = s_ref[hc % n_slots, :bk, :]
            vblk = _unpack_rows(vt_ref[0, h * HEAD_DIM:(h + 1) * HEAD_DIM, pl.ds(k0, bk)])
            vext = jnp.concatenate([vblk, ones], axis=0)
            if masked:
                s = jnp.where(keep, s, NEG_INF)
            m_old = m_ref[hc]
            m_new = jnp.maximum(m_old, jnp.max(s, axis=0, keepdims=True))
            alpha = jnp.exp2(m_old - m_new)
            p = jnp.exp2(s - m_new)
            acc_ref[hc] = alpha * acc_ref[hc] + jnp.dot(
                vext, p.astype(BF16), preferred_element_type=F32)
            m_ref[hc] = m_new

    def body(j, carry):
        block(pl.multiple_of(j * 2 * bq, 2 * bq), 2 * bq, False)
        return carry

    lax.fori_loop(0, lax.shift_right_logical(i, 1), body, 0)

    @pl.when((i & 1) == 1)
    def _():
        block(pl.multiple_of((i - 1) * bq, bq), bq, False)

    block(pl.multiple_of(i * bq, bq), bq, True)

    for h in range(n_heads):
        a0 = acc_ref[2 * h]
        a1 = acc_ref[2 * h + 1]
        o = (a0[:hd2] / a0[hd2:hd2 + 1]
             - lam * (a1[:hd2] / a1[hd2:hd2 + 1]))
        r = lax.rsqrt(jnp.mean(o * o, axis=0, keepdims=True) + NORM_EPS)
        ot_ref[0, h * hd2:(h + 1) * hd2, :] = (
            o * r * sg_ref[...] * (1.0 - lambda_init)).astype(BF16)


def _diff_attn(lam_p, qt, k, vt, sg, lambda_init):
    b, d, s = qt.shape
    hd2 = 2 * HEAD_DIM
    bq = ATTN_BLOCK
    n_half = d // HEAD_DIM
    return pl.pallas_call(
        functools.partial(_diff_attn_kernel, lambda_init=lambda_init),
        grid=(b, s // bq),
        in_specs=[_resident(lam_p.shape),
                  pl.BlockSpec((1, d, bq), lambda bi, i: (bi, 0, i)),
                  pl.BlockSpec((1, s // 2, d), lambda bi, i: (bi, 0, 0)),
                  pl.BlockSpec((1, d // 2, s), lambda bi, i: (bi, 0, 0)),
                  _resident(sg.shape)],
        out_specs=pl.BlockSpec((1, d, bq), lambda bi, i: (bi, 0, i)),
        out_shape=jax.ShapeDtypeStruct((b, d, s), BF16),
        scratch_shapes=[pltpu.VMEM((n_half, hd2, bq), BF16),
                        pltpu.VMEM((n_half, 1, bq), F32),
                        pltpu.VMEM((n_half, hd2 + ONES_ROWS, bq), F32),
                        pltpu.VMEM((LOOKAHEAD + 1, 2 * bq, bq), F32)],
        compiler_params=pltpu.CompilerParams(
            dimension_semantics=("parallel", "arbitrary"),
            vmem_limit_bytes=V7X_VMEM_LIMIT_BYTES),
        name="diff_attn",
    )(lam_p, qt, k, vt, sg)


def _proj_ffn_kernel(x_ref, at_ref, wo_ref, g_ref, wg_ref, wu_ref, wd_ref, o_ref, hid_ref):
    x = x_ref[0] + lax.dot_general(at_ref[0], wo_ref[...], TN_DIMS,
                                   preferred_element_type=F32)
    h = (x * _rms_scale(x) * g_ref[...]).astype(BF16)
    n_chunks = hid_ref.shape[-1] // FFN_CHUNK
    for c in range(n_chunks):
        cols = slice(c * FFN_CHUNK, (c + 1) * FFN_CHUNK)
        gate = jnp.dot(h, wg_ref[:, cols], preferred_element_type=F32)
        up = jnp.dot(h, wu_ref[:, cols], preferred_element_type=F32)
        hid_ref[:, c * FFN_CHUNK:(c + 1) * FFN_CHUNK] = (
            gate * jax.nn.sigmoid(gate) * up).astype(BF16)
    o_ref[0] = x + jnp.dot(hid_ref[...], wd_ref[...], preferred_element_type=F32)


def _proj_ffn(x, at, wo, g, wgu, wd):
    b, s, d = x.shape
    tm = ROW_TILE
    hidden = wd.shape[0]
    tok = pl.BlockSpec((1, tm, d), lambda i, j: (i, j, 0))
    feat = pl.BlockSpec((1, d, tm), lambda i, j: (i, 0, j))
    return pl.pallas_call(
        _proj_ffn_kernel,
        grid=(b, s // tm),
        in_specs=[tok, feat, _resident(wo.shape), _resident((1, d)),
                  pl.BlockSpec((d, hidden), lambda i, j: (0, 0), pipeline_mode=pl.Buffered(1)),
                  pl.BlockSpec((d, hidden), lambda i, j: (0, 1), pipeline_mode=pl.Buffered(1)),
                  _resident(wd.shape)],
        out_specs=tok,
        out_shape=jax.ShapeDtypeStruct((b, s, d), F32),
        scratch_shapes=[pltpu.VMEM((tm, hidden), BF16)],
        compiler_params=pltpu.CompilerParams(
            dimension_semantics=("parallel", "parallel"),
            vmem_limit_bytes=V7X_VMEM_LIMIT_BYTES),
        name="proj_ffn",
    )(x, at, wo, g, wgu, wgu, wd)


def _qkv1_kernel(x_ref, ga_ref, gkv_ref, wqt_ref, wkvt_ref, qg_ref, kg_ref, cos_ref, sin_ref,
                 qt_ref, k_ref, vt_ref, kv_ref, res_ref):
    d_model = x_ref.shape[-1]
    kv_dim = k_ref.shape[-1]
    x = x_ref[0]
    xn = x * _rms_scale(x)
    h_a = (xn * ga_ref[...]).astype(BF16)
    h_kv = (xn * gkv_ref[...]).astype(BF16)
    cos = cos_ref[0]
    sin = sin_ref[0]

    n_chunks = d_model // FEATURE_CHUNK

    def project_q(c):
        rows = slice(c * FEATURE_CHUNK // 2, (c + 1) * FEATURE_CHUNK // 2)
        res_ref[c % 2] = lax.dot_general(_unpack_rows(wqt_ref[rows, :]), h_a, NT_DIMS,
                                         preferred_element_type=F32)

    kv_ref[...] = lax.dot_general(_unpack_rows(wkvt_ref[...]), h_kv, NT_DIMS,
                                  preferred_element_type=F32)
    project_q(0)
    kparts = [_norm_rope_t(kv_ref[g * HEAD_DIM:(g + 1) * HEAD_DIM, :], kg_ref[...], cos, sin)
              for g in range(kv_dim // HEAD_DIM)]
    k_ref[0] = _pack_rows(jnp.concatenate(kparts, axis=0).T.astype(BF16))
    vt_ref[0] = _pack_rows(kv_ref[kv_dim:, :].astype(BF16))

    for c in range(n_chunks):
        if c + 1 < n_chunks:
            project_q(c + 1)
        res = res_ref.at[c % 2]
        rows = slice(c * FEATURE_CHUNK, (c + 1) * FEATURE_CHUNK)
        parts = [_norm_rope_t(res[g * HEAD_DIM:(g + 1) * HEAD_DIM, :], qg_ref[...], cos, sin)
                 for g in range(FEATURE_CHUNK // HEAD_DIM)]
        qt_ref[0, rows, :] = jnp.concatenate(parts, axis=0).astype(BF16)


def _qkv1(x, ga, gkv, wqt, wkvt, qg, kg, cos_t, sin_t):
    b, s, d = x.shape
    tm = ROW_TILE
    kv_dim = wkvt.shape[0]
    tok = pl.BlockSpec((1, tm, d), lambda i, j: (i, j, 0))
    rope = pl.BlockSpec((1, HEAD_DIM // 2, tm), lambda i, j: (i, 0, j))
    return pl.pallas_call(
        _qkv1_kernel,
        grid=(b, s // tm),
        in_specs=[tok, _resident((1, d)), _resident((1, d)), _resident(wqt.shape),
                  _resident(wkvt.shape), _resident(qg.shape), _resident(kg.shape), rope, rope],
        out_specs=[pl.BlockSpec((1, d, tm), lambda i, j: (i, 0, j)),
                   pl.BlockSpec((1, tm // 2, kv_dim), lambda i, j: (i, j, 0)),
                   pl.BlockSpec((1, kv_dim // 2, tm), lambda i, j: (i, 0, j))],
        out_shape=[jax.ShapeDtypeStruct((b, d, s), BF16),
                   jax.ShapeDtypeStruct((b, s // 2, kv_dim), jnp.uint32),
                   jax.ShapeDtypeStruct((b, kv_dim // 2, s), jnp.uint32)],
        scratch_shapes=[pltpu.VMEM((2 * kv_dim, tm), F32),
                        pltpu.VMEM((2, FEATURE_CHUNK, tm), F32)],
        compiler_params=pltpu.CompilerParams(
            dimension_semantics=("parallel", "parallel"),
            vmem_limit_bytes=V7X_VMEM_LIMIT_BYTES),
        name="qkv1_proj",
    )(x, ga, gkv, wqt, wkvt, qg, kg, cos_t, sin_t)


def _swa_kernel(qt_ref, k_ref, vt_ref, sink_ref, ot_ref, bias_ref, s_ref):
    w = WINDOW
    s_len = qt_ref.shape[-1]
    kv_dim = k_ref.shape[-1]
    gw = SW_PACK * w

    n_kv = kv_dim // HEAD_DIM
    n_blocks = s_len // w

    kidx = lax.broadcasted_iota(jnp.int32, (2 * w, gw), 0)
    qidx = lax.broadcasted_iota(jnp.int32, (2 * w, gw), 1) & (w - 1)
    bias_ref[...] = jnp.where((kidx > qidx) & (kidx <= qidx + w), 0.0, NEG_INF).astype(F32)

    def window(n, first):
        nk = w if first else 2 * w
        k0 = 0 if first else pl.multiple_of((n - 1) * w, w)
        q0 = 0 if first else pl.multiple_of(n * w, w)
        return nk, k0, q0

    def heads(j, part):
        return [SW_GROUP * j + SW_PACK * part + u for u in range(SW_PACK)]

    def scores(n, j, part, first):
        nk, k0, q0 = window(n, first)
        k0_packed = k0 if first else pl.multiple_of((n - 1) * (w // 2), w // 2)
        kwin = _unpack_rows(k_ref[0, pl.ds(k0_packed, nk // 2), :])
        qcat = jnp.concatenate(
            [qt_ref[0, h * HEAD_DIM:(h + 1) * HEAD_DIM, pl.ds(q0, w)]
             for h in heads(j, part)], axis=1)
        pieces = []
        if j > 0:
            pieces.append(jnp.zeros((j * HEAD_DIM, gw), BF16))
        pieces.append(qcat)
        if j + 1 < n_kv:
            pieces.append(jnp.zeros(((n_kv - j - 1) * HEAD_DIM, gw), BF16))
        qz = jnp.concatenate(pieces, axis=0) if len(pieces) > 1 else qcat
        return jnp.dot(kwin, qz, preferred_element_type=F32)

    def finish(n, j, part, first, s):
        nk, k0, q0 = window(n, first)
        s = s + bias_ref[2 * w - nk:, :]
        sink = sink_ref[j:j + 1, part * gw:(part + 1) * gw]
        m = jnp.maximum(jnp.max(s, axis=0, keepdims=True), sink)
        e = jnp.exp2(s - m).astype(BF16)
        vwin = _unpack_rows(
            vt_ref[0, j * HEAD_DIM // 2:(j + 1) * HEAD_DIM // 2, pl.ds(k0, nk)])
        vext = jnp.concatenate([vwin, jnp.ones((ONES_ROWS, nk), BF16)], axis=0)
        o = jnp.dot(vext, e, preferred_element_type=F32)
        den = o[HEAD_DIM:HEAD_DIM + 1] + jnp.exp2(sink - m)
        o = o[:HEAD_DIM] * (1.0 / den)
        for u, h in enumerate(heads(j, part)):
            ot_ref[0, h * HEAD_DIM:(h + 1) * HEAD_DIM, pl.ds(q0, w)] = (
                o[:, u * w:(u + 1) * w].astype(BF16))

    def run(chains):
        n_slots = s_ref.shape[0]

        def park(t):
            nk = window(chains[t][0], chains[t][3])[0]
            s_ref[t % n_slots, :nk, :] = scores(*chains[t])

        for t in range(min(LOOKAHEAD, len(chains))):
            park(t)
        for t, ch in enumerate(chains):
            if t + LOOKAHEAD < len(chains):
                park(t + LOOKAHEAD)
            nk = window(ch[0], ch[3])[0]
            finish(*ch, s_ref[t % n_slots, :nk, :])

    parts = range(SW_GROUP // SW_PACK)
    run([(0, j, part, True) for j in range(n_kv) for part in parts])

    def body(t, carry):
        n0 = 1 + t * SWA_UNROLL
        run([(n0 + u, j, part, False)
             for u in range(SWA_UNROLL) for j in range(n_kv) for part in parts])
        return carry

    lax.fori_loop(0, (n_blocks - 1) // SWA_UNROLL, body, 0)


def _swa(qt, k, vt, sink_rows):
    b, d, s = qt.shape
    kv_dim = k.shape[-1]
    assert (s // WINDOW - 1) % SWA_UNROLL == 0
    return pl.pallas_call(
        _swa_kernel,
        grid=(b,),
        in_specs=[pl.BlockSpec((1, d, s), lambda i: (i, 0, 0)),
                  pl.BlockSpec((1, s // 2, kv_dim), lambda i: (i, 0, 0)),
                  pl.BlockSpec((1, kv_dim // 2, s), lambda i: (i, 0, 0)),
                  _resident(sink_rows.shape)],
        out_specs=pl.BlockSpec((1, d, s), lambda i: (i, 0, 0)),
        out_shape=jax.ShapeDtypeStruct((b, d, s), BF16),
        scratch_shapes=[pltpu.VMEM((2 * WINDOW, SW_PACK * WINDOW), F32),
                        pltpu.VMEM((LOOKAHEAD + 1, 2 * WINDOW, SW_PACK * WINDOW), F32)],
        compiler_params=pltpu.CompilerParams(
            dimension_semantics=("parallel",),
            vmem_limit_bytes=V7X_VMEM_LIMIT_BYTES),
        name="swa_attn",
    )(qt, k, vt, sink_rows)


def _pack_row_pairs(w):
    r, c = w.shape
    pairs = w.astype(BF16).reshape(r // 2, 2, c).transpose(0, 2, 1)
    return lax.bitcast_convert_type(pairs, jnp.uint32)


def _lane_bcast(col, n):
    return jnp.broadcast_to(col.astype(F32).reshape(-1, 1), (col.size, n))


def kernel(x, positions, attn_norm, ffn_norm, w_gate_up, w_down, da_w_qkv, da_q_norm, da_k_norm,
           da_lambda, da_subln, da_w_o, kv_norm, w_kv, k_norm, sw_w_q, sw_q_norm, sw_sinks, sw_w_o):
    b, s, d = x.shape
    scale = 1.0 / math.sqrt(HEAD_DIM)

    inv = 1.0 / (ROPE_THETA ** (jnp.arange(0, HEAD_DIM, 2, dtype=F32) / HEAD_DIM))
    ang_t = positions.astype(F32)[:, None, :] * inv[None, :, None]
    cos_t = jnp.cos(ang_t)
    sin_t = jnp.sin(ang_t)

    lambda_init = 0.8 - 0.6 * math.exp(-0.3 * 0)
    qt, k, vt = _qkv0(x, attn_norm[0].reshape(1, d), _pack_row_pairs(da_w_qkv[0].T),
                      _lane_bcast(da_q_norm[0] * (scale * LOG2_E), ROW_TILE),
                      _lane_bcast(da_k_norm[0], ROW_TILE), cos_t, sin_t)
    at = _diff_attn(da_lambda[0].astype(F32), qt, k, vt,
                    _lane_bcast(da_subln[0], ATTN_BLOCK), lambda_init)
    x = _proj_ffn(x, at, da_w_o[0].astype(BF16), ffn_norm[0].reshape(1, d),
                  w_gate_up[0].astype(BF16), w_down[0].astype(BF16))

    qt, k, vt = _qkv1(x, attn_norm[1].reshape(1, d), kv_norm.reshape(1, d),
                      _pack_row_pairs(sw_w_q[0].T), _pack_row_pairs(w_kv.T),
                      _lane_bcast(sw_q_norm[0] * (scale * LOG2_E), ROW_TILE),
                      _lane_bcast(k_norm, ROW_TILE), cos_t, sin_t)
    sink_rows = jnp.repeat((sw_sinks[0].astype(F32) * LOG2_E).reshape(SW_KV_HEADS, SW_GROUP),
                           WINDOW, axis=1)
    at = _swa(qt, k, vt, sink_rows)
    x = _proj_ffn(x, at, sw_w_o[0].astype(BF16), ffn_norm[1].reshape(1, d),
                  w_gate_up[1].astype(BF16), w_down[1].astype(BF16))
    return x
```

```python
import functools
import math

import jax
import jax.numpy as jnp
from jax import lax
from jax.experimental import pallas as pl
from jax.experimental.pallas import tpu as pltpu

HEAD_DIM = 64
ROPE_THETA = 10000.0
NORM_EPS = 1e-6
NEG_INF = -1e30
LOG2_E = math.log2(math.e)
WINDOW = 128
SW_KV_HEADS = 4
SW_GROUP = 4
SW_PACK = 2

F32 = jnp.float32
BF16 = jnp.bfloat16

V7X_VMEM_LIMIT_BYTES = 56 * 1024 * 1024

ROW_TILE = 512
FEATURE_CHUNK = 512
FFN_CHUNK = 256
ATTN_BLOCK = 256
ONES_ROWS = 16
LOOKAHEAD = 4
SWA_UNROLL = 3

NT_DIMS = (((1,), (1,)), ((), ()))
TN_DIMS = (((0,), (0,)), ((), ()))


def _resident(shape):
    nd = len(shape)
    return pl.BlockSpec(shape, lambda *_: (0,) * nd, pipeline_mode=pl.Buffered(1))


def _rms_scale(x):
    return lax.rsqrt(jnp.mean(x * x, axis=-1, keepdims=True) + NORM_EPS)


def _pack_rows(x):
    return pltpu.bitcast(x, jnp.uint32)


def _unpack_rows(x):
    return pltpu.bitcast(x, BF16)


def _transpose_pack_weight(w_ref, wt_ref):
    n = w_ref.shape[1]
    for c in range(n // FEATURE_CHUNK):
        cols = slice(c * FEATURE_CHUNK, (c + 1) * FEATURE_CHUNK)
        rows = slice(c * FEATURE_CHUNK // 2, (c + 1) * FEATURE_CHUNK // 2)
        wt_ref[rows, :] = _pack_rows(w_ref[:, cols].T.astype(BF16))


def _norm_rope_t(t, gain, cos, sin):
    r = lax.rsqrt(jnp.mean(t * t, axis=0, keepdims=True) + NORM_EPS)
    tn = t * r * gain
    x1 = tn[: HEAD_DIM // 2]
    x2 = tn[HEAD_DIM // 2:]
    return jnp.concatenate([x1 * cos - x2 * sin, x2 * cos + x1 * sin], axis=0)


def _qkv0_kernel(x_ref, g_ref, w_ref, qg_ref, kg_ref, cos_ref, sin_ref,
                 qt_ref, k_ref, vt_ref, wt_ref, res_ref):
    d_model = x_ref.shape[-1]

    @pl.when((pl.program_id(0) == 0) & (pl.program_id(1) == 0))
    def _():
        _transpose_pack_weight(w_ref, wt_ref)

    x = x_ref[0]
    h = (x * _rms_scale(x) * g_ref[...]).astype(BF16)
    cos = cos_ref[0]
    sin = sin_ref[0]
    n_chunks = d_model // FEATURE_CHUNK

    def project(c):
        rows = slice(c * FEATURE_CHUNK // 2, (c + 1) * FEATURE_CHUNK // 2)
        w_rows = _unpack_rows(wt_ref[rows, :])
        res_ref[c % 2] = lax.dot_general(w_rows, h, NT_DIMS, preferred_element_type=F32)

    project(0)
    for c in range(3 * n_chunks):
        if c + 1 < 3 * n_chunks:
            project(c + 1)
        res = res_ref.at[c % 2]
        kind, cc = divmod(c, n_chunks)
        out_rows = slice(cc * FEATURE_CHUNK, (cc + 1) * FEATURE_CHUNK)
        half_rows = slice(cc * FEATURE_CHUNK // 2, (cc + 1) * FEATURE_CHUNK // 2)
        if kind == 2:
            vt_ref[0, half_rows, :] = _pack_rows(res[...].astype(BF16))
            continue
        gain_ref = qg_ref if kind == 0 else kg_ref
        parts = []
        for g in range(FEATURE_CHUNK // HEAD_DIM):
            half = g % 2
            gain = gain_ref[half * HEAD_DIM:(half + 1) * HEAD_DIM, :]
            parts.append(_norm_rope_t(res[g * HEAD_DIM:(g + 1) * HEAD_DIM, :], gain, cos, sin))
        out = jnp.concatenate(parts, axis=0)
        if kind == 0:
            qt_ref[0, out_rows, :] = out.astype(BF16)
        else:
            k_ref[0, :, out_rows] = _pack_rows(out.T.astype(BF16))


def _qkv0(x, g, w, qg, kg, cos_t, sin_t):
    b, s, d = x.shape
    tm = ROW_TILE
    grid = (b, s // tm)
    feat = pl.BlockSpec((1, d, tm), lambda i, j: (i, 0, j))
    feat_packed = pl.BlockSpec((1, d // 2, tm), lambda i, j: (i, 0, j))
    tok = pl.BlockSpec((1, tm, d), lambda i, j: (i, j, 0))
    tok_packed = pl.BlockSpec((1, tm // 2, d), lambda i, j: (i, j, 0))
    rope = pl.BlockSpec((1, HEAD_DIM // 2, tm), lambda i, j: (i, 0, j))
    return pl.pallas_call(
        _qkv0_kernel,
        grid=grid,
        in_specs=[tok, _resident((1, d)), _resident(w.shape), _resident(qg.shape),
                  _resident(kg.shape), rope, rope],
        out_specs=[feat, tok_packed, feat_packed],
        out_shape=[jax.ShapeDtypeStruct((b, d, s), BF16),
                   jax.ShapeDtypeStruct((b, s // 2, d), jnp.uint32),
                   jax.ShapeDtypeStruct((b, d // 2, s), jnp.uint32)],
        scratch_shapes=[pltpu.VMEM((w.shape[1] // 2, d), jnp.uint32),
                        pltpu.VMEM((2, FEATURE_CHUNK, tm), F32)],
        compiler_params=pltpu.CompilerParams(
            dimension_semantics=("arbitrary", "arbitrary"),
            vmem_limit_bytes=V7X_VMEM_LIMIT_BYTES),
        name="qkv0_proj",
    )(x, g, w, qg, kg, cos_t, sin_t)


def _diff_attn_kernel(lam_ref, qt_ref, k_ref, vt_ref, sg_ref, ot_ref,
                      qz_ref, m_ref, acc_ref, s_ref, *, lambda_init):
    bq = qt_ref.shape[-1]
    hd2 = 2 * HEAD_DIM
    n_heads = qt_ref.shape[1] // hd2
    i = pl.program_id(1)

    lp = lam_ref[...]
    lam = (jnp.exp(jnp.sum(lp[0:1] * lp[1:2], axis=-1, keepdims=True))
           - jnp.exp(jnp.sum(lp[2:3] * lp[3:4], axis=-1, keepdims=True)) + lambda_init)

    zero = jnp.zeros((HEAD_DIM, bq), BF16)
    for h in range(n_heads):
        q0 = qt_ref[0, h * hd2:h * hd2 + HEAD_DIM, :]
        q1 = qt_ref[0, h * hd2 + HEAD_DIM:(h + 1) * hd2, :]
        qz_ref[2 * h] = jnp.concatenate([q0, zero], axis=0)
        qz_ref[2 * h + 1] = jnp.concatenate([zero, q1], axis=0)
    m_ref[...] = jnp.full(m_ref.shape, NEG_INF, F32)
    acc_ref[...] = jnp.zeros(acc_ref.shape, F32)

    def block(k0, bk, masked):
        ones = jnp.ones((ONES_ROWS, bk), BF16)
        k0_packed = pl.multiple_of(lax.shift_right_logical(k0, 1), bk // 2)
        if masked:
            kidx = lax.broadcasted_iota(jnp.int32, (bk, bq), 0)
            qidx = lax.broadcasted_iota(jnp.int32, (bk, bq), 1)
            keep = kidx <= qidx

        def scores(hc):
            h = hc // 2
            kblk = _unpack_rows(k_ref[0, pl.ds(k0_packed, bk // 2), h * hd2:(h + 1) * hd2])
            return jnp.dot(kblk, qz_ref[hc], preferred_element_type=F32)

        n_slots = s_ref.shape[0]

        def park(hc):
            s_ref[hc % n_slots, :bk, :] = scores(hc)

        for hc in range(LOOKAHEAD):
            park(hc)
        for hc in range(2 * n_heads):
            h = hc // 2
            if hc + LOOKAHEAD < 2 * n_heads:
                park(hc + LOOKAHEAD)
            s = s_ref[hc % n_slots, :bk, :]
            vblk = _unpack_rows(vt_ref[0, h * HEAD_DIM:(h + 1) * HEAD_DIM, pl.ds(k0, bk)])
            vext = jnp.concatenate([vblk, ones], axis=0)
            if masked:
                s = jnp.where(keep, s, NEG_INF)
            m_old = m_ref[hc]
            m_new = jnp.maximum(m_old, jnp.max(s, axis=0, keepdims=True))
            alpha = jnp.exp2(m_old - m_new)
            p = jnp.exp2(s - m_new)
            acc_ref[hc] = alpha * acc_ref[hc] + jnp.dot(
                vext, p.astype(BF16), preferred_element_type=F32)
            m_ref[hc] = m_new

    def body(j, carry):
        block(pl.multiple_of(j * 2 * bq, 2 * bq), 2 * bq, False)
        return carry

    lax.fori_loop(0, lax.shift_right_logical(i, 1), body, 0)

    @pl.when((i & 1) == 1)
    def _():
        block(pl.multiple_of((i - 1) * bq, bq), bq, False)

    block(pl.multiple_of(i * bq, bq), bq, True)

    for h in range(n_heads):
        a0 = acc_ref[2 * h]
        a1 = acc_ref[2 * h + 1]
        o = (a0[:hd2] / a0[hd2:hd2 + 1]
             - lam * (a1[:hd2] / a1[hd2:hd2 + 1]))
        r = lax.rsqrt(jnp.mean(o * o, axis=0, keepdims=True) + NORM_EPS)
        ot_ref[0, h * hd2:(h + 1) * hd2, :] = (
            o * r * sg_ref[...] * (1.0 - lambda_init)).astype(BF16)


def _diff_attn(lam_p, qt, k, vt, sg, lambda_init):
    b, d, s = qt.shape
    hd2 = 2 * HEAD_DIM
    bq = ATTN_BLOCK
    n_half = d // HEAD_DIM
    return pl.pallas_call(
        functools.partial(_diff_attn_kernel, lambda_init=lambda_init),
        grid=(b, s // bq),
        in_specs=[_resident(lam_p.shape),
                  pl.BlockSpec((1, d, bq), lambda bi, i: (bi, 0, i)),
                  pl.BlockSpec((1, s // 2, d), lambda bi, i: (bi, 0, 0)),
                  pl.BlockSpec((1, d // 2, s), lambda bi, i: (bi, 0, 0)),
                  _resident(sg.shape)],
        out_specs=pl.BlockSpec((1, d, bq), lambda bi, i: (bi, 0, i)),
        out_shape=jax.ShapeDtypeStruct((b, d, s), BF16),
        scratch_shapes=[pltpu.VMEM((n_half, hd2, bq), BF16),
                        pltpu.VMEM((n_half, 1, bq), F32),
                        pltpu.VMEM((n_half, hd2 + ONES_ROWS, bq), F32),
                        pltpu.VMEM((LOOKAHEAD + 1, 2 * bq, bq), F32)],
        compiler_params=pltpu.CompilerParams(
            dimension_semantics=("parallel", "arbitrary"),
            vmem_limit_bytes=V7X_VMEM_LIMIT_BYTES),
        name="diff_attn",
    )(lam_p, qt, k, vt, sg)


def _proj_ffn_kernel(x_ref, at_ref, wo_ref, g_ref, wg_ref, wu_ref, wd_ref, o_ref, hid_ref):
    x = x_ref[0] + lax.dot_general(at_ref[0], wo_ref[...], TN_DIMS,
                                   preferred_element_type=F32)
    h = (x * _rms_scale(x) * g_ref[...]).astype(BF16)
    n_chunks = hid_ref.shape[-1] // FFN_CHUNK
    for c in range(n_chunks):
        cols = slice(c * FFN_CHUNK, (c + 1) * FFN_CHUNK)
        gate = jnp.dot(h, wg_ref[:, cols], preferred_element_type=F32)
        up = jnp.dot(h, wu_ref[:, cols], preferred_element_type=F32)
        hid_ref[:, c * FFN_CHUNK:(c + 1) * FFN_CHUNK] = (
            gate * jax.nn.sigmoid(gate) * up).astype(BF16)
    o_ref[0] = x + jnp.dot(hid_ref[...], wd_ref[...], preferred_element_type=F32)


def _proj_ffn(x, at, wo, g, wgu, wd):
    b, s, d = x.shape
    tm = ROW_TILE
    hidden = wd.shape[0]
    tok = pl.BlockSpec((1, tm, d), lambda i, j: (i, j, 0))
    feat = pl.BlockSpec((1, d, tm), lambda i, j: (i, 0, j))
    return pl.pallas_call(
        _proj_ffn_kernel,
        grid=(b, s // tm),
        in_specs=[tok, feat, _resident(wo.shape), _resident((1, d)),
                  pl.BlockSpec((d, hidden), lambda i, j: (0, 0), pipeline_mode=pl.Buffered(1)),
                  pl.BlockSpec((d, hidden), lambda i, j: (0, 1), pipeline_mode=pl.Buffered(1)),
                  _resident(wd.shape)],
        out_specs=tok,
        out_shape=jax.ShapeDtypeStruct((b, s, d), F32),
        scratch_shapes=[pltpu.VMEM((tm, hidden), BF16)],
        compiler_params=pltpu.CompilerParams(
            dimension_semantics=("parallel", "parallel"),
            vmem_limit_bytes=V7X_VMEM_LIMIT_BYTES),
        name="proj_ffn",
    )(x, at, wo, g, wgu, wgu, wd)


def _qkv1_kernel(x_ref, ga_ref, gkv_ref, wq_ref, wkv_ref, qg_ref, kg_ref, cos_ref, sin_ref,
                 qt_ref, k_ref, vt_ref, wqt_ref, wkvt_ref, kv_ref, res_ref):
    d_model = x_ref.shape[-1]
    kv_dim = k_ref.shape[-1]

    @pl.when((pl.program_id(0) == 0) & (pl.program_id(1) == 0))
    def _():
        _transpose_pack_weight(wq_ref, wqt_ref)
        _transpose_pack_weight(wkv_ref, wkvt_ref)

    x = x_ref[0]
    xn = x * _rms_scale(x)
    h_a = (xn * ga_ref[...]).astype(BF16)
    h_kv = (xn * gkv_ref[...]).astype(BF16)
    cos = cos_ref[0]
    sin = sin_ref[0]

    n_chunks = d_model // FEATURE_CHUNK

    def project_q(c):
        rows = slice(c * FEATURE_CHUNK // 2, (c + 1) * FEATURE_CHUNK // 2)
        res_ref[c % 2] = lax.dot_general(_unpack_rows(wqt_ref[rows, :]), h_a, NT_DIMS,
                                         preferred_element_type=F32)

    kv_ref[...] = lax.dot_general(_unpack_rows(wkvt_ref[...]), h_kv, NT_DIMS,
                                  preferred_element_type=F32)
    project_q(0)
    kparts = [_norm_rope_t(kv_ref[g * HEAD_DIM:(g + 1) * HEAD_DIM, :], kg_ref[...], cos, sin)
              for g in range(kv_dim // HEAD_DIM)]
    k_ref[0] = _pack_rows(jnp.concatenate(kparts, axis=0).T.astype(BF16))
    vt_ref[0] = _pack_rows(kv_ref[kv_dim:, :].astype(BF16))

    for c in range(n_chunks):
        if c + 1 < n_chunks:
            project_q(c + 1)
        res = res_ref.at[c % 2]
        rows = slice(c * FEATURE_CHUNK, (c + 1) * FEATURE_CHUNK)
        parts = [_norm_rope_t(res[g * HEAD_DIM:(g + 1) * HEAD_DIM, :], qg_ref[...], cos, sin)
                 for g in range(FEATURE_CHUNK // HEAD_DIM)]
        qt_ref[0, rows, :] = jnp.concatenate(parts, axis=0).astype(BF16)


def _qkv1(x, ga, gkv, wq, wkv, qg, kg, cos_t, sin_t):
    b, s, d = x.shape
    tm = ROW_TILE
    kv_dim = wkv.shape[1] // 2
    tok = pl.BlockSpec((1, tm, d), lambda i, j: (i, j, 0))
    rope = pl.BlockSpec((1, HEAD_DIM // 2, tm), lambda i, j: (i, 0, j))
    return pl.pallas_call(
        _qkv1_kernel,
        grid=(b, s // tm),
        in_specs=[tok, _resident((1, d)), _resident((1, d)), _resident(wq.shape),
                  _resident(wkv.shape), _resident(qg.shape), _resident(kg.shape), rope, rope],
        out_specs=[pl.BlockSpec((1, d, tm), lambda i, j: (i, 0, j)),
                   pl.BlockSpec((1, tm // 2, kv_dim), lambda i, j: (i, j, 0)),
                   pl.BlockSpec((1, kv_dim // 2, tm), lambda i, j: (i, 0, j))],
        out_shape=[jax.ShapeDtypeStruct((b, d, s), BF16),
                   jax.ShapeDtypeStruct((b, s // 2, kv_dim), jnp.uint32),
                   jax.ShapeDtypeStruct((b, kv_dim // 2, s), jnp.uint32)],
        scratch_shapes=[pltpu.VMEM((wq.shape[1] // 2, d), jnp.uint32),
                        pltpu.VMEM((kv_dim, d), jnp.uint32),
                        pltpu.VMEM((2 * kv_dim, tm), F32),
                        pltpu.VMEM((2, FEATURE_CHUNK, tm), F32)],
        compiler_params=pltpu.CompilerParams(
            dimension_semantics=("arbitrary", "arbitrary"),
            vmem_limit_bytes=V7X_VMEM_LIMIT_BYTES),
        name="qkv1_proj",
    )(x, ga, gkv, wq, wkv, qg, kg, cos_t, sin_t)


def _swa_kernel(qt_ref, k_ref, vt_ref, sink_ref, ot_ref, bias_ref, s_ref):
    w = WINDOW
    s_len = qt_ref.shape[-1]
    kv_dim = k_ref.shape[-1]
    gw = SW_PACK * w

    n_kv = kv_dim // HEAD_DIM
    n_blocks = s_len // w

    kidx = lax.broadcasted_iota(jnp.int32, (2 * w, gw), 0)
    qidx = lax.broadcasted_iota(jnp.int32, (2 * w, gw), 1) & (w - 1)
    bias_ref[...] = jnp.where((kidx > qidx) & (kidx <= qidx + w), 0.0, NEG_INF).astype(F32)

    def window(n, first):
        nk = w if first else 2 * w
        k0 = 0 if first else pl.multiple_of((n - 1) * w, w)
        q0 = 0 if first else pl.multiple_of(n * w, w)
        return nk, k0, q0

    def heads(j, part):
        return [SW_GROUP * j + SW_PACK * part + u for u in range(SW_PACK)]

    def scores(n, j, part, first):
        nk, k0, q0 = window(n, first)
        k0_packed = k0 if first else pl.multiple_of((n - 1) * (w // 2), w // 2)
        kwin = _unpack_rows(k_ref[0, pl.ds(k0_packed, nk // 2), :])
        qcat = jnp.concatenate(
            [qt_ref[0, h * HEAD_DIM:(h + 1) * HEAD_DIM, pl.ds(q0, w)]
             for h in heads(j, part)], axis=1)
        pieces = []
        if j > 0:
            pieces.append(jnp.zeros((j * HEAD_DIM, gw), BF16))
        pieces.append(qcat)
        if j + 1 < n_kv:
            pieces.append(jnp.zeros(((n_kv - j - 1) * HEAD_DIM, gw), BF16))
        qz = jnp.concatenate(pieces, axis=0) if len(pieces) > 1 else qcat
        return jnp.dot(kwin, qz, preferred_element_type=F32)

    def finish(n, j, part, first, s):
        nk, k0, q0 = window(n, first)
        s = s + bias_ref[2 * w - nk:, :]
        sink = sink_ref[j:j + 1, part * gw:(part + 1) * gw]
        m = jnp.maximum(jnp.max(s, axis=0, keepdims=True), sink)
        e = jnp.exp2(s - m).astype(BF16)
        vwin = _unpack_rows(
            vt_ref[0, j * HEAD_DIM // 2:(j + 1) * HEAD_DIM // 2, pl.ds(k0, nk)])
        vext = jnp.concatenate([vwin, jnp.ones((ONES_ROWS, nk), BF16)], axis=0)
        o = jnp.dot(vext, e, preferred_element_type=F32)
        den = o[HEAD_DIM:HEAD_DIM + 1] + jnp.exp2(sink - m)
        o = o[:HEAD_DIM] * (1.0 / den)
        for u, h in enumerate(heads(j, part)):
            ot_ref[0, h * HEAD_DIM:(h + 1) * HEAD_DIM, pl.ds(q0, w)] = (
                o[:, u * w:(u + 1) * w].astype(BF16))

    def run(chains):
        n_slots = s_ref.shape[0]

        def park(t):
            nk = window(chains[t][0], chains[t][3])[0]
            s_ref[t % n_slots, :nk, :] = scores(*chains[t])

        for t in range(min(LOOKAHEAD, len(chains))):
            park(t)
        for t, ch in enumerate(chains):
            if t + LOOKAHEAD < len(chains):
                park(t + LOOKAHEAD)
            nk = window(ch[0], ch[3])[0]
            finish(*ch, s_ref[t % n_slots, :nk, :])

    parts = range(SW_GROUP // SW_PACK)
    run([(0, j, part, True) for j in range(n_kv) for part in parts])

    def body(t, carry):
        n0 = 1 + t * SWA_UNROLL
        run([(n0 + u, j, part, False)
             for u in range(SWA_UNROLL) for j in range(n_kv) for part in parts])
        return carry

    lax.fori_loop(0, (n_blocks - 1) // SWA_UNROLL, body, 0)


def _swa(qt, k, vt, sink_rows):
    b, d, s = qt.shape
    kv_dim = k.shape[-1]
    assert (s // WINDOW - 1) % SWA_UNROLL == 0
    return pl.pallas_call(
        _swa_kernel,
        grid=(b,),
        in_specs=[pl.BlockSpec((1, d, s), lambda i: (i, 0, 0)),
                  pl.BlockSpec((1, s // 2, kv_dim), lambda i: (i, 0, 0)),
                  pl.BlockSpec((1, kv_dim // 2, s), lambda i: (i, 0, 0)),
                  _resident(sink_rows.shape)],
        out_specs=pl.BlockSpec((1, d, s), lambda i: (i, 0, 0)),
        out_shape=jax.ShapeDtypeStruct((b, d, s), BF16),
        scratch_shapes=[pltpu.VMEM((2 * WINDOW, SW_PACK * WINDOW), F32),
                        pltpu.VMEM((LOOKAHEAD + 1, 2 * WINDOW, SW_PACK * WINDOW), F32)],
        compiler_params=pltpu.CompilerParams(
            dimension_semantics=("parallel",),
            vmem_limit_bytes=V7X_VMEM_LIMIT_BYTES),
        name="swa_attn",
    )(qt, k, vt, sink_rows)


def _lane_bcast(col, n):
    return jnp.broadcast_to(col.astype(F32).reshape(-1, 1), (col.size, n))


def kernel(x, positions, attn_norm, ffn_norm, w_gate_up, w_down, da_w_qkv, da_q_norm, da_k_norm,
           da_lambda, da_subln, da_w_o, kv_norm, w_kv, k_norm, sw_w_q, sw_q_norm, sw_sinks, sw_w_o):
    b, s, d = x.shape
    scale = 1.0 / math.sqrt(HEAD_DIM)

    inv = 1.0 / (ROPE_THETA ** (jnp.arange(0, HEAD_DIM, 2, dtype=F32) / HEAD_DIM))
    ang_t = positions.astype(F32)[:, None, :] * inv[None, :, None]
    cos_t = jnp.cos(ang_t)
    sin_t = jnp.sin(ang_t)

    lambda_init = 0.8 - 0.6 * math.exp(-0.3 * 0)
    qt, k, vt = _qkv0(x, attn_norm[0].reshape(1, d), da_w_qkv[0],
                      _lane_bcast(da_q_norm[0] * (scale * LOG2_E), ROW_TILE),
                      _lane_bcast(da_k_norm[0], ROW_TILE), cos_t, sin_t)
    at = _diff_attn(da_lambda[0].astype(F32), qt, k, vt,
                    _lane_bcast(da_subln[0], ATTN_BLOCK), lambda_init)
    x = _proj_ffn(x, at, da_w_o[0].astype(BF16), ffn_norm[0].reshape(1, d),
                  w_gate_up[0].astype(BF16), w_down[0].astype(BF16))

    qt, k, vt = _qkv1(x, attn_norm[1].reshape(1, d), kv_norm.reshape(1, d),
                      sw_w_q[0], w_kv,
                      _lane_bcast(sw_q_norm[0] * (scale * LOG2_E), ROW_TILE),
                      _lane_bcast(k_norm, ROW_TILE), cos_t, sin_t)
    sink_rows = jnp.repeat((sw_sinks[0].astype(F32) * LOG2_E).reshape(SW_KV_HEADS, SW_GROUP),
                           WINDOW, axis=1)
    at = _swa(qt, k, vt, sink_rows)
    x = _proj_ffn(x, at, sw_w_o[0].astype(BF16), ffn_norm[1].reshape(1, d),
                  w_gate_up[1].astype(BF16), w_down[1].astype(BF16))
    return x
```

```python
import functools
import math

import jax
import jax.numpy as jnp
from jax import lax
from jax.experimental import pallas as pl
from jax.experimental.pallas import tpu as pltpu

HEAD_DIM = 64
ROPE_THETA = 10000.0
NORM_EPS = 1e-6
NEG_INF = -1e30
LOG2_E = math.log2(math.e)
WINDOW = 128
SW_KV_HEADS = 4
SW_GROUP = 4
SW_PACK = 2

F32 = jnp.float32
BF16 = jnp.bfloat16

V7X_VMEM_LIMIT_BYTES = 56 * 1024 * 1024

ROW_TILE = 512
FEATURE_CHUNK = 512
FFN_CHUNK = 256
ATTN_BLOCK = 256
ONES_ROWS = 16
LOOKAHEAD = 4
SWA_UNROLL = 3

NT_DIMS = (((1,), (1,)), ((), ()))
TN_DIMS = (((0,), (0,)), ((), ()))


def _resident(shape):
    nd = len(shape)
    return pl.BlockSpec(shape, lambda *_: (0,) * nd, pipeline_mode=pl.Buffered(1))


def _rms_scale(x):
    return lax.rsqrt(jnp.mean(x * x, axis=-1, keepdims=True) + NORM_EPS)


def _pack_rows(x):
    return pltpu.bitcast(x, jnp.uint32)


def _unpack_rows(x):
    return pltpu.bitcast(x, BF16)


def _transpose_pack_weight(w_ref, wt_ref):
    n = w_ref.shape[1]
    for c in range(n // FEATURE_CHUNK):
        cols = slice(c * FEATURE_CHUNK, (c + 1) * FEATURE_CHUNK)
        rows = slice(c * FEATURE_CHUNK // 2, (c + 1) * FEATURE_CHUNK // 2)
        wt_ref[rows, :] = _pack_rows(w_ref[:, cols].T.astype(BF16))


def _norm_rope_t(t, gain, cos, sin):
    r = lax.rsqrt(jnp.mean(t * t, axis=0, keepdims=True) + NORM_EPS)
    tn = t * r * gain
    x1 = tn[: HEAD_DIM // 2]
    x2 = tn[HEAD_DIM // 2:]
    return jnp.concatenate([x1 * cos - x2 * sin, x2 * cos + x1 * sin], axis=0)


def _qkv0_kernel(x_ref, g_ref, w_ref, qg_ref, kg_ref, cos_ref, sin_ref,
                 qt_ref, k_ref, vt_ref, wt_ref, res_ref):
    d_model = x_ref.shape[-1]

    @pl.when((pl.program_id(0) == 0) & (pl.program_id(1) == 0))
    def _():
        _transpose_pack_weight(w_ref, wt_ref)

    x = x_ref[0]
    h = (x * _rms_scale(x) * g_ref[...]).astype(BF16)
    cos = cos_ref[0]
    sin = sin_ref[0]
    n_chunks = d_model // FEATURE_CHUNK

    def project(c):
        rows = slice(c * FEATURE_CHUNK // 2, (c + 1) * FEATURE_CHUNK // 2)
        w_rows = _unpack_rows(wt_ref[rows, :])
        res_ref[c % 2] = lax.dot_general(w_rows, h, NT_DIMS, preferred_element_type=F32)

    project(0)
    for c in range(3 * n_chunks):
        if c + 1 < 3 * n_chunks:
            project(c + 1)
        res = res_ref.at[c % 2]
        kind, cc = divmod(c, n_chunks)
        out_rows = slice(cc * FEATURE_CHUNK, (cc + 1) * FEATURE_CHUNK)
        half_rows = slice(cc * FEATURE_CHUNK // 2, (cc + 1) * FEATURE_CHUNK // 2)
        if kind == 2:
            vt_ref[0, half_rows, :] = _pack_rows(res[...].astype(BF16))
            continue
        gain_ref = qg_ref if kind == 0 else kg_ref
        parts = []
        for g in range(FEATURE_CHUNK // HEAD_DIM):
            half = g % 2
            gain = gain_ref[half * HEAD_DIM:(half + 1) * HEAD_DIM, :]
            parts.append(_norm_rope_t(res[g * HEAD_DIM:(g + 1) * HEAD_DIM, :], gain, cos, sin))
        out = jnp.concatenate(parts, axis=0)
        if kind == 0:
            qt_ref[0, out_rows, :] = out.astype(BF16)
        else:
            k_ref[0, :, out_rows] = _pack_rows(out.T.astype(BF16))


def _qkv0(x, g, w, qg, kg, cos_t, sin_t):
    b, s, d = x.shape
    tm = ROW_TILE
    grid = (b, s // tm)
    feat = pl.BlockSpec((1, d, tm), lambda i, j: (i, 0, j))
    feat_packed = pl.BlockSpec((1, d // 2, tm), lambda i, j: (i, 0, j))
    tok = pl.BlockSpec((1, tm, d), lambda i, j: (i, j, 0))
    tok_packed = pl.BlockSpec((1, tm // 2, d), lambda i, j: (i, j, 0))
    rope = pl.BlockSpec((1, HEAD_DIM // 2, tm), lambda i, j: (i, 0, j))
    return pl.pallas_call(
        _qkv0_kernel,
        grid=grid,
        in_specs=[tok, _resident((1, d)), _resident(w.shape), _resident(qg.shape),
                  _resident(kg.shape), rope, rope],
        out_specs=[feat, tok_packed, feat_packed],
        out_shape=[jax.ShapeDtypeStruct((b, d, s), BF16),
                   jax.ShapeDtypeStruct((b, s // 2, d), jnp.uint32),
                   jax.ShapeDtypeStruct((b, d // 2, s), jnp.uint32)],
        scratch_shapes=[pltpu.VMEM((w.shape[1] // 2, d), jnp.uint32),
                        pltpu.VMEM((2, FEATURE_CHUNK, tm), F32)],
        compiler_params=pltpu.CompilerParams(
            dimension_semantics=("arbitrary", "arbitrary"),
            vmem_limit_bytes=V7X_VMEM_LIMIT_BYTES),
        name="qkv0_proj",
    )(x, g, w, qg, kg, cos_t, sin_t)


def _diff_attn_kernel(lam_ref, qt_ref, k_ref, vt_ref, sg_ref, ot_ref,
                      qz_ref, m_ref, acc_ref, s_ref, *, lambda_init):
    bq = qt_ref.shape[-1]
    hd2 = 2 * HEAD_DIM
    n_heads = qt_ref.shape[1] // hd2
    i = pl.program_id(1)

    lp = lam_ref[...]
    lam = (jnp.exp(jnp.sum(lp[0:1] * lp[1:2], axis=-1, keepdims=True))
           - jnp.exp(jnp.sum(lp[2:3] * lp[3:4], axis=-1, keepdims=True)) + lambda_init)

    zero = jnp.zeros((HEAD_DIM, bq), BF16)
    for h in range(n_heads):
        q0 = qt_ref[0, h * hd2:h * hd2 + HEAD_DIM, :]
        q1 = qt_ref[0, h * hd2 + HEAD_DIM:(h + 1) * hd2, :]
        qz_ref[2 * h] = jnp.concatenate([q0, zero], axis=0)
        qz_ref[2 * h + 1] = jnp.concatenate([zero, q1], axis=0)

    def block(k0, bk, first):
        ones = jnp.ones((ONES_ROWS, bk), BF16)
        k0_packed = pl.multiple_of(lax.shift_right_logical(k0, 1), bk // 2)
        if first:
            kidx = lax.broadcasted_iota(jnp.int32, (bk, bq), 0)
            qidx = lax.broadcasted_iota(jnp.int32, (bk, bq), 1)
            keep = kidx <= qidx + (bk - bq)

        def scores(hc):
            h = hc // 2
            kblk = _unpack_rows(k_ref[0, pl.ds(k0_packed, bk // 2), h * hd2:(h + 1) * hd2])
            return jnp.dot(kblk, qz_ref[hc], preferred_element_type=F32)

        n_slots = s_ref.shape[0]

        def park(hc):
            s_ref[hc % n_slots, :bk, :] = scores(hc)

        for hc in range(LOOKAHEAD):
            park(hc)
        for hc in range(2 * n_heads):
            h = hc // 2
            if hc + LOOKAHEAD < 2 * n_heads:
                park(hc + LOOKAHEAD)
            s = s_ref[hc % n_slots, :bk, :]
            vblk = _unpack_rows(vt_ref[0, h * HEAD_DIM:(h + 1) * HEAD_DIM, pl.ds(k0, bk)])
            vext = jnp.concatenate([vblk, ones], axis=0)
            if first:
                s = jnp.where(keep, s, NEG_INF)
                m_new = jnp.max(s, axis=0, keepdims=True)
                p = jnp.exp2(s - m_new)
                acc_ref[hc] = jnp.dot(vext, p.astype(BF16), preferred_element_type=F32)
            else:
                m_old = m_ref[hc]
                m_new = jnp.maximum(m_old, jnp.max(s, axis=0, keepdims=True))
                alpha = jnp.exp2(m_old - m_new)
                p = jnp.exp2(s - m_new)
                acc_ref[hc] = alpha * acc_ref[hc] + jnp.dot(
                    vext, p.astype(BF16), preferred_element_type=F32)
            m_ref[hc] = m_new

    @pl.when((i & 1) == 0)
    def _():
        block(pl.multiple_of(i * bq, bq), bq, True)

    @pl.when((i & 1) == 1)
    def _():
        block(pl.multiple_of((i - 1) * bq, 2 * bq), 2 * bq, True)

    def body(j, carry):
        block(pl.multiple_of(j * 2 * bq, 2 * bq), 2 * bq, False)
        return carry

    lax.fori_loop(0, lax.shift_right_logical(i, 1), body, 0)

    for h in range(n_heads):
        a0 = acc_ref[2 * h]
        a1 = acc_ref[2 * h + 1]
        inv0 = 1.0 / a0[hd2:hd2 + 1]
        inv1 = lam / a1[hd2:hd2 + 1]
        o = a0[:hd2] * inv0 - a1[:hd2] * inv1
        r = lax.rsqrt(jnp.mean(o * o, axis=0, keepdims=True) + NORM_EPS)
        ot_ref[0, h * hd2:(h + 1) * hd2, :] = (
            o * r * sg_ref[...]).astype(BF16)


def _diff_attn(lam_p, qt, k, vt, sg, lambda_init):
    b, d, s = qt.shape
    hd2 = 2 * HEAD_DIM
    bq = ATTN_BLOCK
    n_half = d // HEAD_DIM
    return pl.pallas_call(
        functools.partial(_diff_attn_kernel, lambda_init=lambda_init),
        grid=(b, s // bq),
        in_specs=[_resident(lam_p.shape),
                  pl.BlockSpec((1, d, bq), lambda bi, i: (bi, 0, i)),
                  pl.BlockSpec((1, s // 2, d), lambda bi, i: (bi, 0, 0)),
                  pl.BlockSpec((1, d // 2, s), lambda bi, i: (bi, 0, 0)),
                  _resident(sg.shape)],
        out_specs=pl.BlockSpec((1, d, bq), lambda bi, i: (bi, 0, i)),
        out_shape=jax.ShapeDtypeStruct((b, d, s), BF16),
        scratch_shapes=[pltpu.VMEM((n_half, hd2, bq), BF16),
                        pltpu.VMEM((n_half, 1, bq), F32),
                        pltpu.VMEM((n_half, hd2 + ONES_ROWS, bq), F32),
                        pltpu.VMEM((LOOKAHEAD + 1, 2 * bq, bq), F32)],
        compiler_params=pltpu.CompilerParams(
            dimension_semantics=("parallel", "arbitrary"),
            vmem_limit_bytes=V7X_VMEM_LIMIT_BYTES),
        name="diff_attn",
    )(lam_p, qt, k, vt, sg)


def _proj_ffn_kernel(x_ref, at_ref, wo_ref, g_ref, wg_ref, wu_ref, wd_ref, o_ref, hid_ref):
    x = x_ref[0] + lax.dot_general(at_ref[0], wo_ref[...], TN_DIMS,
                                   preferred_element_type=F32)
    h = (x * _rms_scale(x) * g_ref[...]).astype(BF16)
    n_chunks = hid_ref.shape[-1] // FFN_CHUNK
    for c in range(n_chunks):
        cols = slice(c * FFN_CHUNK, (c + 1) * FFN_CHUNK)
        gate = jnp.dot(h, wg_ref[:, cols], preferred_element_type=F32)
        up = jnp.dot(h, wu_ref[:, cols], preferred_element_type=F32)
        hid_ref[:, c * FFN_CHUNK:(c + 1) * FFN_CHUNK] = (
            gate * jax.nn.sigmoid(gate) * up).astype(BF16)
    o_ref[0] = x + jnp.dot(hid_ref[...], wd_ref[...], preferred_element_type=F32)


def _proj_ffn(x, at, wo, g, wgu, wd, layer):
    b, s, d = x.shape
    tm = ROW_TILE
    hidden = wd.shape[1]
    tok = pl.BlockSpec((1, tm, d), lambda i, j: (i, j, 0))
    feat = pl.BlockSpec((1, d, tm), lambda i, j: (i, 0, j))
    once = pl.Buffered(1)
    return pl.pallas_call(
        _proj_ffn_kernel,
        grid=(b, s // tm),
        in_specs=[tok, feat, _resident(wo.shape), _resident((1, d)),
                  pl.BlockSpec((None, d, hidden), lambda i, j: (layer, 0, 0), pipeline_mode=once),
                  pl.BlockSpec((None, d, hidden), lambda i, j: (layer, 0, 1), pipeline_mode=once),
                  pl.BlockSpec((None, hidden, d), lambda i, j: (layer, 0, 0), pipeline_mode=once)],
        out_specs=tok,
        out_shape=jax.ShapeDtypeStruct((b, s, d), F32),
        scratch_shapes=[pltpu.VMEM((tm, hidden), BF16)],
        compiler_params=pltpu.CompilerParams(
            dimension_semantics=("parallel", "parallel"),
            vmem_limit_bytes=V7X_VMEM_LIMIT_BYTES),
        name="proj_ffn",
    )(x, at, wo, g, wgu, wgu, wd)


def _qkv1_kernel(x_ref, ga_ref, gkv_ref, wq_ref, wkv_ref, qg_ref, kg_ref, cos_ref, sin_ref,
                 qt_ref, k_ref, vt_ref, wqt_ref, wkvt_ref, kv_ref, res_ref):
    d_model = x_ref.shape[-1]
    kv_dim = k_ref.shape[-1]

    @pl.when((pl.program_id(0) == 0) & (pl.program_id(1) == 0))
    def _():
        _transpose_pack_weight(wq_ref, wqt_ref)
        _transpose_pack_weight(wkv_ref, wkvt_ref)

    x = x_ref[0]
    xn = x * _rms_scale(x)
    h_a = (xn * ga_ref[...]).astype(BF16)
    h_kv = (xn * gkv_ref[...]).astype(BF16)
    cos = cos_ref[0]
    sin = sin_ref[0]

    n_chunks = d_model // FEATURE_CHUNK

    def project_q(c):
        rows = slice(c * FEATURE_CHUNK // 2, (c + 1) * FEATURE_CHUNK // 2)
        res_ref[c % 2] = lax.dot_general(_unpack_rows(wqt_ref[rows, :]), h_a, NT_DIMS,
                                         preferred_element_type=F32)

    kv_ref[...] = lax.dot_general(_unpack_rows(wkvt_ref[...]), h_kv, NT_DIMS,
                                  preferred_element_type=F32)
    project_q(0)
    kparts = [_norm_rope_t(kv_ref[g * HEAD_DIM:(g + 1) * HEAD_DIM, :], kg_ref[...], cos, sin)
              for g in range(kv_dim // HEAD_DIM)]
    k_ref[0] = _pack_rows(jnp.concatenate(kparts, axis=0).T.astype(BF16))
    vt_ref[0] = _pack_rows(kv_ref[kv_dim:, :].astype(BF16))

    for c in range(n_chunks):
        if c + 1 < n_chunks:
            project_q(c + 1)
        res = res_ref.at[c % 2]
        rows = slice(c * FEATURE_CHUNK, (c + 1) * FEATURE_CHUNK)
        parts = [_norm_rope_t(res[g * HEAD_DIM:(g + 1) * HEAD_DIM, :], qg_ref[...], cos, sin)
                 for g in range(FEATURE_CHUNK // HEAD_DIM)]
        qt_ref[0, rows, :] = jnp.concatenate(parts, axis=0).astype(BF16)


def _qkv1(x, ga, gkv, wq, wkv, qg, kg, cos_t, sin_t):
    b, s, d = x.shape
    tm = ROW_TILE
    kv_dim = wkv.shape[1] // 2
    tok = pl.BlockSpec((1, tm, d), lambda i, j: (i, j, 0))
    rope = pl.BlockSpec((1, HEAD_DIM // 2, tm), lambda i, j: (i, 0, j))
    return pl.pallas_call(
        _qkv1_kernel,
        grid=(b, s // tm),
        in_specs=[tok, _resident((1, d)), _resident((1, d)), _resident(wq.shape),
                  _resident(wkv.shape), _resident(qg.shape), _resident(kg.shape), rope, rope],
        out_specs=[pl.BlockSpec((1, d, tm), lambda i, j: (i, 0, j)),
                   pl.BlockSpec((1, tm // 2, kv_dim), lambda i, j: (i, j, 0)),
                   pl.BlockSpec((1, kv_dim // 2, tm), lambda i, j: (i, 0, j))],
        out_shape=[jax.ShapeDtypeStruct((b, d, s), BF16),
                   jax.ShapeDtypeStruct((b, s // 2, kv_dim), jnp.uint32),
                   jax.ShapeDtypeStruct((b, kv_dim // 2, s), jnp.uint32)],
        scratch_shapes=[pltpu.VMEM((wq.shape[1] // 2, d), jnp.uint32),
                        pltpu.VMEM((kv_dim, d), jnp.uint32),
                        pltpu.VMEM((2 * kv_dim, tm), F32),
                        pltpu.VMEM((2, FEATURE_CHUNK, tm), F32)],
        compiler_params=pltpu.CompilerParams(
            dimension_semantics=("arbitrary", "arbitrary"),
            vmem_limit_bytes=V7X_VMEM_LIMIT_BYTES),
        name="qkv1_proj",
    )(x, ga, gkv, wq, wkv, qg, kg, cos_t, sin_t)


def _swa_kernel(qt_ref, k_ref, vt_ref, sink_ref, ot_ref, bias_ref, s_ref):
    w = WINDOW
    s_len = qt_ref.shape[-1]
    kv_dim = k_ref.shape[-1]
    gw = SW_PACK * w

    n_kv = kv_dim // HEAD_DIM
    n_blocks = s_len // w

    kidx = lax.broadcasted_iota(jnp.int32, (2 * w, gw), 0)
    qidx = lax.broadcasted_iota(jnp.int32, (2 * w, gw), 1) & (w - 1)
    bias_ref[...] = jnp.where((kidx > qidx) & (kidx <= qidx + w), 0.0, NEG_INF).astype(F32)

    def window(n, first):
        nk = w if first else 2 * w
        k0 = 0 if first else pl.multiple_of((n - 1) * w, w)
        q0 = 0 if first else pl.multiple_of(n * w, w)
        return nk, k0, q0

    def heads(j, part):
        return [SW_GROUP * j + SW_PACK * part + u for u in range(SW_PACK)]

    def scores(n, j, part, first):
        nk, k0, q0 = window(n, first)
        k0_packed = k0 if first else pl.multiple_of((n - 1) * (w // 2), w // 2)
        kwin = _unpack_rows(k_ref[0, pl.ds(k0_packed, nk // 2), :])
        qcat = jnp.concatenate(
            [qt_ref[0, h * HEAD_DIM:(h + 1) * HEAD_DIM, pl.ds(q0, w)]
             for h in heads(j, part)], axis=1)
        pieces = []
        if j > 0:
            pieces.append(jnp.zeros((j * HEAD_DIM, gw), BF16))
        pieces.append(qcat)
        if j + 1 < n_kv:
            pieces.append(jnp.zeros(((n_kv - j - 1) * HEAD_DIM, gw), BF16))
        qz = jnp.concatenate(pieces, axis=0) if len(pieces) > 1 else qcat
        return jnp.dot(kwin, qz, preferred_element_type=F32)

    def finish(n, j, part, first, s):
        nk, k0, q0 = window(n, first)
        s = s + bias_ref[2 * w - nk:, :]
        sink = sink_ref[j:j + 1, part * gw:(part + 1) * gw]
        m = jnp.maximum(jnp.max(s, axis=0, keepdims=True), sink)
        e = jnp.exp2(s - m).astype(BF16)
        vwin = _unpack_rows(
            vt_ref[0, j * HEAD_DIM // 2:(j + 1) * HEAD_DIM // 2, pl.ds(k0, nk)])
        vext = jnp.concatenate([vwin, jnp.ones((ONES_ROWS, nk), BF16)], axis=0)
        o = jnp.dot(vext, e, preferred_element_type=F32)
        den = o[HEAD_DIM:HEAD_DIM + 1] + jnp.exp2(sink - m)
        o = o[:HEAD_DIM] * (1.0 / den)
        for u, h in enumerate(heads(j, part)):
            ot_ref[0, h * HEAD_DIM:(h + 1) * HEAD_DIM, pl.ds(q0, w)] = (
                o[:, u * w:(u + 1) * w].astype(BF16))

    def run(chains):
        n_slots = s_ref.shape[0]

        def park(t):
            nk = window(chains[t][0], chains[t][3])[0]
            s_ref[t % n_slots, :nk, :] = scores(*chains[t])

        for t in range(min(LOOKAHEAD, len(chains))):
            park(t)
        for t, ch in enumerate(chains):
            if t + LOOKAHEAD < len(chains):
                park(t + LOOKAHEAD)
            nk = window(ch[0], ch[3])[0]
            finish(*ch, s_ref[t % n_slots, :nk, :])

    parts = range(SW_GROUP // SW_PACK)
    run([(0, j, part, True) for j in range(n_kv) for part in parts])

    def body(t, carry):
        n0 = 1 + t * SWA_UNROLL
        run([(n0 + u, j, part, False)
             for u in range(SWA_UNROLL) for j in range(n_kv) for part in parts])
        return carry

    lax.fori_loop(0, (n_blocks - 1) // SWA_UNROLL, body, 0)


def _swa(qt, k, vt, sink_rows):
    b, d, s = qt.shape
    kv_dim = k.shape[-1]
    assert (s // WINDOW - 1) % SWA_UNROLL == 0
    return pl.pallas_call(
        _swa_kernel,
        grid=(b,),
        in_specs=[pl.BlockSpec((1, d, s), lambda i: (i, 0, 0)),
                  pl.BlockSpec((1, s // 2, kv_dim), lambda i: (i, 0, 0)),
                  pl.BlockSpec((1, kv_dim // 2, s), lambda i: (i, 0, 0)),
                  _resident(sink_rows.shape)],
        out_specs=pl.BlockSpec((1, d, s), lambda i: (i, 0, 0)),
        out_shape=jax.ShapeDtypeStruct((b, d, s), BF16),
        scratch_shapes=[pltpu.VMEM((2 * WINDOW, SW_PACK * WINDOW), F32),
                        pltpu.VMEM((LOOKAHEAD + 1, 2 * WINDOW, SW_PACK * WINDOW), F32)],
        compiler_params=pltpu.CompilerParams(
            dimension_semantics=("parallel",),
            vmem_limit_bytes=V7X_VMEM_LIMIT_BYTES),
        name="swa_attn",
    )(qt, k, vt, sink_rows)


def _lane_bcast(col, n):
    return jnp.broadcast_to(col.astype(F32).reshape(-1, 1), (col.size, n))


def kernel(x, positions, attn_norm, ffn_norm, w_gate_up, w_down, da_w_qkv, da_q_norm, da_k_norm,
           da_lambda, da_subln, da_w_o, kv_norm, w_kv, k_norm, sw_w_q, sw_q_norm, sw_sinks, sw_w_o):
    b, s, d = x.shape
    scale = 1.0 / math.sqrt(HEAD_DIM)

    inv = 1.0 / (ROPE_THETA ** (jnp.arange(0, HEAD_DIM, 2, dtype=F32) / HEAD_DIM))
    ang_t = positions.astype(F32)[:, None, :] * inv[None, :, None]
    cos_t = jnp.cos(ang_t)
    sin_t = jnp.sin(ang_t)

    lambda_init = 0.8 - 0.6 * math.exp(-0.3 * 0)
    qt, k, vt = _qkv0(x, attn_norm[0].reshape(1, d), da_w_qkv[0],
                      _lane_bcast(da_q_norm[0] * (scale * LOG2_E), ROW_TILE),
                      _lane_bcast(da_k_norm[0], ROW_TILE), cos_t, sin_t)
    at = _diff_attn(da_lambda[0].astype(F32), qt, k, vt,
                    _lane_bcast(da_subln[0] * (1.0 - lambda_init), ATTN_BLOCK), lambda_init)
    wgu = w_gate_up.astype(BF16)
    wd = w_down.astype(BF16)
    x = _proj_ffn(x, at, da_w_o[0].astype(BF16), ffn_norm[0].reshape(1, d), wgu, wd, 0)

    qt, k, vt = _qkv1(x, attn_norm[1].reshape(1, d), kv_norm.reshape(1, d),
                      sw_w_q[0], w_kv,
                      _lane_bcast(sw_q_norm[0] * (scale * LOG2_E), ROW_TILE),
                      _lane_bcast(k_norm, ROW_TILE), cos_t, sin_t)
    sink_rows = jnp.repeat((sw_sinks[0].astype(F32) * LOG2_E).reshape(SW_KV_HEADS, SW_GROUP),
                           WINDOW, axis=1)
    at = _swa(qt, k, vt, sink_rows)
    x = _proj_ffn(x, at, sw_w_o[0].astype(BF16), ffn_norm[1].reshape(1, d), wgu, wd, 1)
    return x
```

```python
import functools
import math

import jax
import jax.numpy as jnp
from jax import lax
from jax.experimental import pallas as pl
from jax.experimental.pallas import tpu as pltpu

HEAD_DIM = 64
ROPE_THETA = 10000.0
NORM_EPS = 1e-6
NEG_INF = -1e30
LOG2_E = math.log2(math.e)
WINDOW = 128
SW_KV_HEADS = 4
SW_GROUP = 4
SW_PACK = 2

F32 = jnp.float32
BF16 = jnp.bfloat16

V7X_VMEM_LIMIT_BYTES = 56 * 1024 * 1024

ROW_TILE = 512
FEATURE_CHUNK = 512
FFN_CHUNK = 256
ATTN_BLOCK = 256
ONES_ROWS = 16
LOOKAHEAD = 4
SWA_UNROLL = 3

NT_DIMS = (((1,), (1,)), ((), ()))
TN_DIMS = (((0,), (0,)), ((), ()))


def _resident(shape):
    nd = len(shape)
    return pl.BlockSpec(shape, lambda *_: (0,) * nd, pipeline_mode=pl.Buffered(1))


def _rms_scale(x):
    return lax.rsqrt(jnp.mean(x * x, axis=-1, keepdims=True) + NORM_EPS)


def _pack_rows(x):
    return pltpu.bitcast(x, jnp.uint32)


def _unpack_rows(x):
    return pltpu.bitcast(x, BF16)


def _transpose_pack_weight(w_ref, wt_ref):
    n = w_ref.shape[1]
    for c in range(n // FEATURE_CHUNK):
        cols = slice(c * FEATURE_CHUNK, (c + 1) * FEATURE_CHUNK)
        rows = slice(c * FEATURE_CHUNK // 2, (c + 1) * FEATURE_CHUNK // 2)
        wt_ref[rows, :] = _pack_rows(w_ref[:, cols].T.astype(BF16))


def _norm_rope_t(t, gain, cos, sin):
    r = lax.rsqrt(jnp.mean(t * t, axis=0, keepdims=True) + NORM_EPS)
    tn = t * r * gain
    x1 = tn[: HEAD_DIM // 2]
    x2 = tn[HEAD_DIM // 2:]
    return jnp.concatenate([x1 * cos - x2 * sin, x2 * cos + x1 * sin], axis=0)


def _qkv0_kernel(x_ref, g_ref, w_ref, qg_ref, kg_ref, cos_ref, sin_ref,
                 wgu_ref, wd_ref, wo0_ref, wo1_ref,
                 qt_ref, k_ref, vt_ref, wgu_bf_ref, wd_bf_ref, wo0_bf_ref, wo1_bf_ref,
                 wt_ref, res_ref):
    d_model = x_ref.shape[-1]

    @pl.when((pl.program_id(0) == 0) & (pl.program_id(1) == 0))
    def _():
        _transpose_pack_weight(w_ref, wt_ref)

    @pl.when(pl.program_id(1) == 0)
    def _():
        wd_bf_ref[...] = wd_ref[...].astype(BF16)

    wgu_bf_ref[...] = wgu_ref[...].astype(BF16)
    wo0_bf_ref[...] = wo0_ref[...].astype(BF16)
    wo1_bf_ref[...] = wo1_ref[...].astype(BF16)

    x = x_ref[0]
    h = (x * _rms_scale(x) * g_ref[...]).astype(BF16)
    cos = cos_ref[0]
    sin = sin_ref[0]
    n_chunks = d_model // FEATURE_CHUNK

    def project(c):
        rows = slice(c * FEATURE_CHUNK // 2, (c + 1) * FEATURE_CHUNK // 2)
        w_rows = _unpack_rows(wt_ref[rows, :])
        res_ref[c % 2] = lax.dot_general(w_rows, h, NT_DIMS, preferred_element_type=F32)

    project(0)
    for c in range(3 * n_chunks):
        if c + 1 < 3 * n_chunks:
            project(c + 1)
        res = res_ref.at[c % 2]
        kind, cc = divmod(c, n_chunks)
        out_rows = slice(cc * FEATURE_CHUNK, (cc + 1) * FEATURE_CHUNK)
        half_rows = slice(cc * FEATURE_CHUNK // 2, (cc + 1) * FEATURE_CHUNK // 2)
        if kind == 2:
            vt_ref[0, half_rows, :] = _pack_rows(res[...].astype(BF16))
            continue
        gain_ref = qg_ref if kind == 0 else kg_ref
        parts = []
        for g in range(FEATURE_CHUNK // HEAD_DIM):
            half = g % 2
            gain = gain_ref[half * HEAD_DIM:(half + 1) * HEAD_DIM, :]
            parts.append(_norm_rope_t(res[g * HEAD_DIM:(g + 1) * HEAD_DIM, :], gain, cos, sin))
        out = jnp.concatenate(parts, axis=0)
        if kind == 0:
            qt_ref[0, out_rows, :] = out.astype(BF16)
        else:
            k_ref[0, :, out_rows] = _pack_rows(out.T.astype(BF16))


def _qkv0(x, g, w, qg, kg, cos_t, sin_t, wgu, wd, wo0, wo1):
    b, s, d = x.shape
    tm = ROW_TILE
    nt = s // tm
    grid = (b, nt)

    def per_step(a):
        return pl.BlockSpec((a.shape[0] // (b * nt), a.shape[1]), lambda i, j: (i * nt + j, 0))

    def per_batch(a):
        return pl.BlockSpec((a.shape[0] // b, a.shape[1]), lambda i, j: (i, 0))

    def as_bf16(a):
        return jax.ShapeDtypeStruct(a.shape, BF16)
    feat = pl.BlockSpec((1, d, tm), lambda i, j: (i, 0, j))
    feat_packed = pl.BlockSpec((1, d // 2, tm), lambda i, j: (i, 0, j))
    tok = pl.BlockSpec((1, tm, d), lambda i, j: (i, j, 0))
    tok_packed = pl.BlockSpec((1, tm // 2, d), lambda i, j: (i, j, 0))
    rope = pl.BlockSpec((1, HEAD_DIM // 2, tm), lambda i, j: (i, 0, j))
    return pl.pallas_call(
        _qkv0_kernel,
        grid=grid,
        in_specs=[tok, _resident((1, d)), _resident(w.shape), _resident(qg.shape),
                  _resident(kg.shape), rope, rope,
                  per_step(wgu), per_batch(wd), per_step(wo0), per_step(wo1)],
        out_specs=[feat, tok_packed, feat_packed,
                   per_step(wgu), per_batch(wd), per_step(wo0), per_step(wo1)],
        out_shape=[jax.ShapeDtypeStruct((b, d, s), BF16),
                   jax.ShapeDtypeStruct((b, s // 2, d), jnp.uint32),
                   jax.ShapeDtypeStruct((b, d // 2, s), jnp.uint32),
                   as_bf16(wgu), as_bf16(wd), as_bf16(wo0), as_bf16(wo1)],
        scratch_shapes=[pltpu.VMEM((w.shape[1] // 2, d), jnp.uint32),
                        pltpu.VMEM((2, FEATURE_CHUNK, tm), F32)],
        compiler_params=pltpu.CompilerParams(
            dimension_semantics=("arbitrary", "arbitrary"),
            vmem_limit_bytes=V7X_VMEM_LIMIT_BYTES),
        name="qkv0_proj",
    )(x, g, w, qg, kg, cos_t, sin_t, wgu, wd, wo0, wo1)


def _diff_attn_kernel(lam_ref, qt_ref, k_ref, vt_ref, sg_ref, ot_ref,
                      qz_ref, m_ref, acc_ref, s_ref, *, lambda_init):
    bq = qt_ref.shape[-1]
    hd2 = 2 * HEAD_DIM
    n_heads = qt_ref.shape[1] // hd2
    i = pl.program_id(1)

    lp = lam_ref[...]
    lam = (jnp.exp(jnp.sum(lp[0:1] * lp[1:2], axis=-1, keepdims=True))
           - jnp.exp(jnp.sum(lp[2:3] * lp[3:4], axis=-1, keepdims=True)) + lambda_init)

    zero = jnp.zeros((HEAD_DIM, bq), BF16)
    for h in range(n_heads):
        q0 = qt_ref[0, h * hd2:h * hd2 + HEAD_DIM, :]
        q1 = qt_ref[0, h * hd2 + HEAD_DIM:(h + 1) * hd2, :]
        qz_ref[2 * h] = jnp.concatenate([q0, zero], axis=0)
        qz_ref[2 * h + 1] = jnp.concatenate([zero, q1], axis=0)

    def block(k0, bk, first):
        ones = jnp.ones((ONES_ROWS, bk), BF16)
        k0_packed = pl.multiple_of(lax.shift_right_logical(k0, 1), bk // 2)
        if first:
            kidx = lax.broadcasted_iota(jnp.int32, (bk, bq), 0)
            qidx = lax.broadcasted_iota(jnp.int32, (bk, bq), 1)
            keep = kidx <= qidx + (bk - bq)

        def scores(hc):
            h = hc // 2
            kblk = _unpack_rows(k_ref[0, pl.ds(k0_packed, bk // 2), h * hd2:(h + 1) * hd2])
            return jnp.dot(kblk, qz_ref[hc], preferred_element_type=F32)

        n_slots = s_ref.shape[0]

        def park(hc):
            s_ref[hc % n_slots, :bk, :] = scores(hc)

        for hc in range(LOOKAHEAD):
            park(hc)
        for hc in range(2 * n_heads):
            h = hc // 2
            if hc + LOOKAHEAD < 2 * n_heads:
                park(hc + LOOKAHEAD)
            s = s_ref[hc % n_slots, :bk, :]
            vblk = _unpack_rows(vt_ref[0, h * HEAD_DIM:(h + 1) * HEAD_DIM, pl.ds(k0, bk)])
            vext = jnp.concatenate([vblk, ones], axis=0)
            if first:
                s = jnp.where(keep, s, NEG_INF)
                m_new = jnp.max(s, axis=0, keepdims=True)
                p = jnp.exp2(s - m_new)
                acc_ref[hc] = jnp.dot(vext, p.astype(BF16), preferred_element_type=F32)
            else:
                m_old = m_ref[hc]
                m_new = jnp.maximum(m_old, jnp.max(s, axis=0, keepdims=True))
                alpha = jnp.exp2(m_old - m_new)
                p = jnp.exp2(s - m_new)
                acc_ref[hc] = alpha * acc_ref[hc] + jnp.dot(
                    vext, p.astype(BF16), preferred_element_type=F32)
            m_ref[hc] = m_new

    @pl.when((i & 1) == 0)
    def _():
        block(pl.multiple_of(i * bq, bq), bq, True)

    @pl.when((i & 1) == 1)
    def _():
        block(pl.multiple_of((i - 1) * bq, 2 * bq), 2 * bq, True)

    def body(j, carry):
        block(pl.multiple_of(j * 2 * bq, 2 * bq), 2 * bq, False)
        return carry

    lax.fori_loop(0, lax.shift_right_logical(i, 1), body, 0)

    for h in range(n_heads):
        a0 = acc_ref[2 * h]
        a1 = acc_ref[2 * h + 1]
        inv0 = 1.0 / a0[hd2:hd2 + 1]
        inv1 = lam / a1[hd2:hd2 + 1]
        o = a0[:hd2] * inv0 - a1[:hd2] * inv1
        r = lax.rsqrt(jnp.mean(o * o, axis=0, keepdims=True) + NORM_EPS)
        ot_ref[0, h * hd2:(h + 1) * hd2, :] = (
            o * r * sg_ref[...]).astype(BF16)


def _diff_attn(lam_p, qt, k, vt, sg, lambda_init):
    b, d, s = qt.shape
    hd2 = 2 * HEAD_DIM
    bq = ATTN_BLOCK
    n_half = d // HEAD_DIM
    return pl.pallas_call(
        functools.partial(_diff_attn_kernel, lambda_init=lambda_init),
        grid=(b, s // bq),
        in_specs=[_resident(lam_p.shape),
                  pl.BlockSpec((1, d, bq), lambda bi, i: (bi, 0, i)),
                  pl.BlockSpec((1, s // 2, d), lambda bi, i: (bi, 0, 0)),
                  pl.BlockSpec((1, d // 2, s), lambda bi, i: (bi, 0, 0)),
                  _resident(sg.shape)],
        out_specs=pl.BlockSpec((1, d, bq), lambda bi, i: (bi, 0, i)),
        out_shape=jax.ShapeDtypeStruct((b, d, s), BF16),
        scratch_shapes=[pltpu.VMEM((n_half, hd2, bq), BF16),
                        pltpu.VMEM((n_half, 1, bq), F32),
                        pltpu.VMEM((n_half, hd2 + ONES_ROWS, bq), F32),
                        pltpu.VMEM((LOOKAHEAD + 1, 2 * bq, bq), F32)],
        compiler_params=pltpu.CompilerParams(
            dimension_semantics=("parallel", "arbitrary"),
            vmem_limit_bytes=V7X_VMEM_LIMIT_BYTES),
        name="diff_attn",
    )(lam_p, qt, k, vt, sg)


def _proj_ffn_kernel(x_ref, at_ref, wo_ref, g_ref, wg_ref, wu_ref, wd_ref, o_ref, hid_ref):
    x = x_ref[0] + lax.dot_general(at_ref[0], wo_ref[...], TN_DIMS,
                                   preferred_element_type=F32)
    h = (x * _rms_scale(x) * g_ref[...]).astype(BF16)
    n_chunks = hid_ref.shape[-1] // FFN_CHUNK
    for c in range(n_chunks):
        cols = slice(c * FFN_CHUNK, (c + 1) * FFN_CHUNK)
        gate = jnp.dot(h, wg_ref[:, cols], preferred_element_type=F32)
        up = jnp.dot(h, wu_ref[:, cols], preferred_element_type=F32)
        hid_ref[:, c * FFN_CHUNK:(c + 1) * FFN_CHUNK] = (
            gate * jax.nn.sigmoid(gate) * up).astype(BF16)
    o_ref[0] = x + jnp.dot(hid_ref[...], wd_ref[...], preferred_element_type=F32)


def _proj_ffn(x, at, wo, g, wgu, wd, layer):
    b, s, d = x.shape
    tm = ROW_TILE
    hidden = wd.shape[1]
    tok = pl.BlockSpec((1, tm, d), lambda i, j: (i, j, 0))
    feat = pl.BlockSpec((1, d, tm), lambda i, j: (i, 0, j))
    once = pl.Buffered(1)
    return pl.pallas_call(
        _proj_ffn_kernel,
        grid=(b, s // tm),
        in_specs=[tok, feat, _resident(wo.shape), _resident((1, d)),
                  pl.BlockSpec((None, d, hidden), lambda i, j: (layer, 0, 0), pipeline_mode=once),
                  pl.BlockSpec((None, d, hidden), lambda i, j: (layer, 0, 1), pipeline_mode=once),
                  pl.BlockSpec((None, hidden, d), lambda i, j: (layer, 0, 0), pipeline_mode=once)],
        out_specs=tok,
        out_shape=jax.ShapeDtypeStruct((b, s, d), F32),
        scratch_shapes=[pltpu.VMEM((tm, hidden), BF16)],
        compiler_params=pltpu.CompilerParams(
            dimension_semantics=("parallel", "parallel"),
            vmem_limit_bytes=V7X_VMEM_LIMIT_BYTES),
        name="proj_ffn",
    )(x, at, wo, g, wgu, wgu, wd)


def _qkv1_kernel(x_ref, ga_ref, gkv_ref, wq_ref, wkv_ref, qg_ref, kg_ref, cos_ref, sin_ref,
                 qt_ref, k_ref, vt_ref, wqt_ref, wkvt_ref, kv_ref, res_ref):
    d_model = x_ref.shape[-1]
    kv_dim = k_ref.shape[-1]

    @pl.when((pl.program_id(0) == 0) & (pl.program_id(1) == 0))
    def _():
        _transpose_pack_weight(wq_ref, wqt_ref)
        _transpose_pack_weight(wkv_ref, wkvt_ref)

    x = x_ref[0]
    xn = x * _rms_scale(x)
    h_a = (xn * ga_ref[...]).astype(BF16)
    h_kv = (xn * gkv_ref[...]).astype(BF16)
    cos = cos_ref[0]
    sin = sin_ref[0]

    n_chunks = d_model // FEATURE_CHUNK

    def project_q(c):
        rows = slice(c * FEATURE_CHUNK // 2, (c + 1) * FEATURE_CHUNK // 2)
        res_ref[c % 2] = lax.dot_general(_unpack_rows(wqt_ref[rows, :]), h_a, NT_DIMS,
                                         preferred_element_type=F32)

    kv_ref[...] = lax.dot_general(_unpack_rows(wkvt_ref[...]), h_kv, NT_DIMS,
                                  preferred_element_type=F32)
    project_q(0)
    kparts = [_norm_rope_t(kv_ref[g * HEAD_DIM:(g + 1) * HEAD_DIM, :], kg_ref[...], cos, sin)
              for g in range(kv_dim // HEAD_DIM)]
    k_ref[0] = _pack_rows(jnp.concatenate(kparts, axis=0).T.astype(BF16))
    vt_ref[0] = _pack_rows(kv_ref[kv_dim:, :].astype(BF16))

    for c in range(n_chunks):
        if c + 1 < n_chunks:
            project_q(c + 1)
        res = res_ref.at[c % 2]
        rows = slice(c * FEATURE_CHUNK, (c + 1) * FEATURE_CHUNK)
        parts = [_norm_rope_t(res[g * HEAD_DIM:(g + 1) * HEAD_DIM, :], qg_ref[...], cos, sin)
                 for g in range(FEATURE_CHUNK // HEAD_DIM)]
        qt_ref[0, rows, :] = jnp.concatenate(parts, axis=0).astype(BF16)


def _qkv1(x, ga, gkv, wq, wkv, qg, kg, cos_t, sin_t):
    b, s, d = x.shape
    tm = ROW_TILE
    kv_dim = wkv.shape[1] // 2
    tok = pl.BlockSpec((1, tm, d), lambda i, j: (i, j, 0))
    rope = pl.BlockSpec((1, HEAD_DIM // 2, tm), lambda i, j: (i, 0, j))
    return pl.pallas_call(
        _qkv1_kernel,
        grid=(b, s // tm),
        in_specs=[tok, _resident((1, d)), _resident((1, d)), _resident(wq.shape),
                  _resident(wkv.shape), _resident(qg.shape), _resident(kg.shape), rope, rope],
        out_specs=[pl.BlockSpec((1, d, tm), lambda i, j: (i, 0, j)),
                   pl.BlockSpec((1, tm // 2, kv_dim), lambda i, j: (i, j, 0)),
                   pl.BlockSpec((1, kv_dim // 2, tm), lambda i, j: (i, 0, j))],
        out_shape=[jax.ShapeDtypeStruct((b, d, s), BF16),
                   jax.ShapeDtypeStruct((b, s // 2, kv_dim), jnp.uint32),
                   jax.ShapeDtypeStruct((b, kv_dim // 2, s), jnp.uint32)],
        scratch_shapes=[pltpu.VMEM((wq.shape[1] // 2, d), jnp.uint32),
                        pltpu.VMEM((kv_dim, d), jnp.uint32),
                        pltpu.VMEM((2 * kv_dim, tm), F32),
                        pltpu.VMEM((2, FEATURE_CHUNK, tm), F32)],
        compiler_params=pltpu.CompilerParams(
            dimension_semantics=("arbitrary", "arbitrary"),
            vmem_limit_bytes=V7X_VMEM_LIMIT_BYTES),
        name="qkv1_proj",
    )(x, ga, gkv, wq, wkv, qg, kg, cos_t, sin_t)


def _swa_kernel(qt_ref, k_ref, vt_ref, sink_ref, ot_ref, bias_ref, s_ref):
    w = WINDOW
    s_len = qt_ref.shape[-1]
    kv_dim = k_ref.shape[-1]
    gw = SW_PACK * w

    n_kv = kv_dim // HEAD_DIM
    n_blocks = s_len // w

    kidx = lax.broadcasted_iota(jnp.int32, (2 * w, gw), 0)
    qidx = lax.broadcasted_iota(jnp.int32, (2 * w, gw), 1) & (w - 1)
    bias_ref[...] = jnp.where((kidx > qidx) & (kidx <= qidx + w), 0.0, NEG_INF).astype(F32)

    def window(n, first):
        nk = w if first else 2 * w
        k0 = 0 if first else pl.multiple_of((n - 1) * w, w)
        q0 = 0 if first else pl.multiple_of(n * w, w)
        return nk, k0, q0

    def heads(j, part):
        return [SW_GROUP * j + SW_PACK * part + u for u in range(SW_PACK)]

    def scores(n, j, part, first):
        nk, k0, q0 = window(n, first)
        k0_packed = k0 if first else pl.multiple_of((n - 1) * (w // 2), w // 2)
        kwin = _unpack_rows(k_ref[0, pl.ds(k0_packed, nk // 2), :])
        qcat = jnp.concatenate(
            [qt_ref[0, h * HEAD_DIM:(h + 1) * HEAD_DIM, pl.ds(q0, w)]
             for h in heads(j, part)], axis=1)
        pieces = []
        if j > 0:
            pieces.append(jnp.zeros((j * HEAD_DIM, gw), BF16))
        pieces.append(qcat)
        if j + 1 < n_kv:
            pieces.append(jnp.zeros(((n_kv - j - 1) * HEAD_DIM, gw), BF16))
        qz = jnp.concatenate(pieces, axis=0) if len(pieces) > 1 else qcat
        return jnp.dot(kwin, qz, preferred_element_type=F32)

    def finish(n, j, part, first, s):
        nk, k0, q0 = window(n, first)
        s = s + bias_ref[2 * w - nk:, :]
        sink = sink_ref[j:j + 1, part * gw:(part + 1) * gw]
        m = jnp.maximum(jnp.max(s, axis=0, keepdims=True), sink)
        e = jnp.exp2(s - m).astype(BF16)
        vwin = _unpack_rows(
            vt_ref[0, j * HEAD_DIM // 2:(j + 1) * HEAD_DIM // 2, pl.ds(k0, nk)])
        vext = jnp.concatenate([vwin, jnp.ones((ONES_ROWS, nk), BF16)], axis=0)
        o = jnp.dot(vext, e, preferred_element_type=F32)
        den = o[HEAD_DIM:HEAD_DIM + 1] + jnp.exp2(sink - m)
        o = o[:HEAD_DIM] * (1.0 / den)
        for u, h in enumerate(heads(j, part)):
            ot_ref[0, h * HEAD_DIM:(h + 1) * HEAD_DIM, pl.ds(q0, w)] = (
                o[:, u * w:(u + 1) * w].astype(BF16))

    def run(chains):
        n_slots = s_ref.shape[0]

        def park(t):
            nk = window(chains[t][0], chains[t][3])[0]
            s_ref[t % n_slots, :nk, :] = scores(*chains[t])

        for t in range(min(LOOKAHEAD, len(chains))):
            park(t)
        for t, ch in enumerate(chains):
            if t + LOOKAHEAD < len(chains):
                park(t + LOOKAHEAD)
            nk = window(ch[0], ch[3])[0]
            finish(*ch, s_ref[t % n_slots, :nk, :])

    parts = range(SW_GROUP // SW_PACK)
    run([(0, j, part, True) for j in range(n_kv) for part in parts])

    def body(t, carry):
        n0 = 1 + t * SWA_UNROLL
        run([(n0 + u, j, part, False)
             for u in range(SWA_UNROLL) for j in range(n_kv) for part in parts])
        return carry

    lax.fori_loop(0, (n_blocks - 1) // SWA_UNROLL, body, 0)


def _swa(qt, k, vt, sink_rows):
    b, d, s = qt.shape
    kv_dim = k.shape[-1]
    assert (s // WINDOW - 1) % SWA_UNROLL == 0
    return pl.pallas_call(
        _swa_kernel,
        grid=(b,),
        in_specs=[pl.BlockSpec((1, d, s), lambda i: (i, 0, 0)),
                  pl.BlockSpec((1, s // 2, kv_dim), lambda i: (i, 0, 0)),
                  pl.BlockSpec((1, kv_dim // 2, s), lambda i: (i, 0, 0)),
                  _resident(sink_rows.shape)],
        out_specs=pl.BlockSpec((1, d, s), lambda i: (i, 0, 0)),
        out_shape=jax.ShapeDtypeStruct((b, d, s), BF16),
        scratch_shapes=[pltpu.VMEM((2 * WINDOW, SW_PACK * WINDOW), F32),
                        pltpu.VMEM((LOOKAHEAD + 1, 2 * WINDOW, SW_PACK * WINDOW), F32)],
        compiler_params=pltpu.CompilerParams(
            dimension_semantics=("parallel",),
            vmem_limit_bytes=V7X_VMEM_LIMIT_BYTES),
        name="swa_attn",
    )(qt, k, vt, sink_rows)


def _lane_bcast(col, n):
    return jnp.broadcast_to(col.astype(F32).reshape(-1, 1), (col.size, n))


def kernel(x, positions, attn_norm, ffn_norm, w_gate_up, w_down, da_w_qkv, da_q_norm, da_k_norm,
           da_lambda, da_subln, da_w_o, kv_norm, w_kv, k_norm, sw_w_q, sw_q_norm, sw_sinks, sw_w_o):
    b, s, d = x.shape
    scale = 1.0 / math.sqrt(HEAD_DIM)

    inv = 1.0 / (ROPE_THETA ** (jnp.arange(0, HEAD_DIM, 2, dtype=F32) / HEAD_DIM))
    ang_t = positions.astype(F32)[:, None, :] * inv[None, :, None]
    cos_t = jnp.cos(ang_t)
    sin_t = jnp.sin(ang_t)

    lambda_init = 0.8 - 0.6 * math.exp(-0.3 * 0)
    n_layers, _, two_h = w_gate_up.shape
    qt, k, vt, wgu, wd, wo0, wo1 = _qkv0(
        x, attn_norm[0].reshape(1, d), da_w_qkv[0],
        _lane_bcast(da_q_norm[0] * (scale * LOG2_E), ROW_TILE),
        _lane_bcast(da_k_norm[0], ROW_TILE), cos_t, sin_t,
        w_gate_up.reshape(n_layers * d, two_h), w_down.reshape(n_layers * (two_h // 2), d),
        da_w_o[0], sw_w_o[0])
    wgu = wgu.reshape(n_layers, d, two_h)
    wd = wd.reshape(n_layers, two_h // 2, d)
    at = _diff_attn(da_lambda[0].astype(F32), qt, k, vt,
                    _lane_bcast(da_subln[0] * (1.0 - lambda_init), ATTN_BLOCK), lambda_init)
    x = _proj_ffn(x, at, wo0, ffn_norm[0].reshape(1, d), wgu, wd, 0)

    qt, k, vt = _qkv1(x, attn_norm[1].reshape(1, d), kv_norm.reshape(1, d),
                      sw_w_q[0], w_kv,
                      _lane_bcast(sw_q_norm[0] * (scale * LOG2_E), ROW_TILE),
                      _lane_bcast(k_norm, ROW_TILE), cos_t, sin_t)
    sink_rows = jnp.repeat((sw_sinks[0].astype(F32) * LOG2_E).reshape(SW_KV_HEADS, SW_GROUP),
                           WINDOW, axis=1)
    at = _swa(qt, k, vt, sink_rows)
    x = _proj_ffn(x, at, wo1, ffn_norm[1].reshape(1, d), wgu, wd, 1)
    return x
```

```python
import functools
import math

import jax
import jax.numpy as jnp
from jax import lax
from jax.experimental import pallas as pl
from jax.experimental.pallas import tpu as pltpu

HEAD_DIM = 64
ROPE_THETA = 10000.0
NORM_EPS = 1e-6
NEG_INF = -1e30
LOG2_E = math.log2(math.e)
WINDOW = 128
SW_KV_HEADS = 4
SW_GROUP = 4
SW_PACK = 2

F32 = jnp.float32
BF16 = jnp.bfloat16

V7X_VMEM_LIMIT_BYTES = 56 * 1024 * 1024

ROW_TILE = 512
FEATURE_CHUNK = 512
FFN_CHUNK = 256
ATTN_BLOCK = 256
ONES_ROWS = 16
LOOKAHEAD = 6
SWA_UNROLL = 5

NT_DIMS = (((1,), (1,)), ((), ()))
TN_DIMS = (((0,), (0,)), ((), ()))


def _resident(shape):
    nd = len(shape)
    return pl.BlockSpec(shape, lambda *_: (0,) * nd, pipeline_mode=pl.Buffered(1))


def _rms_scale(x):
    return lax.rsqrt(jnp.mean(x * x, axis=-1, keepdims=True) + NORM_EPS)


def _pack_rows(x):
    return pltpu.bitcast(x, jnp.uint32)


def _unpack_rows(x):
    return pltpu.bitcast(x, BF16)


def _transpose_pack_weight(w_ref, wt_ref):
    n = w_ref.shape[1]
    for c in range(n // FEATURE_CHUNK):
        cols = slice(c * FEATURE_CHUNK, (c + 1) * FEATURE_CHUNK)
        rows = slice(c * FEATURE_CHUNK // 2, (c + 1) * FEATURE_CHUNK // 2)
        wt_ref[rows, :] = _pack_rows(w_ref[:, cols].T.astype(BF16))


def _norm_rope_t(t, gain, cos, sin):
    r = lax.rsqrt(jnp.mean(t * t, axis=0, keepdims=True) + NORM_EPS)
    tn = t * r * gain
    x1 = tn[: HEAD_DIM // 2]
    x2 = tn[HEAD_DIM // 2:]
    return jnp.concatenate([x1 * cos - x2 * sin, x2 * cos + x1 * sin], axis=0)


def _qkv0_kernel(x_ref, g_ref, w_ref, qg_ref, kg_ref, cos_ref, sin_ref,
                 wgu_ref, wd_ref, wo0_ref, wo1_ref,
                 qt_ref, k_ref, vt_ref, wgu_bf_ref, wd_bf_ref, wo0_bf_ref, wo1_bf_ref,
                 wt_ref, res_ref):
    d_model = x_ref.shape[-1]

    @pl.when((pl.program_id(0) == 0) & (pl.program_id(1) == 0))
    def _():
        _transpose_pack_weight(w_ref, wt_ref)

    @pl.when(pl.program_id(1) == 0)
    def _():
        wd_bf_ref[...] = wd_ref[...].astype(BF16)

    wgu_bf_ref[...] = wgu_ref[...].astype(BF16)
    wo0_bf_ref[...] = wo0_ref[...].astype(BF16)
    wo1_bf_ref[...] = wo1_ref[...].astype(BF16)

    x = x_ref[0]
    h = (x * _rms_scale(x) * g_ref[...]).astype(BF16)
    cos = cos_ref[0]
    sin = sin_ref[0]
    n_chunks = d_model // FEATURE_CHUNK

    def project(c):
        rows = slice(c * FEATURE_CHUNK // 2, (c + 1) * FEATURE_CHUNK // 2)
        w_rows = _unpack_rows(wt_ref[rows, :])
        res_ref[c % 2] = lax.dot_general(w_rows, h, NT_DIMS, preferred_element_type=F32)

    project(0)
    for c in range(3 * n_chunks):
        if c + 1 < 3 * n_chunks:
            project(c + 1)
        res = res_ref.at[c % 2]
        kind, cc = divmod(c, n_chunks)
        out_rows = slice(cc * FEATURE_CHUNK, (cc + 1) * FEATURE_CHUNK)
        half_rows = slice(cc * FEATURE_CHUNK // 2, (cc + 1) * FEATURE_CHUNK // 2)
        if kind == 2:
            vt_ref[0, half_rows, :] = _pack_rows(res[...].astype(BF16))
            continue
        gain_ref = qg_ref if kind == 0 else kg_ref
        parts = []
        for g in range(FEATURE_CHUNK // HEAD_DIM):
            half = g % 2
            gain = gain_ref[half * HEAD_DIM:(half + 1) * HEAD_DIM, :]
            parts.append(_norm_rope_t(res[g * HEAD_DIM:(g + 1) * HEAD_DIM, :], gain, cos, sin))
        out = jnp.concatenate(parts, axis=0)
        if kind == 0:
            qt_ref[0, out_rows, :] = out.astype(BF16)
        else:
            k_ref[0, :, out_rows] = _pack_rows(out.T.astype(BF16))


def _qkv0(x, g, w, qg, kg, cos_t, sin_t, wgu, wd, wo0, wo1):
    b, s, d = x.shape
    tm = ROW_TILE
    nt = s // tm
    grid = (b, nt)

    def per_step(a):
        return pl.BlockSpec((a.shape[0] // (b * nt), a.shape[1]), lambda i, j: (i * nt + j, 0))

    def per_batch(a):
        return pl.BlockSpec((a.shape[0] // b, a.shape[1]), lambda i, j: (i, 0))

    def as_bf16(a):
        return jax.ShapeDtypeStruct(a.shape, BF16)
    feat = pl.BlockSpec((1, d, tm), lambda i, j: (i, 0, j))
    feat_packed = pl.BlockSpec((1, d // 2, tm), lambda i, j: (i, 0, j))
    tok = pl.BlockSpec((1, tm, d), lambda i, j: (i, j, 0))
    tok_packed = pl.BlockSpec((1, tm // 2, d), lambda i, j: (i, j, 0))
    rope = pl.BlockSpec((1, HEAD_DIM // 2, tm), lambda i, j: (i, 0, j))
    return pl.pallas_call(
        _qkv0_kernel,
        grid=grid,
        in_specs=[tok, _resident((1, d)), _resident(w.shape), _resident(qg.shape),
                  _resident(kg.shape), rope, rope,
                  per_step(wgu), per_batch(wd), per_step(wo0), per_step(wo1)],
        out_specs=[feat, tok_packed, feat_packed,
                   per_step(wgu), per_batch(wd), per_step(wo0), per_step(wo1)],
        out_shape=[jax.ShapeDtypeStruct((b, d, s), BF16),
                   jax.ShapeDtypeStruct((b, s // 2, d), jnp.uint32),
                   jax.ShapeDtypeStruct((b, d // 2, s), jnp.uint32),
                   as_bf16(wgu), as_bf16(wd), as_bf16(wo0), as_bf16(wo1)],
        scratch_shapes=[pltpu.VMEM((w.shape[1] // 2, d), jnp.uint32),
                        pltpu.VMEM((2, FEATURE_CHUNK, tm), F32)],
        compiler_params=pltpu.CompilerParams(
            dimension_semantics=("arbitrary", "arbitrary"),
            vmem_limit_bytes=V7X_VMEM_LIMIT_BYTES),
        name="qkv0_proj",
    )(x, g, w, qg, kg, cos_t, sin_t, wgu, wd, wo0, wo1)


def _diff_attn_kernel(lam_ref, qt_ref, k_ref, vt_ref, sg_ref, ot_ref,
                      qz_ref, m_ref, acc_ref, s_ref, *, lambda_init):
    bq = qt_ref.shape[-1]
    hd2 = 2 * HEAD_DIM
    n_heads = qt_ref.shape[1] // hd2
    i = pl.program_id(1)

    lp = lam_ref[...]
    lam = (jnp.exp(jnp.sum(lp[0:1] * lp[1:2], axis=-1, keepdims=True))
           - jnp.exp(jnp.sum(lp[2:3] * lp[3:4], axis=-1, keepdims=True)) + lambda_init)

    zero = jnp.zeros((HEAD_DIM, bq), BF16)
    for h in range(n_heads):
        q0 = qt_ref[0, h * hd2:h * hd2 + HEAD_DIM, :]
        q1 = qt_ref[0, h * hd2 + HEAD_DIM:(h + 1) * hd2, :]
        qz_ref[2 * h] = jnp.concatenate([q0, zero], axis=0)
        qz_ref[2 * h + 1] = jnp.concatenate([zero, q1], axis=0)

    def block(k0, bk, first):
        ones = jnp.ones((ONES_ROWS, bk), BF16)
        k0_packed = pl.multiple_of(lax.shift_right_logical(k0, 1), bk // 2)
        if first:
            kidx = lax.broadcasted_iota(jnp.int32, (bk, bq), 0)
            qidx = lax.broadcasted_iota(jnp.int32, (bk, bq), 1)
            keep = kidx <= qidx + (bk - bq)

        def scores(hc):
            h = hc // 2
            kblk = _unpack_rows(k_ref[0, pl.ds(k0_packed, bk // 2), h * hd2:(h + 1) * hd2])
            return jnp.dot(kblk, qz_ref[hc], preferred_element_type=F32)

        n_slots = s_ref.shape[0]

        def park(hc):
            s_ref[hc % n_slots, :bk, :] = scores(hc)

        for hc in range(LOOKAHEAD):
            park(hc)
        for hc in range(2 * n_heads):
            h = hc // 2
            if hc + LOOKAHEAD < 2 * n_heads:
                park(hc + LOOKAHEAD)
            s = s_ref[hc % n_slots, :bk, :]
            vblk = _unpack_rows(vt_ref[0, h * HEAD_DIM:(h + 1) * HEAD_DIM, pl.ds(k0, bk)])
            vext = jnp.concatenate([vblk, ones], axis=0)
            if first:
                s = jnp.where(keep, s, NEG_INF)
                m_new = jnp.max(s, axis=0, keepdims=True)
                p = jnp.exp2(s - m_new)
                acc_ref[hc] = jnp.dot(vext, p.astype(BF16), preferred_element_type=F32)
            else:
                m_old = m_ref[hc]
                m_new = jnp.maximum(m_old, jnp.max(s, axis=0, keepdims=True))
                alpha = jnp.exp2(m_old - m_new)
                p = jnp.exp2(s - m_new)
                acc_ref[hc] = alpha * acc_ref[hc] + jnp.dot(
                    vext, p.astype(BF16), preferred_element_type=F32)
            m_ref[hc] = m_new

    @pl.when((i & 1) == 0)
    def _():
        block(pl.multiple_of(i * bq, bq), bq, True)

    @pl.when((i & 1) == 1)
    def _():
        block(pl.multiple_of((i - 1) * bq, 2 * bq), 2 * bq, True)

    def body(j, carry):
        block(pl.multiple_of(j * 2 * bq, 2 * bq), 2 * bq, False)
        return carry

    lax.fori_loop(0, lax.shift_right_logical(i, 1), body, 0)

    for h in range(n_heads):
        a0 = acc_ref[2 * h]
        a1 = acc_ref[2 * h + 1]
        inv0 = 1.0 / a0[hd2:hd2 + 1]
        inv1 = lam / a1[hd2:hd2 + 1]
        o = a0[:hd2] * inv0 - a1[:hd2] * inv1
        r = lax.rsqrt(jnp.mean(o * o, axis=0, keepdims=True) + NORM_EPS)
        ot_ref[0, h * hd2:(h + 1) * hd2, :] = (
            o * r * sg_ref[...]).astype(BF16)


def _diff_attn(lam_p, qt, k, vt, sg, lambda_init):
    b, d, s = qt.shape
    hd2 = 2 * HEAD_DIM
    bq = ATTN_BLOCK
    n_half = d // HEAD_DIM
    return pl.pallas_call(
        functools.partial(_diff_attn_kernel, lambda_init=lambda_init),
        grid=(b, s // bq),
        in_specs=[_resident(lam_p.shape),
                  pl.BlockSpec((1, d, bq), lambda bi, i: (bi, 0, i)),
                  pl.BlockSpec((1, s // 2, d), lambda bi, i: (bi, 0, 0)),
                  pl.BlockSpec((1, d // 2, s), lambda bi, i: (bi, 0, 0)),
                  _resident(sg.shape)],
        out_specs=pl.BlockSpec((1, d, bq), lambda bi, i: (bi, 0, i)),
        out_shape=jax.ShapeDtypeStruct((b, d, s), BF16),
        scratch_shapes=[pltpu.VMEM((n_half, hd2, bq), BF16),
                        pltpu.VMEM((n_half, 1, bq), F32),
                        pltpu.VMEM((n_half, hd2 + ONES_ROWS, bq), F32),
                        pltpu.VMEM((LOOKAHEAD + 1, 2 * bq, bq), F32)],
        compiler_params=pltpu.CompilerParams(
            dimension_semantics=("parallel", "arbitrary"),
            vmem_limit_bytes=V7X_VMEM_LIMIT_BYTES),
        name="diff_attn",
    )(lam_p, qt, k, vt, sg)


def _proj_ffn_kernel(x_ref, at_ref, wo_ref, g_ref, wg_ref, wu_ref, wd_ref, o_ref, hid_ref):
    x = x_ref[0] + lax.dot_general(at_ref[0], wo_ref[...], TN_DIMS,
                                   preferred_element_type=F32)
    h = (x * _rms_scale(x) * g_ref[...]).astype(BF16)
    n_chunks = hid_ref.shape[-1] // FFN_CHUNK
    for c in range(n_chunks):
        cols = slice(c * FFN_CHUNK, (c + 1) * FFN_CHUNK)
        gate = jnp.dot(h, wg_ref[:, cols], preferred_element_type=F32)
        up = jnp.dot(h, wu_ref[:, cols], preferred_element_type=F32)
        hid_ref[:, c * FFN_CHUNK:(c + 1) * FFN_CHUNK] = (
            gate * jax.nn.sigmoid(gate) * up).astype(BF16)
    o_ref[0] = x + jnp.dot(hid_ref[...], wd_ref[...], preferred_element_type=F32)


def _proj_ffn(x, at, wo, g, wgu, wd, layer):
    b, s, d = x.shape
    tm = ROW_TILE
    hidden = wd.shape[1]
    tok = pl.BlockSpec((1, tm, d), lambda i, j: (i, j, 0))
    feat = pl.BlockSpec((1, d, tm), lambda i, j: (i, 0, j))
    once = pl.Buffered(1)
    return pl.pallas_call(
        _proj_ffn_kernel,
        grid=(b, s // tm),
        in_specs=[tok, feat, _resident(wo.shape), _resident((1, d)),
                  pl.BlockSpec((None, d, hidden), lambda i, j: (layer, 0, 0), pipeline_mode=once),
                  pl.BlockSpec((None, d, hidden), lambda i, j: (layer, 0, 1), pipeline_mode=once),
                  pl.BlockSpec((None, hidden, d), lambda i, j: (layer, 0, 0), pipeline_mode=once)],
        out_specs=tok,
        out_shape=jax.ShapeDtypeStruct((b, s, d), F32),
        scratch_shapes=[pltpu.VMEM((tm, hidden), BF16)],
        compiler_params=pltpu.CompilerParams(
            dimension_semantics=("parallel", "parallel"),
            vmem_limit_bytes=V7X_VMEM_LIMIT_BYTES),
        name="proj_ffn",
    )(x, at, wo, g, wgu, wgu, wd)


def _qkv1_kernel(x_ref, ga_ref, gkv_ref, wq_ref, wkv_ref, qg_ref, kg_ref, cos_ref, sin_ref,
                 qt_ref, k_ref, vt_ref, wqt_ref, wkvt_ref, kv_ref, res_ref):
    d_model = x_ref.shape[-1]
    kv_dim = k_ref.shape[-1]

    @pl.when((pl.program_id(0) == 0) & (pl.program_id(1) == 0))
    def _():
        _transpose_pack_weight(wq_ref, wqt_ref)
        _transpose_pack_weight(wkv_ref, wkvt_ref)

    x = x_ref[0]
    xn = x * _rms_scale(x)
    h_a = (xn * ga_ref[...]).astype(BF16)
    h_kv = (xn * gkv_ref[...]).astype(BF16)
    cos = cos_ref[0]
    sin = sin_ref[0]

    n_chunks = d_model // FEATURE_CHUNK

    def project_q(c):
        rows = slice(c * FEATURE_CHUNK // 2, (c + 1) * FEATURE_CHUNK // 2)
        res_ref[c % 2] = lax.dot_general(_unpack_rows(wqt_ref[rows, :]), h_a, NT_DIMS,
                                         preferred_element_type=F32)

    kv_ref[...] = lax.dot_general(_unpack_rows(wkvt_ref[...]), h_kv, NT_DIMS,
                                  preferred_element_type=F32)
    project_q(0)
    kparts = [_norm_rope_t(kv_ref[g * HEAD_DIM:(g + 1) * HEAD_DIM, :], kg_ref[...], cos, sin)
              for g in range(kv_dim // HEAD_DIM)]
    k_ref[0] = _pack_rows(jnp.concatenate(kparts, axis=0).T.astype(BF16))
    vt_ref[0] = _pack_rows(kv_ref[kv_dim:, :].astype(BF16))

    for c in range(n_chunks):
        if c + 1 < n_chunks:
            project_q(c + 1)
        res = res_ref.at[c % 2]
        rows = slice(c * FEATURE_CHUNK, (c + 1) * FEATURE_CHUNK)
        parts = [_norm_rope_t(res[g * HEAD_DIM:(g + 1) * HEAD_DIM, :], qg_ref[...], cos, sin)
                 for g in range(FEATURE_CHUNK // HEAD_DIM)]
        qt_ref[0, rows, :] = jnp.concatenate(parts, axis=0).astype(BF16)


def _qkv1(x, ga, gkv, wq, wkv, qg, kg, cos_t, sin_t):
    b, s, d = x.shape
    tm = ROW_TILE
    kv_dim = wkv.shape[1] // 2
    tok = pl.BlockSpec((1, tm, d), lambda i, j: (i, j, 0))
    rope = pl.BlockSpec((1, HEAD_DIM // 2, tm), lambda i, j: (i, 0, j))
    return pl.pallas_call(
        _qkv1_kernel,
        grid=(b, s // tm),
        in_specs=[tok, _resident((1, d)), _resident((1, d)), _resident(wq.shape),
                  _resident(wkv.shape), _resident(qg.shape), _resident(kg.shape), rope, rope],
        out_specs=[pl.BlockSpec((1, d, tm), lambda i, j: (i, 0, j)),
                   pl.BlockSpec((1, tm // 2, kv_dim), lambda i, j: (i, j, 0)),
                   pl.BlockSpec((1, kv_dim // 2, tm), lambda i, j: (i, 0, j))],
        out_shape=[jax.ShapeDtypeStruct((b, d, s), BF16),
                   jax.ShapeDtypeStruct((b, s // 2, kv_dim), jnp.uint32),
                   jax.ShapeDtypeStruct((b, kv_dim // 2, s), jnp.uint32)],
        scratch_shapes=[pltpu.VMEM((wq.shape[1] // 2, d), jnp.uint32),
                        pltpu.VMEM((kv_dim, d), jnp.uint32),
                        pltpu.VMEM((2 * kv_dim, tm), F32),
                        pltpu.VMEM((2, FEATURE_CHUNK, tm), F32)],
        compiler_params=pltpu.CompilerParams(
            dimension_semantics=("arbitrary", "arbitrary"),
            vmem_limit_bytes=V7X_VMEM_LIMIT_BYTES),
        name="qkv1_proj",
    )(x, ga, gkv, wq, wkv, qg, kg, cos_t, sin_t)


def _swa_kernel(qt_ref, k_ref, vt_ref, sink_ref, ot_ref, bias_ref, s_ref):
    w = WINDOW
    s_len = qt_ref.shape[-1]
    kv_dim = k_ref.shape[-1]
    gw = SW_PACK * w

    n_kv = kv_dim // HEAD_DIM
    n_blocks = s_len // w

    kidx = lax.broadcasted_iota(jnp.int32, (2 * w, gw), 0)
    qidx = lax.broadcasted_iota(jnp.int32, (2 * w, gw), 1) & (w - 1)
    bias_ref[...] = jnp.where((kidx > qidx) & (kidx <= qidx + w), 0.0, NEG_INF).astype(F32)

    def window(n, first):
        nk = w if first else 2 * w
        k0 = 0 if first else pl.multiple_of((n - 1) * w, w)
        q0 = 0 if first else pl.multiple_of(n * w, w)
        return nk, k0, q0

    def heads(j, part):
        return [SW_GROUP * j + SW_PACK * part + u for u in range(SW_PACK)]

    def scores(n, j, part, first):
        nk, k0, q0 = window(n, first)
        k0_packed = k0 if first else pl.multiple_of((n - 1) * (w // 2), w // 2)
        kwin = _unpack_rows(k_ref[0, pl.ds(k0_packed, nk // 2), :])
        qcat = jnp.concatenate(
            [qt_ref[0, h * HEAD_DIM:(h + 1) * HEAD_DIM, pl.ds(q0, w)]
             for h in heads(j, part)], axis=1)
        pieces = []
        if j > 0:
            pieces.append(jnp.zeros((j * HEAD_DIM, gw), BF16))
        pieces.append(qcat)
        if j + 1 < n_kv:
            pieces.append(jnp.zeros(((n_kv - j - 1) * HEAD_DIM, gw), BF16))
        qz = jnp.concatenate(pieces, axis=0) if len(pieces) > 1 else qcat
        return jnp.dot(kwin, qz, preferred_element_type=F32)

    def finish(n, j, part, first, s):
        nk, k0, q0 = window(n, first)
        s = s + bias_ref[2 * w - nk:, :]
        sink = sink_ref[j:j + 1, part * gw:(part + 1) * gw]
        m = jnp.maximum(jnp.max(s, axis=0, keepdims=True), sink)
        e = jnp.exp2(s - m).astype(BF16)
        vwin = _unpack_rows(
            vt_ref[0, j * HEAD_DIM // 2:(j + 1) * HEAD_DIM // 2, pl.ds(k0, nk)])
        vext = jnp.concatenate([vwin, jnp.ones((ONES_ROWS, nk), BF16)], axis=0)
        o = jnp.dot(vext, e, preferred_element_type=F32)
        den = o[HEAD_DIM:HEAD_DIM + 1] + jnp.exp2(sink - m)
        o = o[:HEAD_DIM] * (1.0 / den)
        for u, h in enumerate(heads(j, part)):
            ot_ref[0, h * HEAD_DIM:(h + 1) * HEAD_DIM, pl.ds(q0, w)] = (
                o[:, u * w:(u + 1) * w].astype(BF16))

    def run(chains):
        n_slots = s_ref.shape[0]

        def park(t):
            nk = window(chains[t][0], chains[t][3])[0]
            s_ref[t % n_slots, :nk, :] = scores(*chains[t])

        for t in range(min(LOOKAHEAD, len(chains))):
            park(t)
        for t, ch in enumerate(chains):
            if t + LOOKAHEAD < len(chains):
                park(t + LOOKAHEAD)
            nk = window(ch[0], ch[3])[0]
            finish(*ch, s_ref[t % n_slots, :nk, :])

    parts = range(SW_GROUP // SW_PACK)
    run([(0, j, part, True) for j in range(n_kv) for part in parts])

    def body(t, carry):
        n0 = 1 + t * SWA_UNROLL
        run([(n0 + u, j, part, False)
             for u in range(SWA_UNROLL) for j in range(n_kv) for part in parts])
        return carry

    lax.fori_loop(0, (n_blocks - 1) // SWA_UNROLL, body, 0)


def _swa(qt, k, vt, sink_rows):
    b, d, s = qt.shape
    kv_dim = k.shape[-1]
    assert (s // WINDOW - 1) % SWA_UNROLL == 0
    return pl.pallas_call(
        _swa_kernel,
        grid=(b,),
        in_specs=[pl.BlockSpec((1, d, s), lambda i: (i, 0, 0)),
                  pl.BlockSpec((1, s // 2, kv_dim), lambda i: (i, 0, 0)),
                  pl.BlockSpec((1, kv_dim // 2, s), lambda i: (i, 0, 0)),
                  _resident(sink_rows.shape)],
        out_specs=pl.BlockSpec((1, d, s), lambda i: (i, 0, 0)),
        out_shape=jax.ShapeDtypeStruct((b, d, s), BF16),
        scratch_shapes=[pltpu.VMEM((2 * WINDOW, SW_PACK * WINDOW), F32),
                        pltpu.VMEM((LOOKAHEAD + 1, 2 * WINDOW, SW_PACK * WINDOW), F32)],
        compiler_params=pltpu.CompilerParams(
            dimension_semantics=("parallel",),
            vmem_limit_bytes=V7X_VMEM_LIMIT_BYTES),
        name="swa_attn",
    )(qt, k, vt, sink_rows)


def _lane_bcast(col, n):
    return jnp.broadcast_to(col.astype(F32).reshape(-1, 1), (col.size, n))


def kernel(x, positions, attn_norm, ffn_norm, w_gate_up, w_down, da_w_qkv, da_q_norm, da_k_norm,
           da_lambda, da_subln, da_w_o, kv_norm, w_kv, k_norm, sw_w_q, sw_q_norm, sw_sinks, sw_w_o):
    b, s, d = x.shape
    scale = 1.0 / math.sqrt(HEAD_DIM)

    inv = 1.0 / (ROPE_THETA ** (jnp.arange(0, HEAD_DIM, 2, dtype=F32) / HEAD_DIM))
    ang_t = positions.astype(F32)[:, None, :] * inv[None, :, None]
    cos_t = jnp.cos(ang_t)
    sin_t = jnp.sin(ang_t)

    lambda_init = 0.8 - 0.6 * math.exp(-0.3 * 0)
    n_layers, _, two_h = w_gate_up.shape
    qt, k, vt, wgu, wd, wo0, wo1 = _qkv0(
        x, attn_norm[0].reshape(1, d), da_w_qkv[0],
        _lane_bcast(da_q_norm[0] * (scale * LOG2_E), ROW_TILE),
        _lane_bcast(da_k_norm[0], ROW_TILE), cos_t, sin_t,
        w_gate_up.reshape(n_layers * d, two_h), w_down.reshape(n_layers * (two_h // 2), d),
        da_w_o[0], sw_w_o[0])
    wgu = wgu.reshape(n_layers, d, two_h)
    wd = wd.reshape(n_layers, two_h // 2, d)
    at = _diff_attn(da_lambda[0].astype(F32), qt, k, vt,
                    _lane_bcast(da_subln[0] * (1.0 - lambda_init), ATTN_BLOCK), lambda_init)
    x = _proj_ffn(x, at, wo0, ffn_norm[0].reshape(1, d), wgu, wd, 0)

    qt, k, vt = _qkv1(x, attn_norm[1].reshape(1, d), kv_norm.reshape(1, d),
                      sw_w_q[0], w_kv,
                      _lane_bcast(sw_q_norm[0] * (scale * LOG2_E), ROW_TILE),
                      _lane_bcast(k_norm, ROW_TILE), cos_t, sin_t)
    sink_rows = jnp.repeat((sw_sinks[0].astype(F32) * LOG2_E).reshape(SW_KV_HEADS, SW_GROUP),
                           WINDOW, axis=1)
    at = _swa(qt, k, vt, sink_rows)
    x = _proj_ffn(x, at, wo1, ffn_norm[1].reshape(1, d), wgu, wd, 1)
    return x
```

```python
import functools
import math

import jax
import jax.numpy as jnp
from jax import lax
from jax.experimental import pallas as pl
from jax.experimental.pallas import tpu as pltpu

HEAD_DIM = 64
ROPE_THETA = 10000.0
NORM_EPS = 1e-6
NEG_INF = -1e30
LOG2_E = math.log2(math.e)
WINDOW = 128
SW_KV_HEADS = 4
SW_GROUP = 4
SW_PACK = 2

F32 = jnp.float32
BF16 = jnp.bfloat16

V7X_VMEM_LIMIT_BYTES = 56 * 1024 * 1024

ROW_TILE = 1024
FEATURE_CHUNK = 512
FFN_CHUNK = 256
ATTN_BLOCK = 256
ONES_ROWS = 16
LOOKAHEAD = 6
SWA_UNROLL = 5

NT_DIMS = (((1,), (1,)), ((), ()))
TN_DIMS = (((0,), (0,)), ((), ()))


def _resident(shape):
    nd = len(shape)
    return pl.BlockSpec(shape, lambda *_: (0,) * nd, pipeline_mode=pl.Buffered(1))


def _rms_scale(x):
    return lax.rsqrt(jnp.mean(x * x, axis=-1, keepdims=True) + NORM_EPS)


def _pack_rows(x):
    return pltpu.bitcast(x, jnp.uint32)


def _unpack_rows(x):
    return pltpu.bitcast(x, BF16)


def _transpose_pack_weight(w_ref, wt_ref):
    n = w_ref.shape[1]
    for c in range(n // FEATURE_CHUNK):
        cols = slice(c * FEATURE_CHUNK, (c + 1) * FEATURE_CHUNK)
        rows = slice(c * FEATURE_CHUNK // 2, (c + 1) * FEATURE_CHUNK // 2)
        wt_ref[rows, :] = _pack_rows(w_ref[:, cols].T.astype(BF16))


def _norm_rope_t(t, gain, cos, sin):
    r = lax.rsqrt(jnp.mean(t * t, axis=0, keepdims=True) + NORM_EPS)
    tn = t * r * gain
    x1 = tn[: HEAD_DIM // 2]
    x2 = tn[HEAD_DIM // 2:]
    return jnp.concatenate([x1 * cos - x2 * sin, x2 * cos + x1 * sin], axis=0)


def _qkv0_kernel(x_ref, g_ref, w_ref, qg_ref, kg_ref, cos_ref, sin_ref,
                 wgu_ref, wd_ref, wo0_ref, wo1_ref,
                 qt_ref, k_ref, vt_ref, wgu_bf_ref, wd_bf_ref, wo0_bf_ref, wo1_bf_ref,
                 wt_ref, res_ref):
    d_model = x_ref.shape[-1]

    @pl.when((pl.program_id(0) == 0) & (pl.program_id(1) == 0))
    def _():
        _transpose_pack_weight(w_ref, wt_ref)

    @pl.when(pl.program_id(1) == 0)
    def _():
        wd_bf_ref[...] = wd_ref[...].astype(BF16)

    wgu_bf_ref[...] = wgu_ref[...].astype(BF16)
    wo0_bf_ref[...] = wo0_ref[...].astype(BF16)
    wo1_bf_ref[...] = wo1_ref[...].astype(BF16)

    x = x_ref[0]
    h = (x * _rms_scale(x) * g_ref[...]).astype(BF16)
    cos = cos_ref[0]
    sin = sin_ref[0]
    n_chunks = d_model // FEATURE_CHUNK

    def project(c):
        rows = slice(c * FEATURE_CHUNK // 2, (c + 1) * FEATURE_CHUNK // 2)
        w_rows = _unpack_rows(wt_ref[rows, :])
        res_ref[c % 2] = lax.dot_general(w_rows, h, NT_DIMS, preferred_element_type=F32)

    project(0)
    for c in range(3 * n_chunks):
        if c + 1 < 3 * n_chunks:
            project(c + 1)
        res = res_ref.at[c % 2]
        kind, cc = divmod(c, n_chunks)
        out_rows = slice(cc * FEATURE_CHUNK, (cc + 1) * FEATURE_CHUNK)
        half_rows = slice(cc * FEATURE_CHUNK // 2, (cc + 1) * FEATURE_CHUNK // 2)
        if kind == 2:
            vt_ref[0, half_rows, :] = _pack_rows(res[...].astype(BF16))
            continue
        gain_ref = qg_ref if kind == 0 else kg_ref
        parts = []
        for g in range(FEATURE_CHUNK // HEAD_DIM):
            half = g % 2
            gain = gain_ref[half * HEAD_DIM:(half + 1) * HEAD_DIM, :]
            parts.append(_norm_rope_t(res[g * HEAD_DIM:(g + 1) * HEAD_DIM, :], gain, cos, sin))
        out = jnp.concatenate(parts, axis=0)
        if kind == 0:
            qt_ref[0, out_rows, :] = out.astype(BF16)
        else:
            k_ref[0, :, out_rows] = _pack_rows(out.T.astype(BF16))


def _qkv0(x, g, w, qg, kg, cos_t, sin_t, wgu, wd, wo0, wo1):
    b, s, d = x.shape
    tm = ROW_TILE
    nt = s // tm
    grid = (b, nt)

    def per_step(a):
        return pl.BlockSpec((a.shape[0] // (b * nt), a.shape[1]), lambda i, j: (i * nt + j, 0))

    def per_batch(a):
        return pl.BlockSpec((a.shape[0] // b, a.shape[1]), lambda i, j: (i, 0))

    def as_bf16(a):
        return jax.ShapeDtypeStruct(a.shape, BF16)
    feat = pl.BlockSpec((1, d, tm), lambda i, j: (i, 0, j))
    feat_packed = pl.BlockSpec((1, d // 2, tm), lambda i, j: (i, 0, j))
    tok = pl.BlockSpec((1, tm, d), lambda i, j: (i, j, 0))
    tok_packed = pl.BlockSpec((1, tm // 2, d), lambda i, j: (i, j, 0))
    rope = pl.BlockSpec((1, HEAD_DIM // 2, tm), lambda i, j: (i, 0, j))
    return pl.pallas_call(
        _qkv0_kernel,
        grid=grid,
        in_specs=[tok, _resident((1, d)), _resident(w.shape), _resident(qg.shape),
                  _resident(kg.shape), rope, rope,
                  per_step(wgu), per_batch(wd), per_step(wo0), per_step(wo1)],
        out_specs=[feat, tok_packed, feat_packed,
                   per_step(wgu), per_batch(wd), per_step(wo0), per_step(wo1)],
        out_shape=[jax.ShapeDtypeStruct((b, d, s), BF16),
                   jax.ShapeDtypeStruct((b, s // 2, d), jnp.uint32),
                   jax.ShapeDtypeStruct((b, d // 2, s), jnp.uint32),
                   as_bf16(wgu), as_bf16(wd), as_bf16(wo0), as_bf16(wo1)],
        scratch_shapes=[pltpu.VMEM((w.shape[1] // 2, d), jnp.uint32),
                        pltpu.VMEM((2, FEATURE_CHUNK, tm), F32)],
        compiler_params=pltpu.CompilerParams(
            dimension_semantics=("arbitrary", "arbitrary"),
            vmem_limit_bytes=V7X_VMEM_LIMIT_BYTES),
        name="qkv0_proj",
    )(x, g, w, qg, kg, cos_t, sin_t, wgu, wd, wo0, wo1)


def _diff_attn_kernel(lam_ref, qt_ref, k_ref, vt_ref, sg_ref, ot_ref,
                      qz_ref, m_ref, acc_ref, s_ref, *, lambda_init):
    bq = qt_ref.shape[-1]
    hd2 = 2 * HEAD_DIM
    n_heads = qt_ref.shape[1] // hd2
    i = pl.program_id(1)

    lp = lam_ref[...]
    lam = (jnp.exp(jnp.sum(lp[0:1] * lp[1:2], axis=-1, keepdims=True))
           - jnp.exp(jnp.sum(lp[2:3] * lp[3:4], axis=-1, keepdims=True)) + lambda_init)

    zero = jnp.zeros((HEAD_DIM, bq), BF16)
    for h in range(n_heads):
        q0 = qt_ref[0, h * hd2:h * hd2 + HEAD_DIM, :]
        q1 = qt_ref[0, h * hd2 + HEAD_DIM:(h + 1) * hd2, :]
        qz_ref[2 * h] = jnp.concatenate([q0, zero], axis=0)
        qz_ref[2 * h + 1] = jnp.concatenate([zero, q1], axis=0)

    def block(k0, bk, first):
        ones = jnp.ones((ONES_ROWS, bk), BF16)
        k0_packed = pl.multiple_of(lax.shift_right_logical(k0, 1), bk // 2)
        if first:
            kidx = lax.broadcasted_iota(jnp.int32, (bk, bq), 0)
            qidx = lax.broadcasted_iota(jnp.int32, (bk, bq), 1)
            keep = kidx <= qidx + (bk - bq)

        def scores(hc):
            h = hc // 2
            kblk = _unpack_rows(k_ref[0, pl.ds(k0_packed, bk // 2), h * hd2:(h + 1) * hd2])
            return jnp.dot(kblk, qz_ref[hc], preferred_element_type=F32)

        n_slots = s_ref.shape[0]

        def park(hc):
            s_ref[hc % n_slots, :bk, :] = scores(hc)

        for hc in range(LOOKAHEAD):
            park(hc)
        for hc in range(2 * n_heads):
            h = hc // 2
            if hc + LOOKAHEAD < 2 * n_heads:
                park(hc + LOOKAHEAD)
            s = s_ref[hc % n_slots, :bk, :]
            vblk = _unpack_rows(vt_ref[0, h * HEAD_DIM:(h + 1) * HEAD_DIM, pl.ds(k0, bk)])
            vext = jnp.concatenate([vblk, ones], axis=0)
            if first:
                s = jnp.where(keep, s, NEG_INF)
                m_new = jnp.max(s, axis=0, keepdims=True)
                p = jnp.exp2(s - m_new)
                acc_ref[hc] = jnp.dot(vext, p.astype(BF16), preferred_element_type=F32)
            else:
                m_old = m_ref[hc]
                m_new = jnp.maximum(m_old, jnp.max(s, axis=0, keepdims=True))
                alpha = jnp.exp2(m_old - m_new)
                p = jnp.exp2(s - m_new)
                acc_ref[hc] = alpha * acc_ref[hc] + jnp.dot(
                    vext, p.astype(BF16), preferred_element_type=F32)
            m_ref[hc] = m_new

    @pl.when((i & 1) == 0)
    def _():
        block(pl.multiple_of(i * bq, bq), bq, True)

    @pl.when((i & 1) == 1)
    def _():
        block(pl.multiple_of((i - 1) * bq, 2 * bq), 2 * bq, True)

    def body(j, carry):
        block(pl.multiple_of(j * 2 * bq, 2 * bq), 2 * bq, False)
        return carry

    lax.fori_loop(0, lax.shift_right_logical(i, 1), body, 0)

    for h in range(n_heads):
        a0 = acc_ref[2 * h]
        a1 = acc_ref[2 * h + 1]
        inv0 = 1.0 / a0[hd2:hd2 + 1]
        inv1 = lam / a1[hd2:hd2 + 1]
        o = a0[:hd2] * inv0 - a1[:hd2] * inv1
        r = lax.rsqrt(jnp.mean(o * o, axis=0, keepdims=True) + NORM_EPS)
        ot_ref[0, h * hd2:(h + 1) * hd2, :] = (
            o * r * sg_ref[...]).astype(BF16)


def _diff_attn(lam_p, qt, k, vt, sg, lambda_init):
    b, d, s = qt.shape
    hd2 = 2 * HEAD_DIM
    bq = ATTN_BLOCK
    n_half = d // HEAD_DIM
    return pl.pallas_call(
        functools.partial(_diff_attn_kernel, lambda_init=lambda_init),
        grid=(b, s // bq),
        in_specs=[_resident(lam_p.shape),
                  pl.BlockSpec((1, d, bq), lambda bi, i: (bi, 0, i)),
                  pl.BlockSpec((1, s // 2, d), lambda bi, i: (bi, 0, 0)),
                  pl.BlockSpec((1, d // 2, s), lambda bi, i: (bi, 0, 0)),
                  _resident(sg.shape)],
        out_specs=pl.BlockSpec((1, d, bq), lambda bi, i: (bi, 0, i)),
        out_shape=jax.ShapeDtypeStruct((b, d, s), BF16),
        scratch_shapes=[pltpu.VMEM((n_half, hd2, bq), BF16),
                        pltpu.VMEM((n_half, 1, bq), F32),
                        pltpu.VMEM((n_half, hd2 + ONES_ROWS, bq), F32),
                        pltpu.VMEM((LOOKAHEAD + 1, 2 * bq, bq), F32)],
        compiler_params=pltpu.CompilerParams(
            dimension_semantics=("parallel", "arbitrary"),
            vmem_limit_bytes=V7X_VMEM_LIMIT_BYTES),
        name="diff_attn",
    )(lam_p, qt, k, vt, sg)


def _proj_ffn_kernel(x_ref, at_ref, wo_ref, g_ref, wg_ref, wu_ref, wd_ref, o_ref, hid_ref):
    x = x_ref[0] + lax.dot_general(at_ref[0], wo_ref[...], TN_DIMS,
                                   preferred_element_type=F32)
    h = (x * _rms_scale(x) * g_ref[...]).astype(BF16)
    n_chunks = hid_ref.shape[-1] // FFN_CHUNK
    for c in range(n_chunks):
        cols = slice(c * FFN_CHUNK, (c + 1) * FFN_CHUNK)
        gate = jnp.dot(h, wg_ref[:, cols], preferred_element_type=F32)
        up = jnp.dot(h, wu_ref[:, cols], preferred_element_type=F32)
        hid_ref[:, c * FFN_CHUNK:(c + 1) * FFN_CHUNK] = (
            gate * jax.nn.sigmoid(gate) * up).astype(BF16)
    o_ref[0] = x + jnp.dot(hid_ref[...], wd_ref[...], preferred_element_type=F32)


def _proj_ffn(x, at, wo, g, wgu, wd, layer):
    b, s, d = x.shape
    tm = ROW_TILE
    hidden = wd.shape[1]
    tok = pl.BlockSpec((1, tm, d), lambda i, j: (i, j, 0))
    feat = pl.BlockSpec((1, d, tm), lambda i, j: (i, 0, j))
    once = pl.Buffered(1)
    return pl.pallas_call(
        _proj_ffn_kernel,
        grid=(b, s // tm),
        in_specs=[tok, feat, _resident(wo.shape), _resident((1, d)),
                  pl.BlockSpec((None, d, hidden), lambda i, j: (layer, 0, 0), pipeline_mode=once),
                  pl.BlockSpec((None, d, hidden), lambda i, j: (layer, 0, 1), pipeline_mode=once),
                  pl.BlockSpec((None, hidden, d), lambda i, j: (layer, 0, 0), pipeline_mode=once)],
        out_specs=tok,
        out_shape=jax.ShapeDtypeStruct((b, s, d), F32),
        scratch_shapes=[pltpu.VMEM((tm, hidden), BF16)],
        compiler_params=pltpu.CompilerParams(
            dimension_semantics=("parallel", "parallel"),
            vmem_limit_bytes=V7X_VMEM_LIMIT_BYTES),
        name="proj_ffn",
    )(x, at, wo, g, wgu, wgu, wd)


def _qkv1_kernel(x_ref, ga_ref, gkv_ref, wq_ref, wkv_ref, qg_ref, kg_ref, cos_ref, sin_ref,
                 qt_ref, k_ref, vt_ref, wqt_ref, wkvt_ref, kv_ref, res_ref):
    d_model = x_ref.shape[-1]
    kv_dim = k_ref.shape[-1]

    @pl.when((pl.program_id(0) == 0) & (pl.program_id(1) == 0))
    def _():
        _transpose_pack_weight(wq_ref, wqt_ref)
        _transpose_pack_weight(wkv_ref, wkvt_ref)

    x = x_ref[0]
    xn = x * _rms_scale(x)
    h_a = (xn * ga_ref[...]).astype(BF16)
    h_kv = (xn * gkv_ref[...]).astype(BF16)
    cos = cos_ref[0]
    sin = sin_ref[0]

    n_chunks = d_model // FEATURE_CHUNK

    def project_q(c):
        rows = slice(c * FEATURE_CHUNK // 2, (c + 1) * FEATURE_CHUNK // 2)
        res_ref[c % 2] = lax.dot_general(_unpack_rows(wqt_ref[rows, :]), h_a, NT_DIMS,
                                         preferred_element_type=F32)

    kv_ref[...] = lax.dot_general(_unpack_rows(wkvt_ref[...]), h_kv, NT_DIMS,
                                  preferred_element_type=F32)
    project_q(0)
    kparts = [_norm_rope_t(kv_ref[g * HEAD_DIM:(g + 1) * HEAD_DIM, :], kg_ref[...], cos, sin)
              for g in range(kv_dim // HEAD_DIM)]
    k_ref[0] = _pack_rows(jnp.concatenate(kparts, axis=0).T.astype(BF16))
    vt_ref[0] = _pack_rows(kv_ref[kv_dim:, :].astype(BF16))

    for c in range(n_chunks):
        if c + 1 < n_chunks:
            project_q(c + 1)
        res = res_ref.at[c % 2]
        rows = slice(c * FEATURE_CHUNK, (c + 1) * FEATURE_CHUNK)
        parts = [_norm_rope_t(res[g * HEAD_DIM:(g + 1) * HEAD_DIM, :], qg_ref[...], cos, sin)
                 for g in range(FEATURE_CHUNK // HEAD_DIM)]
        qt_ref[0, rows, :] = jnp.concatenate(parts, axis=0).astype(BF16)


def _qkv1(x, ga, gkv, wq, wkv, qg, kg, cos_t, sin_t):
    b, s, d = x.shape
    tm = ROW_TILE
    kv_dim = wkv.shape[1] // 2
    tok = pl.BlockSpec((1, tm, d), lambda i, j: (i, j, 0))
    rope = pl.BlockSpec((1, HEAD_DIM // 2, tm), lambda i, j: (i, 0, j))
    return pl.pallas_call(
        _qkv1_kernel,
        grid=(b, s // tm),
        in_specs=[tok, _resident((1, d)), _resident((1, d)), _resident(wq.shape),
                  _resident(wkv.shape), _resident(qg.shape), _resident(kg.shape), rope, rope],
        out_specs=[pl.BlockSpec((1, d, tm), lambda i, j: (i, 0, j)),
                   pl.BlockSpec((1, tm // 2, kv_dim), lambda i, j: (i, j, 0)),
                   pl.BlockSpec((1, kv_dim // 2, tm), lambda i, j: (i, 0, j))],
        out_shape=[jax.ShapeDtypeStruct((b, d, s), BF16),
                   jax.ShapeDtypeStruct((b, s // 2, kv_dim), jnp.uint32),
                   jax.ShapeDtypeStruct((b, kv_dim // 2, s), jnp.uint32)],
        scratch_shapes=[pltpu.VMEM((wq.shape[1] // 2, d), jnp.uint32),
                        pltpu.VMEM((kv_dim, d), jnp.uint32),
                        pltpu.VMEM((2 * kv_dim, tm), F32),
                        pltpu.VMEM((2, FEATURE_CHUNK, tm), F32)],
        compiler_params=pltpu.CompilerParams(
            dimension_semantics=("arbitrary", "arbitrary"),
            vmem_limit_bytes=V7X_VMEM_LIMIT_BYTES),
        name="qkv1_proj",
    )(x, ga, gkv, wq, wkv, qg, kg, cos_t, sin_t)


def _swa_kernel(qt_ref, k_ref, vt_ref, sink_ref, ot_ref, bias_ref, s_ref):
    w = WINDOW
    s_len = qt_ref.shape[-1]
    kv_dim = k_ref.shape[-1]
    gw = SW_PACK * w

    n_kv = kv_dim // HEAD_DIM
    n_blocks = s_len // w

    kidx = lax.broadcasted_iota(jnp.int32, (2 * w, gw), 0)
    qidx = lax.broadcasted_iota(jnp.int32, (2 * w, gw), 1) & (w - 1)
    bias_ref[...] = jnp.where((kidx > qidx) & (kidx <= qidx + w), 0.0, NEG_INF).astype(F32)

    def window(n, first):
        nk = w if first else 2 * w
        k0 = 0 if first else pl.multiple_of((n - 1) * w, w)
        q0 = 0 if first else pl.multiple_of(n * w, w)
        return nk, k0, q0

    def heads(j, part):
        return [SW_GROUP * j + SW_PACK * part + u for u in range(SW_PACK)]

    def scores(n, j, part, first):
        nk, k0, q0 = window(n, first)
        k0_packed = k0 if first else pl.multiple_of((n - 1) * (w // 2), w // 2)
        kwin = _unpack_rows(k_ref[0, pl.ds(k0_packed, nk // 2), :])
        qcat = jnp.concatenate(
            [qt_ref[0, h * HEAD_DIM:(h + 1) * HEAD_DIM, pl.ds(q0, w)]
             for h in heads(j, part)], axis=1)
        pieces = []
        if j > 0:
            pieces.append(jnp.zeros((j * HEAD_DIM, gw), BF16))
        pieces.append(qcat)
        if j + 1 < n_kv:
            pieces.append(jnp.zeros(((n_kv - j - 1) * HEAD_DIM, gw), BF16))
        qz = jnp.concatenate(pieces, axis=0) if len(pieces) > 1 else qcat
        return jnp.dot(kwin, qz, preferred_element_type=F32)

    def finish(n, j, part, first, s):
        nk, k0, q0 = window(n, first)
        s = s + bias_ref[2 * w - nk:, :]
        sink = sink_ref[j:j + 1, part * gw:(part + 1) * gw]
        m = jnp.maximum(jnp.max(s, axis=0, keepdims=True), sink)
        e = jnp.exp2(s - m).astype(BF16)
        vwin = _unpack_rows(
            vt_ref[0, j * HEAD_DIM // 2:(j + 1) * HEAD_DIM // 2, pl.ds(k0, nk)])
        vext = jnp.concatenate([vwin, jnp.ones((ONES_ROWS, nk), BF16)], axis=0)
        o = jnp.dot(vext, e, preferred_element_type=F32)
        den = o[HEAD_DIM:HEAD_DIM + 1] + jnp.exp2(sink - m)
        o = o[:HEAD_DIM] * (1.0 / den)
        for u, h in enumerate(heads(j, part)):
            ot_ref[0, h * HEAD_DIM:(h + 1) * HEAD_DIM, pl.ds(q0, w)] = (
                o[:, u * w:(u + 1) * w].astype(BF16))

    def run(chains):
        n_slots = s_ref.shape[0]

        def park(t):
            nk = window(chains[t][0], chains[t][3])[0]
            s_ref[t % n_slots, :nk, :] = scores(*chains[t])

        for t in range(min(LOOKAHEAD, len(chains))):
            park(t)
        for t, ch in enumerate(chains):
            if t + LOOKAHEAD < len(chains):
                park(t + LOOKAHEAD)
            nk = window(ch[0], ch[3])[0]
            finish(*ch, s_ref[t % n_slots, :nk, :])

    parts = range(SW_GROUP // SW_PACK)
    run([(0, j, part, True) for j in range(n_kv) for part in parts])

    def body(t, carry):
        n0 = 1 + t * SWA_UNROLL
        run([(n0 + u, j, part, False)
             for u in range(SWA_UNROLL) for j in range(n_kv) for part in parts])
        return carry

    lax.fori_loop(0, (n_blocks - 1) // SWA_UNROLL, body, 0)


def _swa(qt, k, vt, sink_rows):
    b, d, s = qt.shape
    kv_dim = k.shape[-1]
    assert (s // WINDOW - 1) % SWA_UNROLL == 0
    return pl.pallas_call(
        _swa_kernel,
        grid=(b,),
        in_specs=[pl.BlockSpec((1, d, s), lambda i: (i, 0, 0)),
                  pl.BlockSpec((1, s // 2, kv_dim), lambda i: (i, 0, 0)),
                  pl.BlockSpec((1, kv_dim // 2, s), lambda i: (i, 0, 0)),
                  _resident(sink_rows.shape)],
        out_specs=pl.BlockSpec((1, d, s), lambda i: (i, 0, 0)),
        out_shape=jax.ShapeDtypeStruct((b, d, s), BF16),
        scratch_shapes=[pltpu.VMEM((2 * WINDOW, SW_PACK * WINDOW), F32),
                        pltpu.VMEM((LOOKAHEAD + 1, 2 * WINDOW, SW_PACK * WINDOW), F32)],
        compiler_params=pltpu.CompilerParams(
            dimension_semantics=("parallel",),
            vmem_limit_bytes=V7X_VMEM_LIMIT_BYTES),
        name="swa_attn",
    )(qt, k, vt, sink_rows)


def _lane_bcast(col, n):
    return jnp.broadcast_to(col.astype(F32).reshape(-1, 1), (col.size, n))


def kernel(x, positions, attn_norm, ffn_norm, w_gate_up, w_down, da_w_qkv, da_q_norm, da_k_norm,
           da_lambda, da_subln, da_w_o, kv_norm, w_kv, k_norm, sw_w_q, sw_q_norm, sw_sinks, sw_w_o):
    b, s, d = x.shape
    scale = 1.0 / math.sqrt(HEAD_DIM)

    inv = 1.0 / (ROPE_THETA ** (jnp.arange(0, HEAD_DIM, 2, dtype=F32) / HEAD_DIM))
    ang_t = positions.astype(F32)[:, None, :] * inv[None, :, None]
    cos_t = jnp.cos(ang_t)
    sin_t = jnp.sin(ang_t)

    lambda_init = 0.8 - 0.6 * math.exp(-0.3 * 0)
    n_layers, _, two_h = w_gate_up.shape
    qt, k, vt, wgu, wd, wo0, wo1 = _qkv0(
        x, attn_norm[0].reshape(1, d), da_w_qkv[0],
        _lane_bcast(da_q_norm[0] * (scale * LOG2_E), ROW_TILE),
        _lane_bcast(da_k_norm[0], ROW_TILE), cos_t, sin_t,
        w_gate_up.reshape(n_layers * d, two_h), w_down.reshape(n_layers * (two_h // 2), d),
        da_w_o[0], sw_w_o[0])
    wgu = wgu.reshape(n_layers, d, two_h)
    wd = wd.reshape(n_layers, two_h // 2, d)
    at = _diff_attn(da_lambda[0].astype(F32), qt, k, vt,
                    _lane_bcast(da_subln[0] * (1.0 - lambda_init), ATTN_BLOCK), lambda_init)
    x = _proj_ffn(x, at, wo0, ffn_norm[0].reshape(1, d), wgu, wd, 0)

    qt, k, vt = _qkv1(x, attn_norm[1].reshape(1, d), kv_norm.reshape(1, d),
                      sw_w_q[0], w_kv,
                      _lane_bcast(sw_q_norm[0] * (scale * LOG2_E), ROW_TILE),
                      _lane_bcast(k_norm, ROW_TILE), cos_t, sin_t)
    sink_rows = jnp.repeat((sw_sinks[0].astype(F32) * LOG2_E).reshape(SW_KV_HEADS, SW_GROUP),
                           WINDOW, axis=1)
    at = _swa(qt, k, vt, sink_rows)
    x = _proj_ffn(x, at, wo1, ffn_norm[1].reshape(1, d), wgu, wd, 1)
    return x
```

```python
import functools
import math

import jax
import jax.numpy as jnp
from jax import lax
from jax.experimental import pallas as pl
from jax.experimental.pallas import tpu as pltpu

HEAD_DIM = 64
ROPE_THETA = 10000.0
NORM_EPS = 1e-6
NEG_INF = -1e30
LOG2_E = math.log2(math.e)
WINDOW = 128
SW_KV_HEADS = 4
SW_GROUP = 4
SW_PACK = 2

F32 = jnp.float32
BF16 = jnp.bfloat16

V7X_VMEM_LIMIT_BYTES = 56 * 1024 * 1024

ROW_TILE = 1024
FEATURE_CHUNK = 512
FFN_CHUNK = 256
ATTN_BLOCK = 256
ONES_ROWS = 16
LOOKAHEAD = 6
SWA_UNROLL = 5

NT_DIMS = (((1,), (1,)), ((), ()))
TN_DIMS = (((0,), (0,)), ((), ()))


def _resident(shape):
    nd = len(shape)
    return pl.BlockSpec(shape, lambda *_: (0,) * nd, pipeline_mode=pl.Buffered(1))


def _rms_scale(x):
    return lax.rsqrt(jnp.mean(x * x, axis=-1, keepdims=True) + NORM_EPS)


def _pack_rows(x):
    return pltpu.bitcast(x, jnp.uint32)


def _unpack_rows(x):
    return pltpu.bitcast(x, BF16)


def _transpose_pack_weight(w_ref, wt_ref):
    n = w_ref.shape[1]
    for c in range(n // FEATURE_CHUNK):
        cols = slice(c * FEATURE_CHUNK, (c + 1) * FEATURE_CHUNK)
        rows = slice(c * FEATURE_CHUNK // 2, (c + 1) * FEATURE_CHUNK // 2)
        wt_ref[rows, :] = _pack_rows(w_ref[:, cols].T.astype(BF16))


def _norm_rope_t(t, gain, cos, sin):
    r = lax.rsqrt(jnp.mean(t * t, axis=0, keepdims=True) + NORM_EPS)
    tn = t * r * gain
    x1 = tn[: HEAD_DIM // 2]
    x2 = tn[HEAD_DIM // 2:]
    return jnp.concatenate([x1 * cos - x2 * sin, x2 * cos + x1 * sin], axis=0)


def _qkv0_kernel(x_ref, g_ref, w_ref, qg_ref, kg_ref, cos_ref, sin_ref,
                 wgu_ref, wd_ref, wo0_ref, wo1_ref,
                 qt_ref, k_ref, vt_ref, wgu_bf_ref, wd_bf_ref, wo0_bf_ref, wo1_bf_ref,
                 wt_ref, res_ref):
    d_model = x_ref.shape[-1]

    @pl.when((pl.program_id(0) == 0) & (pl.program_id(1) == 0))
    def _():
        _transpose_pack_weight(w_ref, wt_ref)

    @pl.when(pl.program_id(1) == 0)
    def _():
        wd_bf_ref[...] = wd_ref[...].astype(BF16)

    wgu_bf_ref[...] = wgu_ref[...].astype(BF16)
    wo0_bf_ref[...] = wo0_ref[...].astype(BF16)
    wo1_bf_ref[...] = wo1_ref[...].astype(BF16)

    x = x_ref[0]
    h = (x * _rms_scale(x) * g_ref[...]).astype(BF16)
    cos = cos_ref[0]
    sin = sin_ref[0]
    n_chunks = d_model // FEATURE_CHUNK

    def project(c):
        rows = slice(c * FEATURE_CHUNK // 2, (c + 1) * FEATURE_CHUNK // 2)
        w_rows = _unpack_rows(wt_ref[rows, :])
        res_ref[c % 2] = lax.dot_general(w_rows, h, NT_DIMS, preferred_element_type=F32)

    project(0)
    for c in range(3 * n_chunks):
        if c + 1 < 3 * n_chunks:
            project(c + 1)
        res = res_ref.at[c % 2]
        kind, cc = divmod(c, n_chunks)
        out_rows = slice(cc * FEATURE_CHUNK, (cc + 1) * FEATURE_CHUNK)
        half_rows = slice(cc * FEATURE_CHUNK // 2, (cc + 1) * FEATURE_CHUNK // 2)
        if kind == 2:
            vt_ref[0, half_rows, :] = _pack_rows(res[...].astype(BF16))
            continue
        gain_ref = qg_ref if kind == 0 else kg_ref
        parts = []
        for g in range(FEATURE_CHUNK // HEAD_DIM):
            half = g % 2
            gain = gain_ref[half * HEAD_DIM:(half + 1) * HEAD_DIM, :]
            parts.append(_norm_rope_t(res[g * HEAD_DIM:(g + 1) * HEAD_DIM, :], gain, cos, sin))
        out = jnp.concatenate(parts, axis=0)
        if kind == 0:
            qt_ref[0, out_rows, :] = out.astype(BF16)
        else:
            k_ref[0, :, out_rows] = _pack_rows(out.T.astype(BF16))


def _qkv0(x, g, w, qg, kg, cos_t, sin_t, wgu, wd, wo0, wo1):
    b, s, d = x.shape
    tm = ROW_TILE
    nt = s // tm
    grid = (b, nt)

    def per_step(a):
        return pl.BlockSpec((a.shape[0] // (b * nt), a.shape[1]), lambda i, j: (i * nt + j, 0))

    def per_batch(a):
        return pl.BlockSpec((a.shape[0] // b, a.shape[1]), lambda i, j: (i, 0))

    def as_bf16(a):
        return jax.ShapeDtypeStruct(a.shape, BF16)
    feat = pl.BlockSpec((1, d, tm), lambda i, j: (i, 0, j))
    feat_packed = pl.BlockSpec((1, d // 2, tm), lambda i, j: (i, 0, j))
    tok = pl.BlockSpec((1, tm, d), lambda i, j: (i, j, 0))
    tok_packed = pl.BlockSpec((1, tm // 2, d), lambda i, j: (i, j, 0))
    rope = pl.BlockSpec((1, HEAD_DIM // 2, tm), lambda i, j: (i, 0, j))
    return pl.pallas_call(
        _qkv0_kernel,
        grid=grid,
        in_specs=[tok, _resident((1, d)), _resident(w.shape), _resident(qg.shape),
                  _resident(kg.shape), rope, rope,
                  per_step(wgu), per_batch(wd), per_step(wo0), per_step(wo1)],
        out_specs=[feat, tok_packed, feat_packed,
                   per_step(wgu), per_batch(wd), per_step(wo0), per_step(wo1)],
        out_shape=[jax.ShapeDtypeStruct((b, d, s), BF16),
                   jax.ShapeDtypeStruct((b, s // 2, d), jnp.uint32),
                   jax.ShapeDtypeStruct((b, d // 2, s), jnp.uint32),
                   as_bf16(wgu), as_bf16(wd), as_bf16(wo0), as_bf16(wo1)],
        scratch_shapes=[pltpu.VMEM((w.shape[1] // 2, d), jnp.uint32),
                        pltpu.VMEM((2, FEATURE_CHUNK, tm), F32)],
        compiler_params=pltpu.CompilerParams(
            dimension_semantics=("arbitrary", "arbitrary"),
            vmem_limit_bytes=V7X_VMEM_LIMIT_BYTES),
        name="qkv0_proj",
    )(x, g, w, qg, kg, cos_t, sin_t, wgu, wd, wo0, wo1)


def _diff_attn_kernel(lam_ref, qt_ref, k_ref, vt_ref, sg_ref, ot_ref,
                      qz_ref, m_ref, acc_ref, s_ref, *, lambda_init):
    lp = lam_ref[...]
    lam = (jnp.exp(jnp.sum(lp[0:1] * lp[1:2], axis=-1, keepdims=True))
           - jnp.exp(jnp.sum(lp[2:3] * lp[3:4], axis=-1, keepdims=True)) + lambda_init)

    def body(i, carry):
        _diff_attn_qblock(i, lam, qt_ref, k_ref, vt_ref, sg_ref, ot_ref,
                          qz_ref, m_ref, acc_ref, s_ref)
        return carry

    lax.fori_loop(0, qt_ref.shape[-1] // ATTN_BLOCK, body, 0)


def _diff_attn_qblock(i, lam, qt_ref, k_ref, vt_ref, sg_ref, ot_ref,
                      qz_ref, m_ref, acc_ref, s_ref):
    bq = ATTN_BLOCK
    hd2 = 2 * HEAD_DIM
    n_heads = qt_ref.shape[1] // hd2
    q_lanes = pl.ds(pl.multiple_of(i * bq, bq), bq)

    zero = jnp.zeros((HEAD_DIM, bq), BF16)
    for h in range(n_heads):
        q0 = qt_ref[0, h * hd2:h * hd2 + HEAD_DIM, q_lanes]
        q1 = qt_ref[0, h * hd2 + HEAD_DIM:(h + 1) * hd2, q_lanes]
        qz_ref[2 * h] = jnp.concatenate([q0, zero], axis=0)
        qz_ref[2 * h + 1] = jnp.concatenate([zero, q1], axis=0)

    def block(k0, bk, first):
        ones = jnp.ones((ONES_ROWS, bk), BF16)
        k0_packed = pl.multiple_of(lax.shift_right_logical(k0, 1), bk // 2)
        if first:
            kidx = lax.broadcasted_iota(jnp.int32, (bk, bq), 0)
            qidx = lax.broadcasted_iota(jnp.int32, (bk, bq), 1)
            keep = kidx <= qidx + (bk - bq)

        def scores(hc):
            h = hc // 2
            kblk = _unpack_rows(k_ref[0, pl.ds(k0_packed, bk // 2), h * hd2:(h + 1) * hd2])
            return jnp.dot(kblk, qz_ref[hc], preferred_element_type=F32)

        n_slots = s_ref.shape[0]

        def park(hc):
            s_ref[hc % n_slots, :bk, :] = scores(hc)

        for hc in range(LOOKAHEAD):
            park(hc)
        for hc in range(2 * n_heads):
            h = hc // 2
            if hc + LOOKAHEAD < 2 * n_heads:
                park(hc + LOOKAHEAD)
            s = s_ref[hc % n_slots, :bk, :]
            vblk = _unpack_rows(vt_ref[0, h * HEAD_DIM:(h + 1) * HEAD_DIM, pl.ds(k0, bk)])
            vext = jnp.concatenate([vblk, ones], axis=0)
            if first:
                s = jnp.where(keep, s, NEG_INF)
                m_new = jnp.max(s, axis=0, keepdims=True)
                p = jnp.exp2(s - m_new)
                acc_ref[hc] = jnp.dot(vext, p.astype(BF16), preferred_element_type=F32)
            else:
                m_old = m_ref[hc]
                m_new = jnp.maximum(m_old, jnp.max(s, axis=0, keepdims=True))
                alpha = jnp.exp2(m_old - m_new)
                p = jnp.exp2(s - m_new)
                acc_ref[hc] = alpha * acc_ref[hc] + jnp.dot(
                    vext, p.astype(BF16), preferred_element_type=F32)
            m_ref[hc] = m_new

    @pl.when((i & 1) == 0)
    def _():
        block(pl.multiple_of(i * bq, bq), bq, True)

    @pl.when((i & 1) == 1)
    def _():
        block(pl.multiple_of((i - 1) * bq, 2 * bq), 2 * bq, True)

    def body(j, carry):
        block(pl.multiple_of(j * 2 * bq, 2 * bq), 2 * bq, False)
        return carry

    lax.fori_loop(0, lax.shift_right_logical(i, 1), body, 0)

    for h in range(n_heads):
        a0 = acc_ref[2 * h]
        a1 = acc_ref[2 * h + 1]
        inv0 = 1.0 / a0[hd2:hd2 + 1]
        inv1 = lam / a1[hd2:hd2 + 1]
        o = a0[:hd2] * inv0 - a1[:hd2] * inv1
        r = lax.rsqrt(jnp.mean(o * o, axis=0, keepdims=True) + NORM_EPS)
        ot_ref[0, h * hd2:(h + 1) * hd2, q_lanes] = (
            o * r * sg_ref[...]).astype(BF16)


def _diff_attn(lam_p, qt, k, vt, sg, lambda_init):
    b, d, s = qt.shape
    hd2 = 2 * HEAD_DIM
    bq = ATTN_BLOCK
    n_half = d // HEAD_DIM
    return pl.pallas_call(
        functools.partial(_diff_attn_kernel, lambda_init=lambda_init),
        grid=(b,),
        in_specs=[_resident(lam_p.shape),
                  pl.BlockSpec((1, d, s), lambda bi: (bi, 0, 0)),
                  pl.BlockSpec((1, s // 2, d), lambda bi: (bi, 0, 0)),
                  pl.BlockSpec((1, d // 2, s), lambda bi: (bi, 0, 0)),
                  _resident(sg.shape)],
        out_specs=pl.BlockSpec((1, d, s), lambda bi: (bi, 0, 0)),
        out_shape=jax.ShapeDtypeStruct((b, d, s), BF16),
        scratch_shapes=[pltpu.VMEM((n_half, hd2, bq), BF16),
                        pltpu.VMEM((n_half, 1, bq), F32),
                        pltpu.VMEM((n_half, hd2 + ONES_ROWS, bq), F32),
                        pltpu.VMEM((LOOKAHEAD + 1, 2 * bq, bq), F32)],
        compiler_params=pltpu.CompilerParams(
            dimension_semantics=("parallel",),
            vmem_limit_bytes=V7X_VMEM_LIMIT_BYTES),
        name="diff_attn",
    )(lam_p, qt, k, vt, sg)


def _proj_ffn_kernel(x_ref, at_ref, wo_ref, g_ref, wg_ref, wu_ref, wd_ref, o_ref, hid_ref):
    x = x_ref[0] + lax.dot_general(at_ref[0], wo_ref[...], TN_DIMS,
                                   preferred_element_type=F32)
    h = (x * _rms_scale(x) * g_ref[...]).astype(BF16)
    n_chunks = hid_ref.shape[-1] // FFN_CHUNK
    for c in range(n_chunks):
        cols = slice(c * FFN_CHUNK, (c + 1) * FFN_CHUNK)
        gate = jnp.dot(h, wg_ref[:, cols], preferred_element_type=F32)
        up = jnp.dot(h, wu_ref[:, cols], preferred_element_type=F32)
        hid_ref[:, c * FFN_CHUNK:(c + 1) * FFN_CHUNK] = (
            gate * jax.nn.sigmoid(gate) * up).astype(BF16)
    o_ref[0] = x + jnp.dot(hid_ref[...], wd_ref[...], preferred_element_type=F32)


def _proj_ffn(x, at, wo, g, wgu, wd, layer):
    b, s, d = x.shape
    tm = ROW_TILE
    hidden = wd.shape[1]
    tok = pl.BlockSpec((1, tm, d), lambda i, j: (i, j, 0))
    feat = pl.BlockSpec((1, d, tm), lambda i, j: (i, 0, j))
    once = pl.Buffered(1)
    return pl.pallas_call(
        _proj_ffn_kernel,
        grid=(b, s // tm),
        in_specs=[tok, feat, _resident(wo.shape), _resident((1, d)),
                  pl.BlockSpec((None, d, hidden), lambda i, j: (layer, 0, 0), pipeline_mode=once),
                  pl.BlockSpec((None, d, hidden), lambda i, j: (layer, 0, 1), pipeline_mode=once),
                  pl.BlockSpec((None, hidden, d), lambda i, j: (layer, 0, 0), pipeline_mode=once)],
        out_specs=tok,
        out_shape=jax.ShapeDtypeStruct((b, s, d), F32),
        scratch_shapes=[pltpu.VMEM((tm, hidden), BF16)],
        compiler_params=pltpu.CompilerParams(
            dimension_semantics=("parallel", "parallel"),
            vmem_limit_bytes=V7X_VMEM_LIMIT_BYTES),
        name="proj_ffn",
    )(x, at, wo, g, wgu, wgu, wd)


def _qkv1_kernel(x_ref, ga_ref, gkv_ref, wq_ref, wkv_ref, qg_ref, kg_ref, cos_ref, sin_ref,
                 qt_ref, k_ref, vt_ref, wqt_ref, wkvt_ref, kv_ref, res_ref):
    d_model = x_ref.shape[-1]
    kv_dim = k_ref.shape[-1]

    @pl.when((pl.program_id(0) == 0) & (pl.program_id(1) == 0))
    def _():
        _transpose_pack_weight(wq_ref, wqt_ref)
        _transpose_pack_weight(wkv_ref, wkvt_ref)

    x = x_ref[0]
    xn = x * _rms_scale(x)
    h_a = (xn * ga_ref[...]).astype(BF16)
    h_kv = (xn * gkv_ref[...]).astype(BF16)
    cos = cos_ref[0]
    sin = sin_ref[0]

    n_chunks = d_model // FEATURE_CHUNK

    def project_q(c):
        rows = slice(c * FEATURE_CHUNK // 2, (c + 1) * FEATURE_CHUNK // 2)
        res_ref[c % 2] = lax.dot_general(_unpack_rows(wqt_ref[rows, :]), h_a, NT_DIMS,
                                         preferred_element_type=F32)

    kv_ref[...] = lax.dot_general(_unpack_rows(wkvt_ref[...]), h_kv, NT_DIMS,
                                  preferred_element_type=F32)
    project_q(0)
    kparts = [_norm_rope_t(kv_ref[g * HEAD_DIM:(g + 1) * HEAD_DIM, :], kg_ref[...], cos, sin)
              for g in range(kv_dim // HEAD_DIM)]
    k_ref[0] = _pack_rows(jnp.concatenate(kparts, axis=0).T.astype(BF16))
    vt_ref[0] = _pack_rows(kv_ref[kv_dim:, :].astype(BF16))

    for c in range(n_chunks):
        if c + 1 < n_chunks:
            project_q(c + 1)
        res = res_ref.at[c % 2]
        rows = slice(c * FEATURE_CHUNK, (c + 1) * FEATURE_CHUNK)
        parts = [_norm_rope_t(res[g * HEAD_DIM:(g + 1) * HEAD_DIM, :], qg_ref[...], cos, sin)
                 for g in range(FEATURE_CHUNK // HEAD_DIM)]
        qt_ref[0, rows, :] = jnp.concatenate(parts, axis=0).astype(BF16)


def _qkv1(x, ga, gkv, wq, wkv, qg, kg, cos_t, sin_t):
    b, s, d = x.shape
    tm = ROW_TILE
    kv_dim = wkv.shape[1] // 2
    tok = pl.BlockSpec((1, tm, d), lambda i, j: (i, j, 0))
    rope = pl.BlockSpec((1, HEAD_DIM // 2, tm), lambda i, j: (i, 0, j))
    return pl.pallas_call(
        _qkv1_kernel,
        grid=(b, s // tm),
        in_specs=[tok, _resident((1, d)), _resident((1, d)), _resident(wq.shape),
                  _resident(wkv.shape), _resident(qg.shape), _resident(kg.shape), rope, rope],
        out_specs=[pl.BlockSpec((1, d, tm), lambda i, j: (i, 0, j)),
                   pl.BlockSpec((1, tm // 2, kv_dim), lambda i, j: (i, j, 0)),
                   pl.BlockSpec((1, kv_dim // 2, tm), lambda i, j: (i, 0, j))],
        out_shape=[jax.ShapeDtypeStruct((b, d, s), BF16),
                   jax.ShapeDtypeStruct((b, s // 2, kv_dim), jnp.uint32),
                   jax.ShapeDtypeStruct((b, kv_dim // 2, s), jnp.uint32)],
        scratch_shapes=[pltpu.VMEM((wq.shape[1] // 2, d), jnp.uint32),
                        pltpu.VMEM((kv_dim, d), jnp.uint32),
                        pltpu.VMEM((2 * kv_dim, tm), F32),
                        pltpu.VMEM((2, FEATURE_CHUNK, tm), F32)],
        compiler_params=pltpu.CompilerParams(
            dimension_semantics=("arbitrary", "arbitrary"),
            vmem_limit_bytes=V7X_VMEM_LIMIT_BYTES),
        name="qkv1_proj",
    )(x, ga, gkv, wq, wkv, qg, kg, cos_t, sin_t)


def _swa_kernel(qt_ref, k_ref, vt_ref, sink_ref, ot_ref, bias_ref, s_ref):
    w = WINDOW
    s_len = qt_ref.shape[-1]
    kv_dim = k_ref.shape[-1]
    gw = SW_PACK * w

    n_kv = kv_dim // HEAD_DIM
    n_blocks = s_len // w

    kidx = lax.broadcasted_iota(jnp.int32, (2 * w, gw), 0)
    qidx = lax.broadcasted_iota(jnp.int32, (2 * w, gw), 1) & (w - 1)
    bias_ref[...] = jnp.where((kidx > qidx) & (kidx <= qidx + w), 0.0, NEG_INF).astype(F32)

    def window(n, first):
        nk = w if first else 2 * w
        k0 = 0 if first else pl.multiple_of((n - 1) * w, w)
        q0 = 0 if first else pl.multiple_of(n * w, w)
        return nk, k0, q0

    def heads(j, part):
        return [SW_GROUP * j + SW_PACK * part + u for u in range(SW_PACK)]

    def scores(n, j, part, first):
        nk, k0, q0 = window(n, first)
        k0_packed = k0 if first else pl.multiple_of((n - 1) * (w // 2), w // 2)
        kwin = _unpack_rows(k_ref[0, pl.ds(k0_packed, nk // 2), :])
        qcat = jnp.concatenate(
            [qt_ref[0, h * HEAD_DIM:(h + 1) * HEAD_DIM, pl.ds(q0, w)]
             for h in heads(j, part)], axis=1)
        pieces = []
        if j > 0:
            pieces.append(jnp.zeros((j * HEAD_DIM, gw), BF16))
        pieces.append(qcat)
        if j + 1 < n_kv:
            pieces.append(jnp.zeros(((n_kv - j - 1) * HEAD_DIM, gw), BF16))
        qz = jnp.concatenate(pieces, axis=0) if len(pieces) > 1 else qcat
        return jnp.dot(kwin, qz, preferred_element_type=F32)

    def finish(n, j, part, first, s):
        nk, k0, q0 = window(n, first)
        s = s + bias_ref[2 * w - nk:, :]
        sink = sink_ref[j:j + 1, part * gw:(part + 1) * gw]
        m = jnp.maximum(jnp.max(s, axis=0, keepdims=True), sink)
        e = jnp.exp2(s - m).astype(BF16)
        vwin = _unpack_rows(
            vt_ref[0, j * HEAD_DIM // 2:(j + 1) * HEAD_DIM // 2, pl.ds(k0, nk)])
        vext = jnp.concatenate([vwin, jnp.ones((ONES_ROWS, nk), BF16)], axis=0)
        o = jnp.dot(vext, e, preferred_element_type=F32)
        den = o[HEAD_DIM:HEAD_DIM + 1] + jnp.exp2(sink - m)
        o = o[:HEAD_DIM] * (1.0 / den)
        for u, h in enumerate(heads(j, part)):
            ot_ref[0, h * HEAD_DIM:(h + 1) * HEAD_DIM, pl.ds(q0, w)] = (
                o[:, u * w:(u + 1) * w].astype(BF16))

    def run(chains):
        n_slots = s_ref.shape[0]

        def park(t):
            nk = window(chains[t][0], chains[t][3])[0]
            s_ref[t % n_slots, :nk, :] = scores(*chains[t])

        for t in range(min(LOOKAHEAD, len(chains))):
            park(t)
        for t, ch in enumerate(chains):
            if t + LOOKAHEAD < len(chains):
                park(t + LOOKAHEAD)
            nk = window(ch[0], ch[3])[0]
            finish(*ch, s_ref[t % n_slots, :nk, :])

    parts = range(SW_GROUP // SW_PACK)
    run([(0, j, part, True) for j in range(n_kv) for part in parts])

    def body(t, carry):
        n0 = 1 + t * SWA_UNROLL
        run([(n0 + u, j, part, False)
             for u in range(SWA_UNROLL) for j in range(n_kv) for part in parts])
        return carry

    lax.fori_loop(0, (n_blocks - 1) // SWA_UNROLL, body, 0)


def _swa(qt, k, vt, sink_rows):
    b, d, s = qt.shape
    kv_dim = k.shape[-1]
    assert (s // WINDOW - 1) % SWA_UNROLL == 0
    return pl.pallas_call(
        _swa_kernel,
        grid=(b,),
        in_specs=[pl.BlockSpec((1, d, s), lambda i: (i, 0, 0)),
                  pl.BlockSpec((1, s // 2, kv_dim), lambda i: (i, 0, 0)),
                  pl.BlockSpec((1, kv_dim // 2, s), lambda i: (i, 0, 0)),
                  _resident(sink_rows.shape)],
        out_specs=pl.BlockSpec((1, d, s), lambda i: (i, 0, 0)),
        out_shape=jax.ShapeDtypeStruct((b, d, s), BF16),
        scratch_shapes=[pltpu.VMEM((2 * WINDOW, SW_PACK * WINDOW), F32),
                        pltpu.VMEM((LOOKAHEAD + 1, 2 * WINDOW, SW_PACK * WINDOW), F32)],
        compiler_params=pltpu.CompilerParams(
            dimension_semantics=("parallel",),
            vmem_limit_bytes=V7X_VMEM_LIMIT_BYTES),
        name="swa_attn",
    )(qt, k, vt, sink_rows)


def _lane_bcast(col, n):
    return jnp.broadcast_to(col.astype(F32).reshape(-1, 1), (col.size, n))


def kernel(x, positions, attn_norm, ffn_norm, w_gate_up, w_down, da_w_qkv, da_q_norm, da_k_norm,
           da_lambda, da_subln, da_w_o, kv_norm, w_kv, k_norm, sw_w_q, sw_q_norm, sw_sinks, sw_w_o):
    b, s, d = x.shape
    scale = 1.0 / math.sqrt(HEAD_DIM)

    inv = 1.0 / (ROPE_THETA ** (jnp.arange(0, HEAD_DIM, 2, dtype=F32) / HEAD_DIM))
    ang_t = positions.astype(F32)[:, None, :] * inv[None, :, None]
    cos_t = jnp.cos(ang_t)
    sin_t = jnp.sin(ang_t)

    lambda_init = 0.8 - 0.6 * math.exp(-0.3 * 0)
    n_layers, _, two_h = w_gate_up.shape
    qt, k, vt, wgu, wd, wo0, wo1 = _qkv0(
        x, attn_norm[0].reshape(1, d), da_w_qkv[0],
        _lane_bcast(da_q_norm[0] * (scale * LOG2_E), ROW_TILE),
        _lane_bcast(da_k_norm[0], ROW_TILE), cos_t, sin_t,
        w_gate_up.reshape(n_layers * d, two_h), w_down.reshape(n_layers * (two_h // 2), d),
        da_w_o[0], sw_w_o[0])
    wgu = wgu.reshape(n_layers, d, two_h)
    wd = wd.reshape(n_layers, two_h // 2, d)
    at = _diff_attn(da_lambda[0].astype(F32), qt, k, vt,
                    _lane_bcast(da_subln[0] * (1.0 - lambda_init), ATTN_BLOCK), lambda_init)
    x = _proj_ffn(x, at, wo0, ffn_norm[0].reshape(1, d), wgu, wd, 0)

    qt, k, vt = _qkv1(x, attn_norm[1].reshape(1, d), kv_norm.reshape(1, d),
                      sw_w_q[0], w_kv,
                      _lane_bcast(sw_q_norm[0] * (scale * LOG2_E), ROW_TILE),
                      _lane_bcast(k_norm, ROW_TILE), cos_t, sin_t)
    sink_rows = jnp.repeat((sw_sinks[0].astype(F32) * LOG2_E).reshape(SW_KV_HEADS, SW_GROUP),
                           WINDOW, axis=1)
    at = _swa(qt, k, vt, sink_rows)
    x = _proj_ffn(x, at, wo1, ffn_norm[1].reshape(1, d), wgu, wd, 1)
    return x
```

```python
import functools
import math

import jax
import jax.numpy as jnp
from jax import lax
from jax.experimental import pallas as pl
from jax.experimental.pallas import tpu as pltpu

HEAD_DIM = 64
ROPE_THETA = 10000.0
NORM_EPS = 1e-6
NEG_INF = -1e30
LOG2_E = math.log2(math.e)
WINDOW = 128
SW_KV_HEADS = 4
SW_GROUP = 4
SW_PACK = 2

F32 = jnp.float32
BF16 = jnp.bfloat16

V7X_VMEM_LIMIT_BYTES = 56 * 1024 * 1024

ROW_TILE = 1024
FEATURE_CHUNK = 512
FFN_CHUNK = 256
ATTN_BLOCK = 256
ONES_ROWS = 16
LOOKAHEAD = 6
HEAD_LOOKAHEAD = 3
SWA_UNROLL = 5

NT_DIMS = (((1,), (1,)), ((), ()))
TN_DIMS = (((0,), (0,)), ((), ()))


def _resident(shape):
    nd = len(shape)
    return pl.BlockSpec(shape, lambda *_: (0,) * nd, pipeline_mode=pl.Buffered(1))


def _rms_scale(x):
    return lax.rsqrt(jnp.mean(x * x, axis=-1, keepdims=True) + NORM_EPS)


def _pack_rows(x):
    return pltpu.bitcast(x, jnp.uint32)


def _unpack_rows(x):
    return pltpu.bitcast(x, BF16)


def _transpose_pack_weight(w_ref, wt_ref):
    n = w_ref.shape[1]
    for c in range(n // FEATURE_CHUNK):
        cols = slice(c * FEATURE_CHUNK, (c + 1) * FEATURE_CHUNK)
        rows = slice(c * FEATURE_CHUNK // 2, (c + 1) * FEATURE_CHUNK // 2)
        wt_ref[rows, :] = _pack_rows(w_ref[:, cols].T.astype(BF16))


def _norm_rope_t(t, gain, cos, sin):
    r = lax.rsqrt(jnp.mean(t * t, axis=0, keepdims=True) + NORM_EPS)
    tn = t * r * gain
    x1 = tn[: HEAD_DIM // 2]
    x2 = tn[HEAD_DIM // 2:]
    return jnp.concatenate([x1 * cos - x2 * sin, x2 * cos + x1 * sin], axis=0)


def _qkv0_kernel(x_ref, g_ref, w_ref, qg_ref, kg_ref, cos_ref, sin_ref,
                 wgu_ref, wd_ref, wo0_ref, wo1_ref,
                 qt_ref, k_ref, vt_ref, wgu_bf_ref, wd_bf_ref, wo0_bf_ref, wo1_bf_ref,
                 wt_ref, res_ref):
    d_model = x_ref.shape[-1]

    @pl.when((pl.program_id(0) == 0) & (pl.program_id(1) == 0))
    def _():
        _transpose_pack_weight(w_ref, wt_ref)

    @pl.when(pl.program_id(1) == 0)
    def _():
        wd_bf_ref[...] = wd_ref[...].astype(BF16)

    wgu_bf_ref[...] = wgu_ref[...].astype(BF16)
    wo0_bf_ref[...] = wo0_ref[...].astype(BF16)
    wo1_bf_ref[...] = wo1_ref[...].astype(BF16)

    x = x_ref[0]
    h = (x * _rms_scale(x) * g_ref[...]).astype(BF16)
    cos = cos_ref[0]
    sin = sin_ref[0]
    n_chunks = d_model // FEATURE_CHUNK

    def project(c):
        rows = slice(c * FEATURE_CHUNK // 2, (c + 1) * FEATURE_CHUNK // 2)
        w_rows = _unpack_rows(wt_ref[rows, :])
        res_ref[c % 2] = lax.dot_general(w_rows, h, NT_DIMS, preferred_element_type=F32)

    project(0)
    for c in range(3 * n_chunks):
        if c + 1 < 3 * n_chunks:
            project(c + 1)
        res = res_ref.at[c % 2]
        kind, cc = divmod(c, n_chunks)
        out_rows = slice(cc * FEATURE_CHUNK, (cc + 1) * FEATURE_CHUNK)
        half_rows = slice(cc * FEATURE_CHUNK // 2, (cc + 1) * FEATURE_CHUNK // 2)
        if kind == 2:
            vt_ref[0, half_rows, :] = _pack_rows(res[...].astype(BF16))
            continue
        gain_ref = qg_ref if kind == 0 else kg_ref
        parts = []
        for g in range(FEATURE_CHUNK // HEAD_DIM):
            half = g % 2
            gain = gain_ref[half * HEAD_DIM:(half + 1) * HEAD_DIM, :]
            parts.append(_norm_rope_t(res[g * HEAD_DIM:(g + 1) * HEAD_DIM, :], gain, cos, sin))
        out = jnp.concatenate(parts, axis=0)
        if kind == 0:
            qt_ref[0, out_rows, :] = out.astype(BF16)
        else:
            k_ref[0, :, out_rows] = _pack_rows(out.T.astype(BF16))


def _qkv0(x, g, w, qg, kg, cos_t, sin_t, wgu, wd, wo0, wo1):
    b, s, d = x.shape
    tm = ROW_TILE
    nt = s // tm
    grid = (b, nt)

    def per_step(a):
        return pl.BlockSpec((a.shape[0] // (b * nt), a.shape[1]), lambda i, j: (i * nt + j, 0))

    def per_batch(a):
        return pl.BlockSpec((a.shape[0] // b, a.shape[1]), lambda i, j: (i, 0))

    def as_bf16(a):
        return jax.ShapeDtypeStruct(a.shape, BF16)
    feat = pl.BlockSpec((1, d, tm), lambda i, j: (i, 0, j))
    feat_packed = pl.BlockSpec((1, d // 2, tm), lambda i, j: (i, 0, j))
    tok = pl.BlockSpec((1, tm, d), lambda i, j: (i, j, 0))
    tok_packed = pl.BlockSpec((1, tm // 2, d), lambda i, j: (i, j, 0))
    rope = pl.BlockSpec((1, HEAD_DIM // 2, tm), lambda i, j: (i, 0, j))
    return pl.pallas_call(
        _qkv0_kernel,
        grid=grid,
        in_specs=[tok, _resident((1, d)), _resident(w.shape), _resident(qg.shape),
                  _resident(kg.shape), rope, rope,
                  per_step(wgu), per_batch(wd), per_step(wo0), per_step(wo1)],
        out_specs=[feat, tok_packed, feat_packed,
                   per_step(wgu), per_batch(wd), per_step(wo0), per_step(wo1)],
        out_shape=[jax.ShapeDtypeStruct((b, d, s), BF16),
                   jax.ShapeDtypeStruct((b, s // 2, d), jnp.uint32),
                   jax.ShapeDtypeStruct((b, d // 2, s), jnp.uint32),
                   as_bf16(wgu), as_bf16(wd), as_bf16(wo0), as_bf16(wo1)],
        scratch_shapes=[pltpu.VMEM((w.shape[1] // 2, d), jnp.uint32),
                        pltpu.VMEM((2, FEATURE_CHUNK, tm), F32)],
        compiler_params=pltpu.CompilerParams(
            dimension_semantics=("arbitrary", "arbitrary"),
            vmem_limit_bytes=V7X_VMEM_LIMIT_BYTES),
        name="qkv0_proj",
    )(x, g, w, qg, kg, cos_t, sin_t, wgu, wd, wo0, wo1)


def _diff_attn_kernel(lam_ref, qt_ref, k_ref, vt_ref, sg_ref, ot_ref,
                      qz_ref, m_ref, acc_ref, s_ref, *, lambda_init):
    bq = qt_ref.shape[-1]
    hd2 = 2 * HEAD_DIM
    n_heads = qt_ref.shape[1] // hd2
    i = pl.program_id(1)

    lp = lam_ref[...]
    lam = (jnp.exp(jnp.sum(lp[0:1] * lp[1:2], axis=-1, keepdims=True))
           - jnp.exp(jnp.sum(lp[2:3] * lp[3:4], axis=-1, keepdims=True)) + lambda_init)

    zero = jnp.zeros((HEAD_DIM, bq), BF16)
    for h in range(n_heads):
        q0 = qt_ref[0, h * hd2:h * hd2 + HEAD_DIM, :]
        q1 = qt_ref[0, h * hd2 + HEAD_DIM:(h + 1) * hd2, :]
        qz_ref[h] = jnp.concatenate([jnp.concatenate([q0, zero], axis=0),
                                     jnp.concatenate([zero, q1], axis=0)], axis=1)

    def block(k0, bk, first):
        ones = jnp.ones((ONES_ROWS, bk), BF16)
        k0_packed = pl.multiple_of(lax.shift_right_logical(k0, 1), bk // 2)
        if first:
            kidx = lax.broadcasted_iota(jnp.int32, (bk, 2 * bq), 0)
            qidx = lax.broadcasted_iota(jnp.int32, (bk, 2 * bq), 1) & (bq - 1)
            keep = kidx <= qidx + (bk - bq)

        def scores(h):
            kblk = _unpack_rows(k_ref[0, pl.ds(k0_packed, bk // 2), h * hd2:(h + 1) * hd2])
            return jnp.dot(kblk, qz_ref[h], preferred_element_type=F32)

        n_slots = s_ref.shape[0]

        def park(h):
            s_ref[h % n_slots, :bk, :] = scores(h)

        for h in range(HEAD_LOOKAHEAD):
            park(h)
        for h in range(n_heads):
            if h + HEAD_LOOKAHEAD < n_heads:
                park(h + HEAD_LOOKAHEAD)
            s = s_ref[h % n_slots, :bk, :]
            vblk = _unpack_rows(vt_ref[0, h * HEAD_DIM:(h + 1) * HEAD_DIM, pl.ds(k0, bk)])
            vext = jnp.concatenate([vblk, ones], axis=0)
            if first:
                s = jnp.where(keep, s, NEG_INF)
                m_new = jnp.max(s, axis=0, keepdims=True)
                p = jnp.exp2(s - m_new)
                acc_ref[h] = jnp.dot(vext, p.astype(BF16), preferred_element_type=F32)
            else:
                m_old = m_ref[h]
                m_new = jnp.maximum(m_old, jnp.max(s, axis=0, keepdims=True))
                alpha = jnp.exp2(m_old - m_new)
                p = jnp.exp2(s - m_new)
                acc_ref[h] = alpha * acc_ref[h] + jnp.dot(
                    vext, p.astype(BF16), preferred_element_type=F32)
            m_ref[h] = m_new

    @pl.when((i & 1) == 0)
    def _():
        block(pl.multiple_of(i * bq, bq), bq, True)

    @pl.when((i & 1) == 1)
    def _():
        block(pl.multiple_of((i - 1) * bq, 2 * bq), 2 * bq, True)

    def body(j, carry):
        block(pl.multiple_of(j * 2 * bq, 2 * bq), 2 * bq, False)
        return carry

    lax.fori_loop(0, lax.shift_right_logical(i, 1), body, 0)

    for h in range(n_heads):
        acc = acc_ref[h]
        inv0 = 1.0 / acc[hd2:hd2 + 1, :bq]
        inv1 = lam / acc[hd2:hd2 + 1, bq:]
        o = acc[:hd2, :bq] * inv0 - acc[:hd2, bq:] * inv1
        r = lax.rsqrt(jnp.mean(o * o, axis=0, keepdims=True) + NORM_EPS)
        ot_ref[0, h * hd2:(h + 1) * hd2, :] = (
            o * r * sg_ref[...]).astype(BF16)


def _diff_attn(lam_p, qt, k, vt, sg, lambda_init):
    b, d, s = qt.shape
    hd2 = 2 * HEAD_DIM
    bq = ATTN_BLOCK
    n_heads = d // hd2
    return pl.pallas_call(
        functools.partial(_diff_attn_kernel, lambda_init=lambda_init),
        grid=(b, s // bq),
        in_specs=[_resident(lam_p.shape),
                  pl.BlockSpec((1, d, bq), lambda bi, i: (bi, 0, i)),
                  pl.BlockSpec((1, s // 2, d), lambda bi, i: (bi, 0, 0)),
                  pl.BlockSpec((1, d // 2, s), lambda bi, i: (bi, 0, 0)),
                  _resident(sg.shape)],
        out_specs=pl.BlockSpec((1, d, bq), lambda bi, i: (bi, 0, i)),
        out_shape=jax.ShapeDtypeStruct((b, d, s), BF16),
        scratch_shapes=[pltpu.VMEM((n_heads, hd2, 2 * bq), BF16),
                        pltpu.VMEM((n_heads, 1, 2 * bq), F32),
                        pltpu.VMEM((n_heads, hd2 + ONES_ROWS, 2 * bq), F32),
                        pltpu.VMEM((HEAD_LOOKAHEAD + 1, 2 * bq, 2 * bq), F32)],
        compiler_params=pltpu.CompilerParams(
            dimension_semantics=("parallel", "arbitrary"),
            vmem_limit_bytes=V7X_VMEM_LIMIT_BYTES),
        name="diff_attn",
    )(lam_p, qt, k, vt, sg)


def _proj_ffn_kernel(x_ref, at_ref, wo_ref, g_ref, wg_ref, wu_ref, wd_ref, o_ref, hid_ref):
    x = x_ref[0] + lax.dot_general(at_ref[0], wo_ref[...], TN_DIMS,
                                   preferred_element_type=F32)
    h = (x * _rms_scale(x) * g_ref[...]).astype(BF16)
    n_chunks = hid_ref.shape[-1] // FFN_CHUNK
    for c in range(n_chunks):
        cols = slice(c * FFN_CHUNK, (c + 1) * FFN_CHUNK)
        gate = jnp.dot(h, wg_ref[:, cols], preferred_element_type=F32)
        up = jnp.dot(h, wu_ref[:, cols], preferred_element_type=F32)
        hid_ref[:, c * FFN_CHUNK:(c + 1) * FFN_CHUNK] = (
            gate * jax.nn.sigmoid(gate) * up).astype(BF16)
    o_ref[0] = x + jnp.dot(hid_ref[...], wd_ref[...], preferred_element_type=F32)


def _proj_ffn(x, at, wo, g, wgu, wd, layer):
    b, s, d = x.shape
    tm = ROW_TILE
    hidden = wd.shape[1]
    tok = pl.BlockSpec((1, tm, d), lambda i, j: (i, j, 0))
    feat = pl.BlockSpec((1, d, tm), lambda i, j: (i, 0, j))
    once = pl.Buffered(1)
    return pl.pallas_call(
        _proj_ffn_kernel,
        grid=(b, s // tm),
        in_specs=[tok, feat, _resident(wo.shape), _resident((1, d)),
                  pl.BlockSpec((None, d, hidden), lambda i, j: (layer, 0, 0), pipeline_mode=once),
                  pl.BlockSpec((None, d, hidden), lambda i, j: (layer, 0, 1), pipeline_mode=once),
                  pl.BlockSpec((None, hidden, d), lambda i, j: (layer, 0, 0), pipeline_mode=once)],
        out_specs=tok,
        out_shape=jax.ShapeDtypeStruct((b, s, d), F32),
        scratch_shapes=[pltpu.VMEM((tm, hidden), BF16)],
        compiler_params=pltpu.CompilerParams(
            dimension_semantics=("parallel", "parallel"),
            vmem_limit_bytes=V7X_VMEM_LIMIT_BYTES),
        name="proj_ffn",
    )(x, at, wo, g, wgu, wgu, wd)


def _qkv1_kernel(x_ref, ga_ref, gkv_ref, wq_ref, wkv_ref, qg_ref, kg_ref, cos_ref, sin_ref,
                 qt_ref, k_ref, vt_ref, wqt_ref, wkvt_ref, kv_ref, res_ref):
    d_model = x_ref.shape[-1]
    kv_dim = k_ref.shape[-1]

    @pl.when((pl.program_id(0) == 0) & (pl.program_id(1) == 0))
    def _():
        _transpose_pack_weight(wq_ref, wqt_ref)
        _transpose_pack_weight(wkv_ref, wkvt_ref)

    x = x_ref[0]
    xn = x * _rms_scale(x)
    h_a = (xn * ga_ref[...]).astype(BF16)
    h_kv = (xn * gkv_ref[...]).astype(BF16)
    cos = cos_ref[0]
    sin = sin_ref[0]

    n_chunks = d_model // FEATURE_CHUNK

    def project_q(c):
        rows = slice(c * FEATURE_CHUNK // 2, (c + 1) * FEATURE_CHUNK // 2)
        res_ref[c % 2] = lax.dot_general(_unpack_rows(wqt_ref[rows, :]), h_a, NT_DIMS,
                                         preferred_element_type=F32)

    kv_ref[...] = lax.dot_general(_unpack_rows(wkvt_ref[...]), h_kv, NT_DIMS,
                                  preferred_element_type=F32)
    project_q(0)
    kparts = [_norm_rope_t(kv_ref[g * HEAD_DIM:(g + 1) * HEAD_DIM, :], kg_ref[...], cos, sin)
              for g in range(kv_dim // HEAD_DIM)]
    k_ref[0] = _pack_rows(jnp.concatenate(kparts, axis=0).T.astype(BF16))
    vt_ref[0] = _pack_rows(kv_ref[kv_dim:, :].astype(BF16))

    for c in range(n_chunks):
        if c + 1 < n_chunks:
            project_q(c + 1)
        res = res_ref.at[c % 2]
        rows = slice(c * FEATURE_CHUNK, (c + 1) * FEATURE_CHUNK)
        parts = [_norm_rope_t(res[g * HEAD_DIM:(g + 1) * HEAD_DIM, :], qg_ref[...], cos, sin)
                 for g in range(FEATURE_CHUNK // HEAD_DIM)]
        qt_ref[0, rows, :] = jnp.concatenate(parts, axis=0).astype(BF16)


def _qkv1(x, ga, gkv, wq, wkv, qg, kg, cos_t, sin_t):
    b, s, d = x.shape
    tm = ROW_TILE
    kv_dim = wkv.shape[1] // 2
    tok = pl.BlockSpec((1, tm, d), lambda i, j: (i, j, 0))
    rope = pl.BlockSpec((1, HEAD_DIM // 2, tm), lambda i, j: (i, 0, j))
    return pl.pallas_call(
        _qkv1_kernel,
        grid=(b, s // tm),
        in_specs=[tok, _resident((1, d)), _resident((1, d)), _resident(wq.shape),
                  _resident(wkv.shape), _resident(qg.shape), _resident(kg.shape), rope, rope],
        out_specs=[pl.BlockSpec((1, d, tm), lambda i, j: (i, 0, j)),
                   pl.BlockSpec((1, tm // 2, kv_dim), lambda i, j: (i, j, 0)),
                   pl.BlockSpec((1, kv_dim // 2, tm), lambda i, j: (i, 0, j))],
        out_shape=[jax.ShapeDtypeStruct((b, d, s), BF16),
                   jax.ShapeDtypeStruct((b, s // 2, kv_dim), jnp.uint32),
                   jax.ShapeDtypeStruct((b, kv_dim // 2, s), jnp.uint32)],
        scratch_shapes=[pltpu.VMEM((wq.shape[1] // 2, d), jnp.uint32),
                        pltpu.VMEM((kv_dim, d), jnp.uint32),
                        pltpu.VMEM((2 * kv_dim, tm), F32),
                        pltpu.VMEM((2, FEATURE_CHUNK, tm), F32)],
        compiler_params=pltpu.CompilerParams(
            dimension_semantics=("arbitrary", "arbitrary"),
            vmem_limit_bytes=V7X_VMEM_LIMIT_BYTES),
        name="qkv1_proj",
    )(x, ga, gkv, wq, wkv, qg, kg, cos_t, sin_t)


def _swa_kernel(qt_ref, k_ref, vt_ref, sink_ref, ot_ref, bias_ref, s_ref):
    w = WINDOW
    s_len = qt_ref.shape[-1]
    kv_dim = k_ref.shape[-1]
    gw = SW_PACK * w

    n_kv = kv_dim // HEAD_DIM
    n_blocks = s_len // w

    kidx = lax.broadcasted_iota(jnp.int32, (2 * w, gw), 0)
    qidx = lax.broadcasted_iota(jnp.int32, (2 * w, gw), 1) & (w - 1)
    bias_ref[...] = jnp.where((kidx > qidx) & (kidx <= qidx + w), 0.0, NEG_INF).astype(F32)

    def window(n, first):
        nk = w if first else 2 * w
        k0 = 0 if first else pl.multiple_of((n - 1) * w, w)
        q0 = 0 if first else pl.multiple_of(n * w, w)
        return nk, k0, q0

    def heads(j, part):
        return [SW_GROUP * j + SW_PACK * part + u for u in range(SW_PACK)]

    def scores(n, j, part, first):
        nk, k0, q0 = window(n, first)
        k0_packed = k0 if first else pl.multiple_of((n - 1) * (w // 2), w // 2)
        kwin = _unpack_rows(k_ref[0, pl.ds(k0_packed, nk // 2), :])
        qcat = jnp.concatenate(
            [qt_ref[0, h * HEAD_DIM:(h + 1) * HEAD_DIM, pl.ds(q0, w)]
             for h in heads(j, part)], axis=1)
        pieces = []
        if j > 0:
            pieces.append(jnp.zeros((j * HEAD_DIM, gw), BF16))
        pieces.append(qcat)
        if j + 1 < n_kv:
            pieces.append(jnp.zeros(((n_kv - j - 1) * HEAD_DIM, gw), BF16))
        qz = jnp.concatenate(pieces, axis=0) if len(pieces) > 1 else qcat
        return jnp.dot(kwin, qz, preferred_element_type=F32)

    def finish(n, j, part, first, s):
        nk, k0, q0 = window(n, first)
        s = s + bias_ref[2 * w - nk:, :]
        sink = sink_ref[j:j + 1, part * gw:(part + 1) * gw]
        m = jnp.maximum(jnp.max(s, axis=0, keepdims=True), sink)
        e = jnp.exp2(s - m).astype(BF16)
        vwin = _unpack_rows(
            vt_ref[0, j * HEAD_DIM // 2:(j + 1) * HEAD_DIM // 2, pl.ds(k0, nk)])
        vext = jnp.concatenate([vwin, jnp.ones((ONES_ROWS, nk), BF16)], axis=0)
        o = jnp.dot(vext, e, preferred_element_type=F32)
        den = o[HEAD_DIM:HEAD_DIM + 1] + jnp.exp2(sink - m)
        o = o[:HEAD_DIM] * (1.0 / den)
        for u, h in enumerate(heads(j, part)):
            ot_ref[0, h * HEAD_DIM:(h + 1) * HEAD_DIM, pl.ds(q0, w)] = (
                o[:, u * w:(u + 1) * w].astype(BF16))

    def run(chains):
        n_slots = s_ref.shape[0]

        def park(t):
            nk = window(chains[t][0], chains[t][3])[0]
            s_ref[t % n_slots, :nk, :] = scores(*chains[t])

        for t in range(min(LOOKAHEAD, len(chains))):
            park(t)
        for t, ch in enumerate(chains):
            if t + LOOKAHEAD < len(chains):
                park(t + LOOKAHEAD)
            nk = window(ch[0], ch[3])[0]
            finish(*ch, s_ref[t % n_slots, :nk, :])

    parts = range(SW_GROUP // SW_PACK)
    run([(0, j, part, True) for j in range(n_kv) for part in parts])

    def body(t, carry):
        n0 = 1 + t * SWA_UNROLL
        run([(n0 + u, j, part, False)
             for u in range(SWA_UNROLL) for j in range(n_kv) for part in parts])
        return carry

    lax.fori_loop(0, (n_blocks - 1) // SWA_UNROLL, body, 0)


def _swa(qt, k, vt, sink_rows):
    b, d, s = qt.shape
    kv_dim = k.shape[-1]
    assert (s // WINDOW - 1) % SWA_UNROLL == 0
    return pl.pallas_call(
        _swa_kernel,
        grid=(b,),
        in_specs=[pl.BlockSpec((1, d, s), lambda i: (i, 0, 0)),
                  pl.BlockSpec((1, s // 2, kv_dim), lambda i: (i, 0, 0)),
                  pl.BlockSpec((1, kv_dim // 2, s), lambda i: (i, 0, 0)),
                  _resident(sink_rows.shape)],
        out_specs=pl.BlockSpec((1, d, s), lambda i: (i, 0, 0)),
        out_shape=jax.ShapeDtypeStruct((b, d, s), BF16),
        scratch_shapes=[pltpu.VMEM((2 * WINDOW, SW_PACK * WINDOW), F32),
                        pltpu.VMEM((LOOKAHEAD + 1, 2 * WINDOW, SW_PACK * WINDOW), F32)],
        compiler_params=pltpu.CompilerParams(
            dimension_semantics=("parallel",),
            vmem_limit_bytes=V7X_VMEM_LIMIT_BYTES),
        name="swa_attn",
    )(qt, k, vt, sink_rows)


def _lane_bcast(col, n):
    return jnp.broadcast_to(col.astype(F32).reshape(-1, 1), (col.size, n))


def kernel(x, positions, attn_norm, ffn_norm, w_gate_up, w_down, da_w_qkv, da_q_norm, da_k_norm,
           da_lambda, da_subln, da_w_o, kv_norm, w_kv, k_norm, sw_w_q, sw_q_norm, sw_sinks, sw_w_o):
    b, s, d = x.shape
    scale = 1.0 / math.sqrt(HEAD_DIM)

    inv = 1.0 / (ROPE_THETA ** (jnp.arange(0, HEAD_DIM, 2, dtype=F32) / HEAD_DIM))
    ang_t = positions.astype(F32)[:, None, :] * inv[None, :, None]
    cos_t = jnp.cos(ang_t)
    sin_t = jnp.sin(ang_t)

    lambda_init = 0.8 - 0.6 * math.exp(-0.3 * 0)
    n_layers, _, two_h = w_gate_up.shape
    qt, k, vt, wgu, wd, wo0, wo1 = _qkv0(
        x, attn_norm[0].reshape(1, d), da_w_qkv[0],
        _lane_bcast(da_q_norm[0] * (scale * LOG2_E), ROW_TILE),
        _lane_bcast(da_k_norm[0], ROW_TILE), cos_t, sin_t,
        w_gate_up.reshape(n_layers * d, two_h), w_down.reshape(n_layers * (two_h // 2), d),
        da_w_o[0], sw_w_o[0])
    wgu = wgu.reshape(n_layers, d, two_h)
    wd = wd.reshape(n_layers, two_h // 2, d)
    at = _diff_attn(da_lambda[0].astype(F32), qt, k, vt,
                    _lane_bcast(da_subln[0] * (1.0 - lambda_init), ATTN_BLOCK), lambda_init)
    x = _proj_ffn(x, at, wo0, ffn_norm[0].reshape(1, d), wgu, wd, 0)

    qt, k, vt = _qkv1(x, attn_norm[1].reshape(1, d), kv_norm.reshape(1, d),
                      sw_w_q[0], w_kv,
                      _lane_bcast(sw_q_norm[0] * (scale * LOG2_E), ROW_TILE),
                      _lane_bcast(k_norm, ROW_TILE), cos_t, sin_t)
    sink_rows = jnp.repeat((sw_sinks[0].astype(F32) * LOG2_E).reshape(SW_KV_HEADS, SW_GROUP),
                           WINDOW, axis=1)
    at = _swa(qt, k, vt, sink_rows)
    x = _proj_ffn(x, at, wo1, ffn_norm[1].reshape(1, d), wgu, wd, 1)
    return x
```

```python
import functools
import math

import jax
import jax.numpy as jnp
from jax import lax
from jax.experimental import pallas as pl
from jax.experimental.pallas import tpu as pltpu

HEAD_DIM = 64
ROPE_THETA = 10000.0
NORM_EPS = 1e-6
NEG_INF = -1e30
LOG2_E = math.log2(math.e)
WINDOW = 128
SW_KV_HEADS = 4
SW_GROUP = 4
SW_PACK = 2

F32 = jnp.float32
BF16 = jnp.bfloat16

V7X_VMEM_LIMIT_BYTES = 56 * 1024 * 1024

ROW_TILE = 1024
QKV1_ROW_TILE = 2048
FEATURE_CHUNK = 512
FFN_CHUNK = 256
ATTN_BLOCK = 256
ONES_ROWS = 16
LOOKAHEAD = 6
SWA_UNROLL = 5

NT_DIMS = (((1,), (1,)), ((), ()))
TN_DIMS = (((0,), (0,)), ((), ()))


def _resident(shape):
    nd = len(shape)
    return pl.BlockSpec(shape, lambda *_: (0,) * nd, pipeline_mode=pl.Buffered(1))


def _rms_scale(x):
    return lax.rsqrt(jnp.mean(x * x, axis=-1, keepdims=True) + NORM_EPS)


def _pack_rows(x):
    return pltpu.bitcast(x, jnp.uint32)


def _unpack_rows(x):
    return pltpu.bitcast(x, BF16)


def _transpose_pack_weight(w_ref, wt_ref):
    n = w_ref.shape[1]
    for c in range(n // FEATURE_CHUNK):
        cols = slice(c * FEATURE_CHUNK, (c + 1) * FEATURE_CHUNK)
        rows = slice(c * FEATURE_CHUNK // 2, (c + 1) * FEATURE_CHUNK // 2)
        wt_ref[rows, :] = _pack_rows(w_ref[:, cols].T.astype(BF16))


def _norm_rope_t(t, gain, cos, sin):
    r = lax.rsqrt(jnp.mean(t * t, axis=0, keepdims=True) + NORM_EPS)
    tn = t * r * gain
    x1 = tn[: HEAD_DIM // 2]
    x2 = tn[HEAD_DIM // 2:]
    return jnp.concatenate([x1 * cos - x2 * sin, x2 * cos + x1 * sin], axis=0)


def _qkv0_kernel(x_ref, g_ref, w_ref, qg_ref, kg_ref, cos_ref, sin_ref,
                 wgu_ref, wd_ref, wo0_ref, wo1_ref,
                 qt_ref, k_ref, vt_ref, wgu_bf_ref, wd_bf_ref, wo0_bf_ref, wo1_bf_ref,
                 wt_ref, res_ref):
    d_model = x_ref.shape[-1]

    @pl.when((pl.program_id(0) == 0) & (pl.program_id(1) == 0))
    def _():
        _transpose_pack_weight(w_ref, wt_ref)

    @pl.when(pl.program_id(1) == 0)
    def _():
        wd_bf_ref[...] = wd_ref[...].astype(BF16)

    wgu_bf_ref[...] = wgu_ref[...].astype(BF16)
    wo0_bf_ref[...] = wo0_ref[...].astype(BF16)
    wo1_bf_ref[...] = wo1_ref[...].astype(BF16)

    x = x_ref[0]
    h = (x * _rms_scale(x) * g_ref[...]).astype(BF16)
    cos = cos_ref[0]
    sin = sin_ref[0]
    n_chunks = d_model // FEATURE_CHUNK

    def project(c):
        rows = slice(c * FEATURE_CHUNK // 2, (c + 1) * FEATURE_CHUNK // 2)
        w_rows = _unpack_rows(wt_ref[rows, :])
        res_ref[c % 2] = lax.dot_general(w_rows, h, NT_DIMS, preferred_element_type=F32)

    project(0)
    for c in range(3 * n_chunks):
        if c + 1 < 3 * n_chunks:
            project(c + 1)
        res = res_ref.at[c % 2]
        kind, cc = divmod(c, n_chunks)
        out_rows = slice(cc * FEATURE_CHUNK, (cc + 1) * FEATURE_CHUNK)
        half_rows = slice(cc * FEATURE_CHUNK // 2, (cc + 1) * FEATURE_CHUNK // 2)
        if kind == 2:
            vt_ref[0, half_rows, :] = _pack_rows(res[...].astype(BF16))
            continue
        gain_ref = qg_ref if kind == 0 else kg_ref
        parts = []
        for g in range(FEATURE_CHUNK // HEAD_DIM):
            half = g % 2
            gain = gain_ref[half * HEAD_DIM:(half + 1) * HEAD_DIM, :]
            parts.append(_norm_rope_t(res[g * HEAD_DIM:(g + 1) * HEAD_DIM, :], gain, cos, sin))
        out = jnp.concatenate(parts, axis=0)
        if kind == 0:
            qt_ref[0, out_rows, :] = out.astype(BF16)
        else:
            k_ref[0, :, out_rows] = _pack_rows(out.T.astype(BF16))


def _qkv0(x, g, w, qg, kg, cos_t, sin_t, wgu, wd, wo0, wo1):
    b, s, d = x.shape
    tm = ROW_TILE
    nt = s // tm
    grid = (b, nt)

    def per_step(a):
        return pl.BlockSpec((a.shape[0] // (b * nt), a.shape[1]), lambda i, j: (i * nt + j, 0))

    def per_batch(a):
        return pl.BlockSpec((a.shape[0] // b, a.shape[1]), lambda i, j: (i, 0))

    def as_bf16(a):
        return jax.ShapeDtypeStruct(a.shape, BF16)
    feat = pl.BlockSpec((1, d, tm), lambda i, j: (i, 0, j))
    feat_packed = pl.BlockSpec((1, d // 2, tm), lambda i, j: (i, 0, j))
    tok = pl.BlockSpec((1, tm, d), lambda i, j: (i, j, 0))
    tok_packed = pl.BlockSpec((1, tm // 2, d), lambda i, j: (i, j, 0))
    rope = pl.BlockSpec((1, HEAD_DIM // 2, tm), lambda i, j: (i, 0, j))
    return pl.pallas_call(
        _qkv0_kernel,
        grid=grid,
        in_specs=[tok, _resident((1, d)), _resident(w.shape), _resident(qg.shape),
                  _resident(kg.shape), rope, rope,
                  per_step(wgu), per_batch(wd), per_step(wo0), per_step(wo1)],
        out_specs=[feat, tok_packed, feat_packed,
                   per_step(wgu), per_batch(wd), per_step(wo0), per_step(wo1)],
        out_shape=[jax.ShapeDtypeStruct((b, d, s), BF16),
                   jax.ShapeDtypeStruct((b, s // 2, d), jnp.uint32),
                   jax.ShapeDtypeStruct((b, d // 2, s), jnp.uint32),
                   as_bf16(wgu), as_bf16(wd), as_bf16(wo0), as_bf16(wo1)],
        scratch_shapes=[pltpu.VMEM((w.shape[1] // 2, d), jnp.uint32),
                        pltpu.VMEM((2, FEATURE_CHUNK, tm), F32)],
        compiler_params=pltpu.CompilerParams(
            dimension_semantics=("arbitrary", "arbitrary"),
            vmem_limit_bytes=V7X_VMEM_LIMIT_BYTES),
        name="qkv0_proj",
    )(x, g, w, qg, kg, cos_t, sin_t, wgu, wd, wo0, wo1)


def _diff_attn_kernel(lam_ref, qt_ref, k_ref, vt_ref, sg_ref, ot_ref,
                      qz_ref, m_ref, acc_ref, s_ref, *, lambda_init):
    bq = qt_ref.shape[-1]
    hd2 = 2 * HEAD_DIM
    n_heads = qt_ref.shape[1] // hd2
    i = pl.program_id(1)

    lp = lam_ref[...]
    lam = (jnp.exp(jnp.sum(lp[0:1] * lp[1:2], axis=-1, keepdims=True))
           - jnp.exp(jnp.sum(lp[2:3] * lp[3:4], axis=-1, keepdims=True)) + lambda_init)

    zero = jnp.zeros((HEAD_DIM, bq), BF16)
    for h in range(n_heads):
        q0 = qt_ref[0, h * hd2:h * hd2 + HEAD_DIM, :]
        q1 = qt_ref[0, h * hd2 + HEAD_DIM:(h + 1) * hd2, :]
        qz_ref[2 * h] = jnp.concatenate([q0, zero], axis=0)
        qz_ref[2 * h + 1] = jnp.concatenate([zero, q1], axis=0)

    def block(k0, bk, first):
        ones = jnp.ones((ONES_ROWS, bk), BF16)
        k0_packed = pl.multiple_of(lax.shift_right_logical(k0, 1), bk // 2)
        if first:
            kidx = lax.broadcasted_iota(jnp.int32, (bk, bq), 0)
            qidx = lax.broadcasted_iota(jnp.int32, (bk, bq), 1)
            keep = kidx <= qidx + (bk - bq)

        def scores(hc):
            h = hc // 2
            kblk = _unpack_rows(k_ref[0, pl.ds(k0_packed, bk // 2), h * hd2:(h + 1) * hd2])
            return jnp.dot(kblk, qz_ref[hc], preferred_element_type=F32)

        n_slots = s_ref.shape[0]

        def park(hc):
            s_ref[hc % n_slots, :bk, :] = scores(hc)

        for hc in range(LOOKAHEAD):
            park(hc)
        for hc in range(2 * n_heads):
            h = hc // 2
            if hc + LOOKAHEAD < 2 * n_heads:
                park(hc + LOOKAHEAD)
            s = s_ref[hc % n_slots, :bk, :]
            vblk = _unpack_rows(vt_ref[0, h * HEAD_DIM:(h + 1) * HEAD_DIM, pl.ds(k0, bk)])
            vext = jnp.concatenate([vblk, ones], axis=0)
            if first:
                s = jnp.where(keep, s, NEG_INF)
                m_new = jnp.max(s, axis=0, keepdims=True)
                p = jnp.exp2(s - m_new)
                acc_ref[hc] = jnp.dot(vext, p.astype(BF16), preferred_element_type=F32)
            else:
                m_old = m_ref[hc]
                m_new = jnp.maximum(m_old, jnp.max(s, axis=0, keepdims=True))
                alpha = jnp.exp2(m_old - m_new)
                p = jnp.exp2(s - m_new)
                acc_ref[hc] = alpha * acc_ref[hc] + jnp.dot(
                    vext, p.astype(BF16), preferred_element_type=F32)
            m_ref[hc] = m_new

    @pl.when((i & 1) == 0)
    def _():
        block(pl.multiple_of(i * bq, bq), bq, True)

    @pl.when((i & 1) == 1)
    def _():
        block(pl.multiple_of((i - 1) * bq, 2 * bq), 2 * bq, True)

    def body(j, carry):
        block(pl.multiple_of(j * 2 * bq, 2 * bq), 2 * bq, False)
        return carry

    lax.fori_loop(0, lax.shift_right_logical(i, 1), body, 0)

    for h in range(n_heads):
        a0 = acc_ref[2 * h]
        a1 = acc_ref[2 * h + 1]
        inv0 = 1.0 / a0[hd2:hd2 + 1]
        inv1 = lam / a1[hd2:hd2 + 1]
        o = a0[:hd2] * inv0 - a1[:hd2] * inv1
        r = lax.rsqrt(jnp.mean(o * o, axis=0, keepdims=True) + NORM_EPS)
        ot_ref[0, h * hd2:(h + 1) * hd2, :] = (
            o * r * sg_ref[...]).astype(BF16)


def _diff_attn(lam_p, qt, k, vt, sg, lambda_init):
    b, d, s = qt.shape
    hd2 = 2 * HEAD_DIM
    bq = ATTN_BLOCK
    n_half = d // HEAD_DIM
    return pl.pallas_call(
        functools.partial(_diff_attn_kernel, lambda_init=lambda_init),
        grid=(b, s // bq),
        in_specs=[_resident(lam_p.shape),
                  pl.BlockSpec((1, d, bq), lambda bi, i: (bi, 0, i)),
                  pl.BlockSpec((1, s // 2, d), lambda bi, i: (bi, 0, 0)),
                  pl.BlockSpec((1, d // 2, s), lambda bi, i: (bi, 0, 0)),
                  _resident(sg.shape)],
        out_specs=pl.BlockSpec((1, d, bq), lambda bi, i: (bi, 0, i)),
        out_shape=jax.ShapeDtypeStruct((b, d, s), BF16),
        scratch_shapes=[pltpu.VMEM((n_half, hd2, bq), BF16),
                        pltpu.VMEM((n_half, 1, bq), F32),
                        pltpu.VMEM((n_half, hd2 + ONES_ROWS, bq), F32),
                        pltpu.VMEM((LOOKAHEAD + 1, 2 * bq, bq), F32)],
        compiler_params=pltpu.CompilerParams(
            dimension_semantics=("parallel", "arbitrary"),
            vmem_limit_bytes=V7X_VMEM_LIMIT_BYTES),
        name="diff_attn",
    )(lam_p, qt, k, vt, sg)


def _proj_ffn_kernel(x_ref, at_ref, wo_ref, g_ref, wg_ref, wu_ref, wd_ref, o_ref, hid_ref):
    x = x_ref[0] + lax.dot_general(at_ref[0], wo_ref[...], TN_DIMS,
                                   preferred_element_type=F32)
    h = (x * _rms_scale(x) * g_ref[...]).astype(BF16)
    n_chunks = hid_ref.shape[-1] // FFN_CHUNK
    for c in range(n_chunks):
        cols = slice(c * FFN_CHUNK, (c + 1) * FFN_CHUNK)
        gate = jnp.dot(h, wg_ref[:, cols], preferred_element_type=F32)
        up = jnp.dot(h, wu_ref[:, cols], preferred_element_type=F32)
        hid_ref[:, c * FFN_CHUNK:(c + 1) * FFN_CHUNK] = (
            gate * jax.nn.sigmoid(gate) * up).astype(BF16)
    o_ref[0] = x + jnp.dot(hid_ref[...], wd_ref[...], preferred_element_type=F32)


def _proj_ffn(x, at, wo, g, wgu, wd, layer):
    b, s, d = x.shape
    tm = ROW_TILE
    hidden = wd.shape[1]
    tok = pl.BlockSpec((1, tm, d), lambda i, j: (i, j, 0))
    feat = pl.BlockSpec((1, d, tm), lambda i, j: (i, 0, j))
    once = pl.Buffered(1)
    return pl.pallas_call(
        _proj_ffn_kernel,
        grid=(b, s // tm),
        in_specs=[tok, feat, _resident(wo.shape), _resident((1, d)),
                  pl.BlockSpec((None, d, hidden), lambda i, j: (layer, 0, 0), pipeline_mode=once),
                  pl.BlockSpec((None, d, hidden), lambda i, j: (layer, 0, 1), pipeline_mode=once),
                  pl.BlockSpec((None, hidden, d), lambda i, j: (layer, 0, 0), pipeline_mode=once)],
        out_specs=tok,
        out_shape=jax.ShapeDtypeStruct((b, s, d), F32),
        scratch_shapes=[pltpu.VMEM((tm, hidden), BF16)],
        compiler_params=pltpu.CompilerParams(
            dimension_semantics=("parallel", "parallel"),
            vmem_limit_bytes=V7X_VMEM_LIMIT_BYTES),
        name="proj_ffn",
    )(x, at, wo, g, wgu, wgu, wd)


def _qkv1_kernel(x_ref, ga_ref, gkv_ref, wq_ref, wkv_ref, qg_ref, kg_ref, cos_ref, sin_ref,
                 qt_ref, k_ref, vt_ref, wqt_ref, wkvt_ref, kv_ref, res_ref):
    d_model = x_ref.shape[-1]
    kv_dim = k_ref.shape[-1]

    @pl.when((pl.program_id(0) == 0) & (pl.program_id(1) == 0))
    def _():
        _transpose_pack_weight(wq_ref, wqt_ref)
        _transpose_pack_weight(wkv_ref, wkvt_ref)

    x = x_ref[0]
    xn = x * _rms_scale(x)
    h_a = (xn * ga_ref[...]).astype(BF16)
    h_kv = (xn * gkv_ref[...]).astype(BF16)
    cos = cos_ref[0]
    sin = sin_ref[0]

    n_chunks = d_model // FEATURE_CHUNK

    def project_q(c):
        rows = slice(c * FEATURE_CHUNK // 2, (c + 1) * FEATURE_CHUNK // 2)
        res_ref[c % 2] = lax.dot_general(_unpack_rows(wqt_ref[rows, :]), h_a, NT_DIMS,
                                         preferred_element_type=F32)

    kv_ref[...] = lax.dot_general(_unpack_rows(wkvt_ref[...]), h_kv, NT_DIMS,
                                  preferred_element_type=F32)
    project_q(0)
    kparts = [_norm_rope_t(kv_ref[g * HEAD_DIM:(g + 1) * HEAD_DIM, :], kg_ref[...], cos, sin)
              for g in range(kv_dim // HEAD_DIM)]
    k_ref[0] = _pack_rows(jnp.concatenate(kparts, axis=0).T.astype(BF16))
    vt_ref[0] = _pack_rows(kv_ref[kv_dim:, :].astype(BF16))

    for c in range(n_chunks):
        if c + 1 < n_chunks:
            project_q(c + 1)
        res = res_ref.at[c % 2]
        rows = slice(c * FEATURE_CHUNK, (c + 1) * FEATURE_CHUNK)
        parts = [_norm_rope_t(res[g * HEAD_DIM:(g + 1) * HEAD_DIM, :], qg_ref[...], cos, sin)
                 for g in range(FEATURE_CHUNK // HEAD_DIM)]
        qt_ref[0, rows, :] = jnp.concatenate(parts, axis=0).astype(BF16)


def _qkv1(x, ga, gkv, wq, wkv, qg, kg, cos_t, sin_t):
    b, s, d = x.shape
    tm = QKV1_ROW_TILE
    kv_dim = wkv.shape[1] // 2
    tok = pl.BlockSpec((1, tm, d), lambda i, j: (i, j, 0))
    rope = pl.BlockSpec((1, HEAD_DIM // 2, tm), lambda i, j: (i, 0, j))
    return pl.pallas_call(
        _qkv1_kernel,
        grid=(b, s // tm),
        in_specs=[tok, _resident((1, d)), _resident((1, d)), _resident(wq.shape),
                  _resident(wkv.shape), _resident(qg.shape), _resident(kg.shape), rope, rope],
        out_specs=[pl.BlockSpec((1, d, tm), lambda i, j: (i, 0, j)),
                   pl.BlockSpec((1, tm // 2, kv_dim), lambda i, j: (i, j, 0)),
                   pl.BlockSpec((1, kv_dim // 2, tm), lambda i, j: (i, 0, j))],
        out_shape=[jax.ShapeDtypeStruct((b, d, s), BF16),
                   jax.ShapeDtypeStruct((b, s // 2, kv_dim), jnp.uint32),
                   jax.ShapeDtypeStruct((b, kv_dim // 2, s), jnp.uint32)],
        scratch_shapes=[pltpu.VMEM((wq.shape[1] // 2, d), jnp.uint32),
                        pltpu.VMEM((kv_dim, d), jnp.uint32),
                        pltpu.VMEM((2 * kv_dim, tm), F32),
                        pltpu.VMEM((2, FEATURE_CHUNK, tm), F32)],
        compiler_params=pltpu.CompilerParams(
            dimension_semantics=("arbitrary", "arbitrary"),
            vmem_limit_bytes=V7X_VMEM_LIMIT_BYTES),
        name="qkv1_proj",
    )(x, ga, gkv, wq, wkv, qg, kg, cos_t, sin_t)


def _swa_kernel(qt_ref, k_ref, vt_ref, sink_ref, ot_ref, bias_ref, s_ref):
    w = WINDOW
    s_len = qt_ref.shape[-1]
    kv_dim = k_ref.shape[-1]
    gw = SW_PACK * w

    n_kv = kv_dim // HEAD_DIM
    n_blocks = s_len // w

    kidx = lax.broadcasted_iota(jnp.int32, (2 * w, gw), 0)
    qidx = lax.broadcasted_iota(jnp.int32, (2 * w, gw), 1) & (w - 1)
    bias_ref[...] = jnp.where((kidx > qidx) & (kidx <= qidx + w), 0.0, NEG_INF).astype(F32)

    def window(n, first):
        nk = w if first else 2 * w
        k0 = 0 if first else pl.multiple_of((n - 1) * w, w)
        q0 = 0 if first else pl.multiple_of(n * w, w)
        return nk, k0, q0

    def heads(j, part):
        return [SW_GROUP * j + SW_PACK * part + u for u in range(SW_PACK)]

    def scores(n, j, part, first):
        nk, k0, q0 = window(n, first)
        k0_packed = k0 if first else pl.multiple_of((n - 1) * (w // 2), w // 2)
        kwin = _unpack_rows(k_ref[0, pl.ds(k0_packed, nk // 2), :])
        qcat = jnp.concatenate(
            [qt_ref[0, h * HEAD_DIM:(h + 1) * HEAD_DIM, pl.ds(q0, w)]
             for h in heads(j, part)], axis=1)
        pieces = []
        if j > 0:
            pieces.append(jnp.zeros((j * HEAD_DIM, gw), BF16))
        pieces.append(qcat)
        if j + 1 < n_kv:
            pieces.append(jnp.zeros(((n_kv - j - 1) * HEAD_DIM, gw), BF16))
        qz = jnp.concatenate(pieces, axis=0) if len(pieces) > 1 else qcat
        return jnp.dot(kwin, qz, preferred_element_type=F32)

    def finish(n, j, part, first, s):
        nk, k0, q0 = window(n, first)
        s = s + bias_ref[2 * w - nk:, :]
        sink = sink_ref[j:j + 1, part * gw:(part + 1) * gw]
        m = jnp.maximum(jnp.max(s, axis=0, keepdims=True), sink)
        e = jnp.exp2(s - m).astype(BF16)
        vwin = _unpack_rows(
            vt_ref[0, j * HEAD_DIM // 2:(j + 1) * HEAD_DIM // 2, pl.ds(k0, nk)])
        vext = jnp.concatenate([vwin, jnp.ones((ONES_ROWS, nk), BF16)], axis=0)
        o = jnp.dot(vext, e, preferred_element_type=F32)
        den = o[HEAD_DIM:HEAD_DIM + 1] + jnp.exp2(sink - m)
        o = o[:HEAD_DIM] * (1.0 / den)
        for u, h in enumerate(heads(j, part)):
            ot_ref[0, h * HEAD_DIM:(h + 1) * HEAD_DIM, pl.ds(q0, w)] = (
                o[:, u * w:(u + 1) * w].astype(BF16))

    def run(chains):
        n_slots = s_ref.shape[0]

        def park(t):
            nk = window(chains[t][0], chains[t][3])[0]
            s_ref[t % n_slots, :nk, :] = scores(*chains[t])

        for t in range(min(LOOKAHEAD, len(chains))):
            park(t)
        for t, ch in enumerate(chains):
            if t + LOOKAHEAD < len(chains):
                park(t + LOOKAHEAD)
            nk = window(ch[0], ch[3])[0]
            finish(*ch, s_ref[t % n_slots, :nk, :])

    parts = range(SW_GROUP // SW_PACK)
    run([(0, j, part, True) for j in range(n_kv) for part in parts])

    def body(t, carry):
        n0 = 1 + t * SWA_UNROLL
        run([(n0 + u, j, part, False)
             for u in range(SWA_UNROLL) for j in range(n_kv) for part in parts])
        return carry

    lax.fori_loop(0, (n_blocks - 1) // SWA_UNROLL, body, 0)


def _swa(qt, k, vt, sink_rows):
    b, d, s = qt.shape
    kv_dim = k.shape[-1]
    assert (s // WINDOW - 1) % SWA_UNROLL == 0
    return pl.pallas_call(
        _swa_kernel,
        grid=(b,),
        in_specs=[pl.BlockSpec((1, d, s), lambda i: (i, 0, 0)),
                  pl.BlockSpec((1, s // 2, kv_dim), lambda i: (i, 0, 0)),
                  pl.BlockSpec((1, kv_dim // 2, s), lambda i: (i, 0, 0)),
                  _resident(sink_rows.shape)],
        out_specs=pl.BlockSpec((1, d, s), lambda i: (i, 0, 0)),
        out_shape=jax.ShapeDtypeStruct((b, d, s), BF16),
        scratch_shapes=[pltpu.VMEM((2 * WINDOW, SW_PACK * WINDOW), F32),
                        pltpu.VMEM((LOOKAHEAD + 1, 2 * WINDOW, SW_PACK * WINDOW), F32)],
        compiler_params=pltpu.CompilerParams(
            dimension_semantics=("parallel",),
            vmem_limit_bytes=V7X_VMEM_LIMIT_BYTES),
        name="swa_attn",
    )(qt, k, vt, sink_rows)


def _lane_bcast(col, n):
    return jnp.broadcast_to(col.astype(F32).reshape(-1, 1), (col.size, n))


def kernel(x, positions, attn_norm, ffn_norm, w_gate_up, w_down, da_w_qkv, da_q_norm, da_k_norm,
           da_lambda, da_subln, da_w_o, kv_norm, w_kv, k_norm, sw_w_q, sw_q_norm, sw_sinks, sw_w_o):
    b, s, d = x.shape
    scale = 1.0 / math.sqrt(HEAD_DIM)

    inv = 1.0 / (ROPE_THETA ** (jnp.arange(0, HEAD_DIM, 2, dtype=F32) / HEAD_DIM))
    ang_t = positions.astype(F32)[:, None, :] * inv[None, :, None]
    cos_t = jnp.cos(ang_t)
    sin_t = jnp.sin(ang_t)

    lambda_init = 0.8 - 0.6 * math.exp(-0.3 * 0)
    n_layers, _, two_h = w_gate_up.shape
    qt, k, vt, wgu, wd, wo0, wo1 = _qkv0(
        x, attn_norm[0].reshape(1, d), da_w_qkv[0],
        _lane_bcast(da_q_norm[0] * (scale * LOG2_E), ROW_TILE),
        _lane_bcast(da_k_norm[0], ROW_TILE), cos_t, sin_t,
        w_gate_up.reshape(n_layers * d, two_h), w_down.reshape(n_layers * (two_h // 2), d),
        da_w_o[0], sw_w_o[0])
    wgu = wgu.reshape(n_layers, d, two_h)
    wd = wd.reshape(n_layers, two_h // 2, d)
    at = _diff_attn(da_lambda[0].astype(F32), qt, k, vt,
                    _lane_bcast(da_subln[0] * (1.0 - lambda_init), ATTN_BLOCK), lambda_init)
    x = _proj_ffn(x, at, wo0, ffn_norm[0].reshape(1, d), wgu, wd, 0)

    qt, k, vt = _qkv1(x, attn_norm[1].reshape(1, d), kv_norm.reshape(1, d),
                      sw_w_q[0], w_kv,
                      _lane_bcast(sw_q_norm[0] * (scale * LOG2_E), QKV1_ROW_TILE),
                      _lane_bcast(k_norm, QKV1_ROW_TILE), cos_t, sin_t)
    sink_rows = jnp.repeat((sw_sinks[0].astype(F32) * LOG2_E).reshape(SW_KV_HEADS, SW_GROUP),
                           WINDOW, axis=1)
    at = _swa(qt, k, vt, sink_rows)
    x = _proj_ffn(x, at, wo1, ffn_norm[1].reshape(1, d), wgu, wd, 1)
    return x
```

```python
import functools
import math

import jax
import jax.numpy as jnp
from jax import lax
from jax.experimental import pallas as pl
from jax.experimental.pallas import tpu as pltpu

HEAD_DIM = 64
ROPE_THETA = 10000.0
NORM_EPS = 1e-6
NEG_INF = -1e30
LOG2_E = math.log2(math.e)
WINDOW = 128
SW_KV_HEADS = 4
SW_GROUP = 4
SW_PACK = 2

F32 = jnp.float32
BF16 = jnp.bfloat16

V7X_VMEM_LIMIT_BYTES = 56 * 1024 * 1024

ROW_TILE = 1024
FEATURE_CHUNK = 512
FFN_CHUNK = 256
ATTN_BLOCK = 256
LANE_TILE = 128
ONES_ROWS = 16
LOOKAHEAD = 6
SWA_UNROLL = 5

TN_DIMS = (((0,), (0,)), ((), ()))


def _resident(shape):
    nd = len(shape)
    return pl.BlockSpec(shape, lambda *_: (0,) * nd, pipeline_mode=pl.Buffered(1))


def _rms_scale(x):
    return lax.rsqrt(jnp.mean(x * x, axis=-1, keepdims=True) + NORM_EPS)


def _pack_rows(x):
    return pltpu.bitcast(x, jnp.uint32)


def _unpack_rows(x):
    return pltpu.bitcast(x, BF16)


def _pack_weight(w_ref, wp_ref):
    for c in range(w_ref.shape[1] // FEATURE_CHUNK):
        cols = slice(c * FEATURE_CHUNK, (c + 1) * FEATURE_CHUNK)
        wp_ref[:, cols] = _pack_rows(w_ref[:, cols].astype(BF16))


def _project_t(h, wp_ref, cols):
    return jnp.dot(h, _unpack_rows(wp_ref[:, cols]), preferred_element_type=F32).T


def _norm_rope_t(t, gain, cos, sin):
    r = lax.rsqrt(jnp.mean(t * t, axis=0, keepdims=True) + NORM_EPS)
    tr = t * r
    tn = jnp.concatenate([tr[:, l:l + LANE_TILE] * gain
                          for l in range(0, t.shape[1], LANE_TILE)], axis=1)
    x1 = tn[: HEAD_DIM // 2]
    x2 = tn[HEAD_DIM // 2:]
    return jnp.concatenate([x1 * cos - x2 * sin, x2 * cos + x1 * sin], axis=0)


def _qkv0_kernel(x_ref, g_ref, w_ref, qg_ref, kg_ref, cos_ref, sin_ref,
                 wgu_ref, wd_ref, wo0_ref, wo1_ref,
                 qt_ref, k_ref, vt_ref, wgu_bf_ref, wd_bf_ref, wo0_bf_ref, wo1_bf_ref,
                 wp_ref, res_ref):
    d_model = x_ref.shape[-1]

    @pl.when((pl.program_id(0) == 0) & (pl.program_id(1) == 0))
    def _():
        _pack_weight(w_ref, wp_ref)

    @pl.when(pl.program_id(1) == 0)
    def _():
        wd_bf_ref[...] = wd_ref[...].astype(BF16)

    wgu_bf_ref[...] = wgu_ref[...].astype(BF16)
    wo0_bf_ref[...] = wo0_ref[...].astype(BF16)
    wo1_bf_ref[...] = wo1_ref[...].astype(BF16)

    x = x_ref[0]
    h = (x * _rms_scale(x) * g_ref[...]).astype(BF16)
    cos = cos_ref[0]
    sin = sin_ref[0]
    n_chunks = d_model // FEATURE_CHUNK

    def project(c):
        res_ref[c % 2] = _project_t(h, wp_ref, slice(c * FEATURE_CHUNK, (c + 1) * FEATURE_CHUNK))

    project(0)
    for c in range(3 * n_chunks):
        if c + 1 < 3 * n_chunks:
            project(c + 1)
        res = res_ref.at[c % 2]
        kind, cc = divmod(c, n_chunks)
        out_rows = slice(cc * FEATURE_CHUNK, (cc + 1) * FEATURE_CHUNK)
        half_rows = slice(cc * FEATURE_CHUNK // 2, (cc + 1) * FEATURE_CHUNK // 2)
        if kind == 2:
            vt_ref[0, half_rows, :] = _pack_rows(res[...].astype(BF16))
            continue
        gain_ref = qg_ref if kind == 0 else kg_ref
        parts = []
        for g in range(FEATURE_CHUNK // HEAD_DIM):
            half = g % 2
            gain = gain_ref[half * HEAD_DIM:(half + 1) * HEAD_DIM, :]
            parts.append(_norm_rope_t(res[g * HEAD_DIM:(g + 1) * HEAD_DIM, :], gain, cos, sin))
        out = jnp.concatenate(parts, axis=0)
        if kind == 0:
            qt_ref[0, out_rows, :] = out.astype(BF16)
        else:
            k_ref[0, :, out_rows] = _pack_rows(out.T.astype(BF16))


def _qkv0(x, g, w, qg, kg, cos_t, sin_t, wgu, wd, wo0, wo1):
    b, s, d = x.shape
    tm = ROW_TILE
    nt = s // tm
    grid = (b, nt)

    def per_step(a):
        return pl.BlockSpec((a.shape[0] // (b * nt), a.shape[1]), lambda i, j: (i * nt + j, 0))

    def per_batch(a):
        return pl.BlockSpec((a.shape[0] // b, a.shape[1]), lambda i, j: (i, 0))

    def as_bf16(a):
        return jax.ShapeDtypeStruct(a.shape, BF16)
    feat = pl.BlockSpec((1, d, tm), lambda i, j: (i, 0, j))
    feat_packed = pl.BlockSpec((1, d // 2, tm), lambda i, j: (i, 0, j))
    tok = pl.BlockSpec((1, tm, d), lambda i, j: (i, j, 0))
    tok_packed = pl.BlockSpec((1, tm // 2, d), lambda i, j: (i, j, 0))
    rope = pl.BlockSpec((1, HEAD_DIM // 2, tm), lambda i, j: (i, 0, j))
    return pl.pallas_call(
        _qkv0_kernel,
        grid=grid,
        in_specs=[tok, _resident((1, d)), _resident(w.shape), _resident(qg.shape),
                  _resident(kg.shape), rope, rope,
                  per_step(wgu), per_batch(wd), per_step(wo0), per_step(wo1)],
        out_specs=[feat, tok_packed, feat_packed,
                   per_step(wgu), per_batch(wd), per_step(wo0), per_step(wo1)],
        out_shape=[jax.ShapeDtypeStruct((b, d, s), BF16),
                   jax.ShapeDtypeStruct((b, s // 2, d), jnp.uint32),
                   jax.ShapeDtypeStruct((b, d // 2, s), jnp.uint32),
                   as_bf16(wgu), as_bf16(wd), as_bf16(wo0), as_bf16(wo1)],
        scratch_shapes=[pltpu.VMEM((d // 2, w.shape[1]), jnp.uint32),
                        pltpu.VMEM((2, FEATURE_CHUNK, tm), F32)],
        compiler_params=pltpu.CompilerParams(
            dimension_semantics=("arbitrary", "arbitrary"),
            vmem_limit_bytes=V7X_VMEM_LIMIT_BYTES),
        name="qkv0_proj",
    )(x, g, w, qg, kg, cos_t, sin_t, wgu, wd, wo0, wo1)


def _diff_attn_kernel(lam_ref, qt_ref, k_ref, vt_ref, sg_ref, ot_ref,
                      qz_ref, m_ref, acc_ref, s_ref, *, lambda_init):
    bq = qt_ref.shape[-1]
    hd2 = 2 * HEAD_DIM
    n_heads = qt_ref.shape[1] // hd2
    i = pl.program_id(1)

    lp = lam_ref[...]
    lam = (jnp.exp(jnp.sum(lp[0:1] * lp[1:2], axis=-1, keepdims=True))
           - jnp.exp(jnp.sum(lp[2:3] * lp[3:4], axis=-1, keepdims=True)) + lambda_init)

    zero = jnp.zeros((HEAD_DIM, bq), BF16)
    for h in range(n_heads):
        q0 = qt_ref[0, h * hd2:h * hd2 + HEAD_DIM, :]
        q1 = qt_ref[0, h * hd2 + HEAD_DIM:(h + 1) * hd2, :]
        qz_ref[2 * h] = jnp.concatenate([q0, zero], axis=0)
        qz_ref[2 * h + 1] = jnp.concatenate([zero, q1], axis=0)

    def block(k0, bk, first):
        ones = jnp.ones((ONES_ROWS, bk), BF16)
        k0_packed = pl.multiple_of(lax.shift_right_logical(k0, 1), bk // 2)
        if first:
            kidx = lax.broadcasted_iota(jnp.int32, (bk, bq), 0)
            qidx = lax.broadcasted_iota(jnp.int32, (bk, bq), 1)
            keep = kidx <= qidx + (bk - bq)

        def scores(hc):
            h = hc // 2
            kblk = _unpack_rows(k_ref[0, pl.ds(k0_packed, bk // 2), h * hd2:(h + 1) * hd2])
            return jnp.dot(kblk, qz_ref[hc], preferred_element_type=F32)

        n_slots = s_ref.shape[0]

        def park(hc):
            s_ref[hc % n_slots, :bk, :] = scores(hc)

        for hc in range(LOOKAHEAD):
            park(hc)
        for hc in range(2 * n_heads):
            h = hc // 2
            if hc + LOOKAHEAD < 2 * n_heads:
                park(hc + LOOKAHEAD)
            s = s_ref[hc % n_slots, :bk, :]
            vblk = _unpack_rows(vt_ref[0, h * HEAD_DIM:(h + 1) * HEAD_DIM, pl.ds(k0, bk)])
            vext = jnp.concatenate([vblk, ones], axis=0)
            if first:
                s = jnp.where(keep, s, NEG_INF)
                m_new = jnp.max(s, axis=0, keepdims=True)
                p = jnp.exp2(s - m_new)
                acc_ref[hc] = jnp.dot(vext, p.astype(BF16), preferred_element_type=F32)
            else:
                m_old = m_ref[hc]
                m_new = jnp.maximum(m_old, jnp.max(s, axis=0, keepdims=True))
                alpha = jnp.exp2(m_old - m_new)
                p = jnp.exp2(s - m_new)
                acc_ref[hc] = alpha * acc_ref[hc] + jnp.dot(
                    vext, p.astype(BF16), preferred_element_type=F32)
            m_ref[hc] = m_new

    @pl.when((i & 1) == 0)
    def _():
        block(pl.multiple_of(i * bq, bq), bq, True)

    @pl.when((i & 1) == 1)
    def _():
        block(pl.multiple_of((i - 1) * bq, 2 * bq), 2 * bq, True)

    def body(j, carry):
        block(pl.multiple_of(j * 2 * bq, 2 * bq), 2 * bq, False)
        return carry

    lax.fori_loop(0, lax.shift_right_logical(i, 1), body, 0)

    for h in range(n_heads):
        a0 = acc_ref[2 * h]
        a1 = acc_ref[2 * h + 1]
        inv0 = 1.0 / a0[hd2:hd2 + 1]
        inv1 = lam / a1[hd2:hd2 + 1]
        o = a0[:hd2] * inv0 - a1[:hd2] * inv1
        r = lax.rsqrt(jnp.mean(o * o, axis=0, keepdims=True) + NORM_EPS)
        ot_ref[0, h * hd2:(h + 1) * hd2, :] = (
            o * r * sg_ref[...]).astype(BF16)


def _diff_attn(lam_p, qt, k, vt, sg, lambda_init):
    b, d, s = qt.shape
    hd2 = 2 * HEAD_DIM
    bq = ATTN_BLOCK
    n_half = d // HEAD_DIM
    return pl.pallas_call(
        functools.partial(_diff_attn_kernel, lambda_init=lambda_init),
        grid=(b, s // bq),
        in_specs=[_resident(lam_p.shape),
                  pl.BlockSpec((1, d, bq), lambda bi, i: (bi, 0, i)),
                  pl.BlockSpec((1, s // 2, d), lambda bi, i: (bi, 0, 0)),
                  pl.BlockSpec((1, d // 2, s), lambda bi, i: (bi, 0, 0)),
                  _resident(sg.shape)],
        out_specs=pl.BlockSpec((1, d, bq), lambda bi, i: (bi, 0, i)),
        out_shape=jax.ShapeDtypeStruct((b, d, s), BF16),
        scratch_shapes=[pltpu.VMEM((n_half, hd2, bq), BF16),
                        pltpu.VMEM((n_half, 1, bq), F32),
                        pltpu.VMEM((n_half, hd2 + ONES_ROWS, bq), F32),
                        pltpu.VMEM((LOOKAHEAD + 1, 2 * bq, bq), F32)],
        compiler_params=pltpu.CompilerParams(
            dimension_semantics=("parallel", "arbitrary"),
            vmem_limit_bytes=V7X_VMEM_LIMIT_BYTES),
        name="diff_attn",
    )(lam_p, qt, k, vt, sg)


def _proj_ffn_kernel(x_ref, at_ref, wo_ref, g_ref, wg_ref, wu_ref, wd_ref, o_ref, hid_ref):
    x = x_ref[0] + lax.dot_general(at_ref[0], wo_ref[...], TN_DIMS,
                                   preferred_element_type=F32)
    h = (x * _rms_scale(x) * g_ref[...]).astype(BF16)
    n_chunks = hid_ref.shape[-1] // FFN_CHUNK
    for c in range(n_chunks):
        cols = slice(c * FFN_CHUNK, (c + 1) * FFN_CHUNK)
        gate = jnp.dot(h, wg_ref[:, cols], preferred_element_type=F32)
        up = jnp.dot(h, wu_ref[:, cols], preferred_element_type=F32)
        hid_ref[:, c * FFN_CHUNK:(c + 1) * FFN_CHUNK] = (
            gate * jax.nn.sigmoid(gate) * up).astype(BF16)
    o_ref[0] = x + jnp.dot(hid_ref[...], wd_ref[...], preferred_element_type=F32)


def _proj_ffn(x, at, wo, g, wgu, wd, layer):
    b, s, d = x.shape
    tm = ROW_TILE
    hidden = wd.shape[1]
    tok = pl.BlockSpec((1, tm, d), lambda i, j: (i, j, 0))
    feat = pl.BlockSpec((1, d, tm), lambda i, j: (i, 0, j))
    once = pl.Buffered(1)
    return pl.pallas_call(
        _proj_ffn_kernel,
        grid=(b, s // tm),
        in_specs=[tok, feat, _resident(wo.shape), _resident((1, d)),
                  pl.BlockSpec((None, d, hidden), lambda i, j: (layer, 0, 0), pipeline_mode=once),
                  pl.BlockSpec((None, d, hidden), lambda i, j: (layer, 0, 1), pipeline_mode=once),
                  pl.BlockSpec((None, hidden, d), lambda i, j: (layer, 0, 0), pipeline_mode=once)],
        out_specs=tok,
        out_shape=jax.ShapeDtypeStruct((b, s, d), F32),
        scratch_shapes=[pltpu.VMEM((tm, hidden), BF16)],
        compiler_params=pltpu.CompilerParams(
            dimension_semantics=("parallel", "parallel"),
            vmem_limit_bytes=V7X_VMEM_LIMIT_BYTES),
        name="proj_ffn",
    )(x, at, wo, g, wgu, wgu, wd)


def _qkv1_kernel(x_ref, ga_ref, gkv_ref, wq_ref, wkv_ref, qg_ref, kg_ref, cos_ref, sin_ref,
                 qt_ref, k_ref, vt_ref, wqp_ref, wkvp_ref, kv_ref, res_ref):
    d_model = x_ref.shape[-1]
    kv_dim = k_ref.shape[-1]

    @pl.when((pl.program_id(0) == 0) & (pl.program_id(1) == 0))
    def _():
        _pack_weight(wq_ref, wqp_ref)
        _pack_weight(wkv_ref, wkvp_ref)

    x = x_ref[0]
    xn = x * _rms_scale(x)
    h_a = (xn * ga_ref[...]).astype(BF16)
    h_kv = (xn * gkv_ref[...]).astype(BF16)
    cos = cos_ref[0]
    sin = sin_ref[0]

    n_chunks = d_model // FEATURE_CHUNK

    def project_q(c):
        res_ref[c % 2] = _project_t(h_a, wqp_ref,
                                    slice(c * FEATURE_CHUNK, (c + 1) * FEATURE_CHUNK))

    kv_ref[...] = _project_t(h_kv, wkvp_ref, slice(0, 2 * kv_dim))
    project_q(0)
    kparts = [_norm_rope_t(kv_ref[g * HEAD_DIM:(g + 1) * HEAD_DIM, :], kg_ref[...], cos, sin)
              for g in range(kv_dim // HEAD_DIM)]
    k_ref[0] = _pack_rows(jnp.concatenate(kparts, axis=0).T.astype(BF16))
    vt_ref[0] = _pack_rows(kv_ref[kv_dim:, :].astype(BF16))

    for c in range(n_chunks):
        if c + 1 < n_chunks:
            project_q(c + 1)
        res = res_ref.at[c % 2]
        rows = slice(c * FEATURE_CHUNK, (c + 1) * FEATURE_CHUNK)
        parts = [_norm_rope_t(res[g * HEAD_DIM:(g + 1) * HEAD_DIM, :], qg_ref[...], cos, sin)
                 for g in range(FEATURE_CHUNK // HEAD_DIM)]
        qt_ref[0, rows, :] = jnp.concatenate(parts, axis=0).astype(BF16)


def _qkv1(x, ga, gkv, wq, wkv, qg, kg, cos_t, sin_t):
    b, s, d = x.shape
    tm = ROW_TILE
    kv_dim = wkv.shape[1] // 2
    tok = pl.BlockSpec((1, tm, d), lambda i, j: (i, j, 0))
    rope = pl.BlockSpec((1, HEAD_DIM // 2, tm), lambda i, j: (i, 0, j))
    return pl.pallas_call(
        _qkv1_kernel,
        grid=(b, s // tm),
        in_specs=[tok, _resident((1, d)), _resident((1, d)), _resident(wq.shape),
                  _resident(wkv.shape), _resident(qg.shape), _resident(kg.shape), rope, rope],
        out_specs=[pl.BlockSpec((1, d, tm), lambda i, j: (i, 0, j)),
                   pl.BlockSpec((1, tm // 2, kv_dim), lambda i, j: (i, j, 0)),
                   pl.BlockSpec((1, kv_dim // 2, tm), lambda i, j: (i, 0, j))],
        out_shape=[jax.ShapeDtypeStruct((b, d, s), BF16),
                   jax.ShapeDtypeStruct((b, s // 2, kv_dim), jnp.uint32),
                   jax.ShapeDtypeStruct((b, kv_dim // 2, s), jnp.uint32)],
        scratch_shapes=[pltpu.VMEM((d // 2, wq.shape[1]), jnp.uint32),
                        pltpu.VMEM((d // 2, wkv.shape[1]), jnp.uint32),
                        pltpu.VMEM((2 * kv_dim, tm), F32),
                        pltpu.VMEM((2, FEATURE_CHUNK, tm), F32)],
        compiler_params=pltpu.CompilerParams(
            dimension_semantics=("arbitrary", "arbitrary"),
            vmem_limit_bytes=V7X_VMEM_LIMIT_BYTES),
        name="qkv1_proj",
    )(x, ga, gkv, wq, wkv, qg, kg, cos_t, sin_t)


def _swa_kernel(qt_ref, k_ref, vt_ref, sink_ref, ot_ref, bias_ref, s_ref):
    w = WINDOW
    s_len = qt_ref.shape[-1]
    kv_dim = k_ref.shape[-1]
    gw = SW_PACK * w

    n_kv = kv_dim // HEAD_DIM
    n_blocks = s_len // w

    kidx = lax.broadcasted_iota(jnp.int32, (2 * w, gw), 0)
    qidx = lax.broadcasted_iota(jnp.int32, (2 * w, gw), 1) & (w - 1)
    bias_ref[...] = jnp.where((kidx > qidx) & (kidx <= qidx + w), 0.0, NEG_INF).astype(F32)

    def window(n, first):
        nk = w if first else 2 * w
        k0 = 0 if first else pl.multiple_of((n - 1) * w, w)
        q0 = 0 if first else pl.multiple_of(n * w, w)
        return nk, k0, q0

    def heads(j, part):
        return [SW_GROUP * j + SW_PACK * part + u for u in range(SW_PACK)]

    def scores(n, j, part, first):
        nk, k0, q0 = window(n, first)
        k0_packed = k0 if first else pl.multiple_of((n - 1) * (w // 2), w // 2)
        kwin = _unpack_rows(k_ref[0, pl.ds(k0_packed, nk // 2), :])
        qcat = jnp.concatenate(
            [qt_ref[0, h * HEAD_DIM:(h + 1) * HEAD_DIM, pl.ds(q0, w)]
             for h in heads(j, part)], axis=1)
        pieces = []
        if j > 0:
            pieces.append(jnp.zeros((j * HEAD_DIM, gw), BF16))
        pieces.append(qcat)
        if j + 1 < n_kv:
            pieces.append(jnp.zeros(((n_kv - j - 1) * HEAD_DIM, gw), BF16))
        qz = jnp.concatenate(pieces, axis=0) if len(pieces) > 1 else qcat
        return jnp.dot(kwin, qz, preferred_element_type=F32)

    def finish(n, j, part, first, s):
        nk, k0, q0 = window(n, first)
        s = s + bias_ref[2 * w - nk:, :]
        sink = sink_ref[j:j + 1, part * gw:(part + 1) * gw]
        m = jnp.maximum(jnp.max(s, axis=0, keepdims=True), sink)
        e = jnp.exp2(s - m).astype(BF16)
        vwin = _unpack_rows(
            vt_ref[0, j * HEAD_DIM // 2:(j + 1) * HEAD_DIM // 2, pl.ds(k0, nk)])
        vext = jnp.concatenate([vwin, jnp.ones((ONES_ROWS, nk), BF16)], axis=0)
        o = jnp.dot(vext, e, preferred_element_type=F32)
        den = o[HEAD_DIM:HEAD_DIM + 1] + jnp.exp2(sink - m)
        o = o[:HEAD_DIM] * (1.0 / den)
        for u, h in enumerate(heads(j, part)):
            ot_ref[0, h * HEAD_DIM:(h + 1) * HEAD_DIM, pl.ds(q0, w)] = (
                o[:, u * w:(u + 1) * w].astype(BF16))

    def run(chains):
        n_slots = s_ref.shape[0]

        def park(t):
            nk = window(chains[t][0], chains[t][3])[0]
            s_ref[t % n_slots, :nk, :] = scores(*chains[t])

        for t in range(min(LOOKAHEAD, len(chains))):
            park(t)
        for t, ch in enumerate(chains):
            if t + LOOKAHEAD < len(chains):
                park(t + LOOKAHEAD)
            nk = window(ch[0], ch[3])[0]
            finish(*ch, s_ref[t % n_slots, :nk, :])

    parts = range(SW_GROUP // SW_PACK)
    run([(0, j, part, True) for j in range(n_kv) for part in parts])

    def body(t, carry):
        n0 = 1 + t * SWA_UNROLL
        run([(n0 + u, j, part, False)
             for u in range(SWA_UNROLL) for j in range(n_kv) for part in parts])
        return carry

    lax.fori_loop(0, (n_blocks - 1) // SWA_UNROLL, body, 0)


def _swa(qt, k, vt, sink_rows):
    b, d, s = qt.shape
    kv_dim = k.shape[-1]
    assert (s // WINDOW - 1) % SWA_UNROLL == 0
    return pl.pallas_call(
        _swa_kernel,
        grid=(b,),
        in_specs=[pl.BlockSpec((1, d, s), lambda i: (i, 0, 0)),
                  pl.BlockSpec((1, s // 2, kv_dim), lambda i: (i, 0, 0)),
                  pl.BlockSpec((1, kv_dim // 2, s), lambda i: (i, 0, 0)),
                  _resident(sink_rows.shape)],
        out_specs=pl.BlockSpec((1, d, s), lambda i: (i, 0, 0)),
        out_shape=jax.ShapeDtypeStruct((b, d, s), BF16),
        scratch_shapes=[pltpu.VMEM((2 * WINDOW, SW_PACK * WINDOW), F32),
                        pltpu.VMEM((LOOKAHEAD + 1, 2 * WINDOW, SW_PACK * WINDOW), F32)],
        compiler_params=pltpu.CompilerParams(
            dimension_semantics=("parallel",),
            vmem_limit_bytes=V7X_VMEM_LIMIT_BYTES),
        name="swa_attn",
    )(qt, k, vt, sink_rows)


def _lane_bcast(col, n):
    return jnp.broadcast_to(col.astype(F32).reshape(-1, 1), (col.size, n))


def kernel(x, positions, attn_norm, ffn_norm, w_gate_up, w_down, da_w_qkv, da_q_norm, da_k_norm,
           da_lambda, da_subln, da_w_o, kv_norm, w_kv, k_norm, sw_w_q, sw_q_norm, sw_sinks, sw_w_o):
    b, s, d = x.shape
    scale = 1.0 / math.sqrt(HEAD_DIM)

    inv = 1.0 / (ROPE_THETA ** (jnp.arange(0, HEAD_DIM, 2, dtype=F32) / HEAD_DIM))
    ang_t = positions.astype(F32)[:, None, :] * inv[None, :, None]
    cos_t = jnp.cos(ang_t)
    sin_t = jnp.sin(ang_t)

    lambda_init = 0.8 - 0.6 * math.exp(-0.3 * 0)
    n_layers, _, two_h = w_gate_up.shape
    qt, k, vt, wgu, wd, wo0, wo1 = _qkv0(
        x, attn_norm[0].reshape(1, d), da_w_qkv[0],
        _lane_bcast(da_q_norm[0] * (scale * LOG2_E), LANE_TILE),
        _lane_bcast(da_k_norm[0], LANE_TILE), cos_t, sin_t,
        w_gate_up.reshape(n_layers * d, two_h), w_down.reshape(n_layers * (two_h // 2), d),
        da_w_o[0], sw_w_o[0])
    wgu = wgu.reshape(n_layers, d, two_h)
    wd = wd.reshape(n_layers, two_h // 2, d)
    at = _diff_attn(da_lambda[0].astype(F32), qt, k, vt,
                    _lane_bcast(da_subln[0] * (1.0 - lambda_init), ATTN_BLOCK), lambda_init)
    x = _proj_ffn(x, at, wo0, ffn_norm[0].reshape(1, d), wgu, wd, 0)

    qt, k, vt = _qkv1(x, attn_norm[1].reshape(1, d), kv_norm.reshape(1, d),
                      sw_w_q[0], w_kv,
                      _lane_bcast(sw_q_norm[0] * (scale * LOG2_E), LANE_TILE),
                      _lane_bcast(k_norm, LANE_TILE), cos_t, sin_t)
    sink_rows = jnp.repeat((sw_sinks[0].astype(F32) * LOG2_E).reshape(SW_KV_HEADS, SW_GROUP),
                           WINDOW, axis=1)
    at = _swa(qt, k, vt, sink_rows)
    x = _proj_ffn(x, at, wo1, ffn_norm[1].reshape(1, d), wgu, wd, 1)
    return x
```

```python
import functools
import math

import jax
import jax.numpy as jnp
from jax import lax
from jax.experimental import pallas as pl
from jax.experimental.pallas import tpu as pltpu

HEAD_DIM = 64
ROPE_THETA = 10000.0
NORM_EPS = 1e-6
NEG_INF = -1e30
LOG2_E = math.log2(math.e)
WINDOW = 128
SW_KV_HEADS = 4
SW_GROUP = 4
SW_PACK = 2

F32 = jnp.float32
BF16 = jnp.bfloat16

V7X_VMEM_LIMIT_BYTES = 56 * 1024 * 1024

ROW_TILE = 1024
FEATURE_CHUNK = 512
FFN_CHUNK = 256
ATTN_BLOCK = 256
LANE_TILE = 128
ONES_ROWS = 16
LOOKAHEAD = 6
SWA_UNROLL = 5

TN_DIMS = (((0,), (0,)), ((), ()))


def _resident(shape):
    nd = len(shape)
    return pl.BlockSpec(shape, lambda *_: (0,) * nd, pipeline_mode=pl.Buffered(1))


def _rms_scale(x):
    return lax.rsqrt(jnp.mean(x * x, axis=-1, keepdims=True) + NORM_EPS)


def _pack_rows(x):
    return pltpu.bitcast(x, jnp.uint32)


def _unpack_rows(x):
    return pltpu.bitcast(x, BF16)


def _pack_weight(w_ref, wp_ref):
    for c in range(w_ref.shape[1] // FEATURE_CHUNK):
        cols = slice(c * FEATURE_CHUNK, (c + 1) * FEATURE_CHUNK)
        wp_ref[:, cols] = _pack_rows(w_ref[:, cols].astype(BF16))


def _project_t(h, wp_ref, cols):
    return jnp.dot(h, _unpack_rows(wp_ref[:, cols]), preferred_element_type=F32).T


def _norm_rope_t(t, gain, cos, sin):
    r = lax.rsqrt(jnp.mean(t * t, axis=0, keepdims=True) + NORM_EPS)
    tr = t * r
    tn = jnp.concatenate([tr[:, l:l + LANE_TILE] * gain
                          for l in range(0, t.shape[1], LANE_TILE)], axis=1)
    x1 = tn[: HEAD_DIM // 2]
    x2 = tn[HEAD_DIM // 2:]
    return jnp.concatenate([x1 * cos - x2 * sin, x2 * cos + x1 * sin], axis=0)


def _qkv0_kernel(x_ref, g_ref, w_ref, qg_ref, kg_ref, cos_ref, sin_ref,
                 wgu_ref, wd_ref, wo0_ref, wo1_ref,
                 qt_ref, k_ref, vt_ref, wgu_bf_ref, wd_bf_ref, wo0_bf_ref, wo1_bf_ref,
                 wp_ref, res_ref):
    d_model = x_ref.shape[-1]

    @pl.when((pl.program_id(0) == 0) & (pl.program_id(1) == 0))
    def _():
        _pack_weight(w_ref, wp_ref)

    @pl.when(pl.program_id(1) == 0)
    def _():
        wd_bf_ref[...] = wd_ref[...].astype(BF16)

    wgu_bf_ref[...] = wgu_ref[...].astype(BF16)
    wo0_bf_ref[...] = wo0_ref[...].astype(BF16)
    wo1_bf_ref[...] = wo1_ref[...].astype(BF16)

    x = x_ref[0]
    h = (x * _rms_scale(x) * g_ref[...]).astype(BF16)
    cos = cos_ref[0]
    sin = sin_ref[0]
    n_chunks = d_model // FEATURE_CHUNK

    def project(c):
        res_ref[c % 2] = _project_t(h, wp_ref, slice(c * FEATURE_CHUNK, (c + 1) * FEATURE_CHUNK))

    project(0)
    for c in range(3 * n_chunks):
        if c + 1 < 3 * n_chunks:
            project(c + 1)
        res = res_ref.at[c % 2]
        kind, cc = divmod(c, n_chunks)
        out_rows = slice(cc * FEATURE_CHUNK, (cc + 1) * FEATURE_CHUNK)
        half_rows = slice(cc * FEATURE_CHUNK // 2, (cc + 1) * FEATURE_CHUNK // 2)
        if kind == 2:
            vt_ref[0, half_rows, :] = _pack_rows(res[...].astype(BF16))
            continue
        gain_ref = qg_ref if kind == 0 else kg_ref
        parts = []
        for g in range(FEATURE_CHUNK // HEAD_DIM):
            half = g % 2
            gain = gain_ref[half * HEAD_DIM:(half + 1) * HEAD_DIM, :]
            parts.append(_norm_rope_t(res[g * HEAD_DIM:(g + 1) * HEAD_DIM, :], gain, cos, sin))
        out = jnp.concatenate(parts, axis=0)
        if kind == 0:
            qt_ref[0, out_rows, :] = out.astype(BF16)
        else:
            k_ref[0, :, out_rows] = _pack_rows(out.T.astype(BF16))


def _qkv0(x, g, w, qg, kg, cos_t, sin_t, wgu, wd, wo0, wo1):
    b, s, d = x.shape
    tm = ROW_TILE
    nt = s // tm
    grid = (b, nt)

    def per_step(a):
        return pl.BlockSpec((a.shape[0] // (b * nt), a.shape[1]), lambda i, j: (i * nt + j, 0))

    def per_batch(a):
        return pl.BlockSpec((a.shape[0] // b, a.shape[1]), lambda i, j: (i, 0))

    def as_bf16(a):
        return jax.ShapeDtypeStruct(a.shape, BF16)
    feat = pl.BlockSpec((1, d, tm), lambda i, j: (i, 0, j))
    feat_packed = pl.BlockSpec((1, d // 2, tm), lambda i, j: (i, 0, j))
    tok = pl.BlockSpec((1, tm, d), lambda i, j: (i, j, 0))
    tok_packed = pl.BlockSpec((1, tm // 2, d), lambda i, j: (i, j, 0))
    rope = pl.BlockSpec((1, HEAD_DIM // 2, tm), lambda i, j: (i, 0, j))
    return pl.pallas_call(
        _qkv0_kernel,
        grid=grid,
        in_specs=[tok, _resident((1, d)), _resident(w.shape), _resident(qg.shape),
                  _resident(kg.shape), rope, rope,
                  per_step(wgu), per_batch(wd), per_step(wo0), per_step(wo1)],
        out_specs=[feat, tok_packed, feat_packed,
                   per_step(wgu), per_batch(wd), per_step(wo0), per_step(wo1)],
        out_shape=[jax.ShapeDtypeStruct((b, d, s), BF16),
                   jax.ShapeDtypeStruct((b, s // 2, d), jnp.uint32),
                   jax.ShapeDtypeStruct((b, d // 2, s), jnp.uint32),
                   as_bf16(wgu), as_bf16(wd), as_bf16(wo0), as_bf16(wo1)],
        scratch_shapes=[pltpu.VMEM((d // 2, w.shape[1]), jnp.uint32),
                        pltpu.VMEM((2, FEATURE_CHUNK, tm), F32)],
        compiler_params=pltpu.CompilerParams(
            dimension_semantics=("arbitrary", "arbitrary"),
            vmem_limit_bytes=V7X_VMEM_LIMIT_BYTES),
        name="qkv0_proj",
    )(x, g, w, qg, kg, cos_t, sin_t, wgu, wd, wo0, wo1)


def _diff_attn_kernel(lam_ref, qt_ref, k_ref, vt_ref, sg_ref, ot_ref,
                      s_ref, acc_ref, qz_ref, m_ref, *, lambda_init):
    bq = qt_ref.shape[-1]
    hd2 = 2 * HEAD_DIM
    n_heads = qt_ref.shape[1] // hd2
    i = pl.program_id(1)

    lp = lam_ref[...]
    lam = (jnp.exp(jnp.sum(lp[0:1] * lp[1:2], axis=-1, keepdims=True))
           - jnp.exp(jnp.sum(lp[2:3] * lp[3:4], axis=-1, keepdims=True)) + lambda_init)

    zero = jnp.zeros((HEAD_DIM, bq), BF16)
    for h in range(n_heads):
        q0 = qt_ref[0, h * hd2:h * hd2 + HEAD_DIM, :]
        q1 = qt_ref[0, h * hd2 + HEAD_DIM:(h + 1) * hd2, :]
        qz_ref[2 * h] = jnp.concatenate([q0, zero], axis=0)
        qz_ref[2 * h + 1] = jnp.concatenate([zero, q1], axis=0)

    def block(k0, bk, first):
        ones = jnp.ones((ONES_ROWS, bk), BF16)
        k0_packed = pl.multiple_of(lax.shift_right_logical(k0, 1), bk // 2)
        if first:
            kidx = lax.broadcasted_iota(jnp.int32, (bk, bq), 0)
            qidx = lax.broadcasted_iota(jnp.int32, (bk, bq), 1)
            keep = kidx <= qidx + (bk - bq)

        def scores(hc):
            h = hc // 2
            kblk = _unpack_rows(k_ref[0, pl.ds(k0_packed, bk // 2), h * hd2:(h + 1) * hd2])
            return jnp.dot(kblk, qz_ref[hc], preferred_element_type=F32)

        n_slots = s_ref.shape[0]

        def park(hc):
            s_ref[hc % n_slots, :bk, :] = scores(hc)

        for hc in range(LOOKAHEAD):
            park(hc)
        for hc in range(2 * n_heads):
            h = hc // 2
            if hc + LOOKAHEAD < 2 * n_heads:
                park(hc + LOOKAHEAD)
            s = s_ref[hc % n_slots, :bk, :]
            vblk = _unpack_rows(vt_ref[0, h * HEAD_DIM:(h + 1) * HEAD_DIM, pl.ds(k0, bk)])
            vext = jnp.concatenate([vblk, ones], axis=0)
            if first:
                s = jnp.where(keep, s, NEG_INF)
                m_new = jnp.max(s, axis=0, keepdims=True)
                p = jnp.exp2(s - m_new)
                acc_ref[hc] = jnp.dot(vext, p.astype(BF16), preferred_element_type=F32)
            else:
                m_old = m_ref[hc]
                m_new = jnp.maximum(m_old, jnp.max(s, axis=0, keepdims=True))
                alpha = jnp.exp2(m_old - m_new)
                p = jnp.exp2(s - m_new)
                acc_ref[hc] = alpha * acc_ref[hc] + jnp.dot(
                    vext, p.astype(BF16), preferred_element_type=F32)
            m_ref[hc] = m_new

    @pl.when((i & 1) == 0)
    def _():
        block(pl.multiple_of(i * bq, bq), bq, True)

    @pl.when((i & 1) == 1)
    def _():
        block(pl.multiple_of((i - 1) * bq, 2 * bq), 2 * bq, True)

    def body(j, carry):
        block(pl.multiple_of(j * 2 * bq, 2 * bq), 2 * bq, False)
        return carry

    lax.fori_loop(0, lax.shift_right_logical(i, 1), body, 0)

    for h in range(n_heads):
        a0 = acc_ref[2 * h]
        a1 = acc_ref[2 * h + 1]
        inv0 = 1.0 / a0[hd2:hd2 + 1]
        inv1 = lam / a1[hd2:hd2 + 1]
        o = a0[:hd2] * inv0 - a1[:hd2] * inv1
        r = lax.rsqrt(jnp.mean(o * o, axis=0, keepdims=True) + NORM_EPS)
        ot_ref[0, h * hd2:(h + 1) * hd2, :] = (
            o * r * sg_ref[...]).astype(BF16)


def _diff_attn(lam_p, qt, k, vt, sg, lambda_init):
    b, d, s = qt.shape
    hd2 = 2 * HEAD_DIM
    bq = ATTN_BLOCK
    n_half = d // HEAD_DIM
    return pl.pallas_call(
        functools.partial(_diff_attn_kernel, lambda_init=lambda_init),
        grid=(b, s // bq),
        in_specs=[_resident(lam_p.shape),
                  pl.BlockSpec((1, d, bq), lambda bi, i: (bi, 0, i)),
                  pl.BlockSpec((1, s // 2, d), lambda bi, i: (bi, 0, 0)),
                  pl.BlockSpec((1, d // 2, s), lambda bi, i: (bi, 0, 0)),
                  _resident(sg.shape)],
        out_specs=pl.BlockSpec((1, d, bq), lambda bi, i: (bi, 0, i)),
        out_shape=jax.ShapeDtypeStruct((b, d, s), BF16),
        scratch_shapes=[pltpu.VMEM((LOOKAHEAD + 1, 2 * bq, bq), F32),
                        pltpu.VMEM((n_half, hd2 + ONES_ROWS, bq), F32),
                        pltpu.VMEM((n_half, hd2, bq), BF16),
                        pltpu.VMEM((n_half, 1, bq), F32)],
        compiler_params=pltpu.CompilerParams(
            dimension_semantics=("parallel", "arbitrary"),
            vmem_limit_bytes=V7X_VMEM_LIMIT_BYTES),
        name="diff_attn",
    )(lam_p, qt, k, vt, sg)


def _proj_ffn_kernel(x_ref, at_ref, wo_ref, g_ref, wg_ref, wu_ref, wd_ref, o_ref, hid_ref):
    x = x_ref[0] + lax.dot_general(at_ref[0], wo_ref[...], TN_DIMS,
                                   preferred_element_type=F32)
    h = (x * _rms_scale(x) * g_ref[...]).astype(BF16)
    n_chunks = hid_ref.shape[-1] // FFN_CHUNK
    for c in range(n_chunks):
        cols = slice(c * FFN_CHUNK, (c + 1) * FFN_CHUNK)
        gate = jnp.dot(h, wg_ref[:, cols], preferred_element_type=F32)
        up = jnp.dot(h, wu_ref[:, cols], preferred_element_type=F32)
        hid_ref[:, c * FFN_CHUNK:(c + 1) * FFN_CHUNK] = (
            gate * jax.nn.sigmoid(gate) * up).astype(BF16)
    o_ref[0] = x + jnp.dot(hid_ref[...], wd_ref[...], preferred_element_type=F32)


def _proj_ffn(x, at, wo, g, wgu, wd, layer):
    b, s, d = x.shape
    tm = ROW_TILE
    hidden = wd.shape[1]
    tok = pl.BlockSpec((1, tm, d), lambda i, j: (i, j, 0))
    feat = pl.BlockSpec((1, d, tm), lambda i, j: (i, 0, j))
    once = pl.Buffered(1)
    return pl.pallas_call(
        _proj_ffn_kernel,
        grid=(b, s // tm),
        in_specs=[tok, feat, _resident(wo.shape), _resident((1, d)),
                  pl.BlockSpec((None, d, hidden), lambda i, j: (layer, 0, 0), pipeline_mode=once),
                  pl.BlockSpec((None, d, hidden), lambda i, j: (layer, 0, 1), pipeline_mode=once),
                  pl.BlockSpec((None, hidden, d), lambda i, j: (layer, 0, 0), pipeline_mode=once)],
        out_specs=tok,
        out_shape=jax.ShapeDtypeStruct((b, s, d), F32),
        scratch_shapes=[pltpu.VMEM((tm, hidden), BF16)],
        compiler_params=pltpu.CompilerParams(
            dimension_semantics=("parallel", "parallel"),
            vmem_limit_bytes=V7X_VMEM_LIMIT_BYTES),
        name="proj_ffn",
    )(x, at, wo, g, wgu, wgu, wd)


def _qkv1_kernel(x_ref, ga_ref, gkv_ref, wq_ref, wkv_ref, qg_ref, kg_ref, cos_ref, sin_ref,
                 qt_ref, k_ref, vt_ref, wqp_ref, wkvp_ref, kv_ref, res_ref):
    d_model = x_ref.shape[-1]
    kv_dim = k_ref.shape[-1]

    @pl.when((pl.program_id(0) == 0) & (pl.program_id(1) == 0))
    def _():
        _pack_weight(wq_ref, wqp_ref)
        _pack_weight(wkv_ref, wkvp_ref)

    x = x_ref[0]
    xn = x * _rms_scale(x)
    h_a = (xn * ga_ref[...]).astype(BF16)
    h_kv = (xn * gkv_ref[...]).astype(BF16)
    cos = cos_ref[0]
    sin = sin_ref[0]

    n_chunks = d_model // FEATURE_CHUNK

    def project_q(c):
        res_ref[c % 2] = _project_t(h_a, wqp_ref,
                                    slice(c * FEATURE_CHUNK, (c + 1) * FEATURE_CHUNK))

    kv_ref[...] = _project_t(h_kv, wkvp_ref, slice(0, 2 * kv_dim))
    project_q(0)
    kparts = [_norm_rope_t(kv_ref[g * HEAD_DIM:(g + 1) * HEAD_DIM, :], kg_ref[...], cos, sin)
              for g in range(kv_dim // HEAD_DIM)]
    k_ref[0] = _pack_rows(jnp.concatenate(kparts, axis=0).T.astype(BF16))
    vt_ref[0] = _pack_rows(kv_ref[kv_dim:, :].astype(BF16))

    for c in range(n_chunks):
        if c + 1 < n_chunks:
            project_q(c + 1)
        res = res_ref.at[c % 2]
        rows = slice(c * FEATURE_CHUNK, (c + 1) * FEATURE_CHUNK)
        parts = [_norm_rope_t(res[g * HEAD_DIM:(g + 1) * HEAD_DIM, :], qg_ref[...], cos, sin)
                 for g in range(FEATURE_CHUNK // HEAD_DIM)]
        qt_ref[0, rows, :] = jnp.concatenate(parts, axis=0).astype(BF16)


def _qkv1(x, ga, gkv, wq, wkv, qg, kg, cos_t, sin_t):
    b, s, d = x.shape
    tm = ROW_TILE
    kv_dim = wkv.shape[1] // 2
    tok = pl.BlockSpec((1, tm, d), lambda i, j: (i, j, 0))
    rope = pl.BlockSpec((1, HEAD_DIM // 2, tm), lambda i, j: (i, 0, j))
    return pl.pallas_call(
        _qkv1_kernel,
        grid=(b, s // tm),
        in_specs=[tok, _resident((1, d)), _resident((1, d)), _resident(wq.shape),
                  _resident(wkv.shape), _resident(qg.shape), _resident(kg.shape), rope, rope],
        out_specs=[pl.BlockSpec((1, d, tm), lambda i, j: (i, 0, j)),
                   pl.BlockSpec((1, tm // 2, kv_dim), lambda i, j: (i, j, 0)),
                   pl.BlockSpec((1, kv_dim // 2, tm), lambda i, j: (i, 0, j))],
        out_shape=[jax.ShapeDtypeStruct((b, d, s), BF16),
                   jax.ShapeDtypeStruct((b, s // 2, kv_dim), jnp.uint32),
                   jax.ShapeDtypeStruct((b, kv_dim // 2, s), jnp.uint32)],
        scratch_shapes=[pltpu.VMEM((d // 2, wq.shape[1]), jnp.uint32),
                        pltpu.VMEM((d // 2, wkv.shape[1]), jnp.uint32),
                        pltpu.VMEM((2 * kv_dim, tm), F32),
                        pltpu.VMEM((2, FEATURE_CHUNK, tm), F32)],
        compiler_params=pltpu.CompilerParams(
            dimension_semantics=("arbitrary", "arbitrary"),
            vmem_limit_bytes=V7X_VMEM_LIMIT_BYTES),
        name="qkv1_proj",
    )(x, ga, gkv, wq, wkv, qg, kg, cos_t, sin_t)


def _swa_kernel(qt_ref, k_ref, vt_ref, sink_ref, ot_ref, bias_ref, s_ref):
    w = WINDOW
    s_len = qt_ref.shape[-1]
    kv_dim = k_ref.shape[-1]
    gw = SW_PACK * w

    n_kv = kv_dim // HEAD_DIM
    n_blocks = s_len // w

    kidx = lax.broadcasted_iota(jnp.int32, (2 * w, gw), 0)
    qidx = lax.broadcasted_iota(jnp.int32, (2 * w, gw), 1) & (w - 1)
    bias_ref[...] = jnp.where((kidx > qidx) & (kidx <= qidx + w), 0.0, NEG_INF).astype(F32)

    def window(n, first):
        nk = w if first else 2 * w
        k0 = 0 if first else pl.multiple_of((n - 1) * w, w)
        q0 = 0 if first else pl.multiple_of(n * w, w)
        return nk, k0, q0

    def heads(j, part):
        return [SW_GROUP * j + SW_PACK * part + u for u in range(SW_PACK)]

    def scores(n, j, part, first):
        nk, k0, q0 = window(n, first)
        k0_packed = k0 if first else pl.multiple_of((n - 1) * (w // 2), w // 2)
        kwin = _unpack_rows(k_ref[0, pl.ds(k0_packed, nk // 2), :])
        qcat = jnp.concatenate(
            [qt_ref[0, h * HEAD_DIM:(h + 1) * HEAD_DIM, pl.ds(q0, w)]
             for h in heads(j, part)], axis=1)
        pieces = []
        if j > 0:
            pieces.append(jnp.zeros((j * HEAD_DIM, gw), BF16))
        pieces.append(qcat)
        if j + 1 < n_kv:
            pieces.append(jnp.zeros(((n_kv - j - 1) * HEAD_DIM, gw), BF16))
        qz = jnp.concatenate(pieces, axis=0) if len(pieces) > 1 else qcat
        return jnp.dot(kwin, qz, preferred_element_type=F32)

    def finish(n, j, part, first, s):
        nk, k0, q0 = window(n, first)
        s = s + bias_ref[2 * w - nk:, :]
        sink = sink_ref[j:j + 1, part * gw:(part + 1) * gw]
        m = jnp.maximum(jnp.max(s, axis=0, keepdims=True), sink)
        e = jnp.exp2(s - m).astype(BF16)
        vwin = _unpack_rows(
            vt_ref[0, j * HEAD_DIM // 2:(j + 1) * HEAD_DIM // 2, pl.ds(k0, nk)])
        vext = jnp.concatenate([vwin, jnp.ones((ONES_ROWS, nk), BF16)], axis=0)
        o = jnp.dot(vext, e, preferred_element_type=F32)
        den = o[HEAD_DIM:HEAD_DIM + 1] + jnp.exp2(sink - m)
        o = o[:HEAD_DIM] * (1.0 / den)
        for u, h in enumerate(heads(j, part)):
            ot_ref[0, h * HEAD_DIM:(h + 1) * HEAD_DIM, pl.ds(q0, w)] = (
                o[:, u * w:(u + 1) * w].astype(BF16))

    def run(chains):
        n_slots = s_ref.shape[0]

        def park(t):
            nk = window(chains[t][0], chains[t][3])[0]
            s_ref[t % n_slots, :nk, :] = scores(*chains[t])

        for t in range(min(LOOKAHEAD, len(chains))):
            park(t)
        for t, ch in enumerate(chains):
            if t + LOOKAHEAD < len(chains):
                park(t + LOOKAHEAD)
            nk = window(ch[0], ch[3])[0]
            finish(*ch, s_ref[t % n_slots, :nk, :])

    parts = range(SW_GROUP // SW_PACK)
    run([(0, j, part, True) for j in range(n_kv) for part in parts])

    def body(t, carry):
        n0 = 1 + t * SWA_UNROLL
        run([(n0 + u, j, part, False)
             for u in range(SWA_UNROLL) for j in range(n_kv) for part in parts])
        return carry

    lax.fori_loop(0, (n_blocks - 1) // SWA_UNROLL, body, 0)


def _swa(qt, k, vt, sink_rows):
    b, d, s = qt.shape
    kv_dim = k.shape[-1]
    assert (s // WINDOW - 1) % SWA_UNROLL == 0
    return pl.pallas_call(
        _swa_kernel,
        grid=(b,),
        in_specs=[pl.BlockSpec((1, d, s), lambda i: (i, 0, 0)),
                  pl.BlockSpec((1, s // 2, kv_dim), lambda i: (i, 0, 0)),
                  pl.BlockSpec((1, kv_dim // 2, s), lambda i: (i, 0, 0)),
                  _resident(sink_rows.shape)],
        out_specs=pl.BlockSpec((1, d, s), lambda i: (i, 0, 0)),
        out_shape=jax.ShapeDtypeStruct((b, d, s), BF16),
        scratch_shapes=[pltpu.VMEM((2 * WINDOW, SW_PACK * WINDOW), F32),
                        pltpu.VMEM((LOOKAHEAD + 1, 2 * WINDOW, SW_PACK * WINDOW), F32)],
        compiler_params=pltpu.CompilerParams(
            dimension_semantics=("parallel",),
            vmem_limit_bytes=V7X_VMEM_LIMIT_BYTES),
        name="swa_attn",
    )(qt, k, vt, sink_rows)


def _lane_bcast(col, n):
    return jnp.broadcast_to(col.astype(F32).reshape(-1, 1), (col.size, n))


def kernel(x, positions, attn_norm, ffn_norm, w_gate_up, w_down, da_w_qkv, da_q_norm, da_k_norm,
           da_lambda, da_subln, da_w_o, kv_norm, w_kv, k_norm, sw_w_q, sw_q_norm, sw_sinks, sw_w_o):
    b, s, d = x.shape
    scale = 1.0 / math.sqrt(HEAD_DIM)

    inv = 1.0 / (ROPE_THETA ** (jnp.arange(0, HEAD_DIM, 2, dtype=F32) / HEAD_DIM))
    ang_t = positions.astype(F32)[:, None, :] * inv[None, :, None]
    cos_t = jnp.cos(ang_t)
    sin_t = jnp.sin(ang_t)

    lambda_init = 0.8 - 0.6 * math.exp(-0.3 * 0)
    n_layers, _, two_h = w_gate_up.shape
    qt, k, vt, wgu, wd, wo0, wo1 = _qkv0(
        x, attn_norm[0].reshape(1, d), da_w_qkv[0],
        _lane_bcast(da_q_norm[0] * (scale * LOG2_E), LANE_TILE),
        _lane_bcast(da_k_norm[0], LANE_TILE), cos_t, sin_t,
        w_gate_up.reshape(n_layers * d, two_h), w_down.reshape(n_layers * (two_h // 2), d),
        da_w_o[0], sw_w_o[0])
    wgu = wgu.reshape(n_layers, d, two_h)
    wd = wd.reshape(n_layers, two_h // 2, d)
    at = _diff_attn(da_lambda[0].astype(F32), qt, k, vt,
                    _lane_bcast(da_subln[0] * (1.0 - lambda_init), ATTN_BLOCK), lambda_init)
    x = _proj_ffn(x, at, wo0, ffn_norm[0].reshape(1, d), wgu, wd, 0)

    qt, k, vt = _qkv1(x, attn_norm[1].reshape(1, d), kv_norm.reshape(1, d),
                      sw_w_q[0], w_kv,
                      _lane_bcast(sw_q_norm[0] * (scale * LOG2_E), LANE_TILE),
                      _lane_bcast(k_norm, LANE_TILE), cos_t, sin_t)
    sink_rows = jnp.repeat((sw_sinks[0].astype(F32) * LOG2_E).reshape(SW_KV_HEADS, SW_GROUP),
                           WINDOW, axis=1)
    at = _swa(qt, k, vt, sink_rows)
    x = _proj_ffn(x, at, wo1, ffn_norm[1].reshape(1, d), wgu, wd, 1)
    return x
```

```python
import functools
import math

import jax
import jax.numpy as jnp
from jax import lax
from jax.experimental import pallas as pl
from jax.experimental.pallas import tpu as pltpu

HEAD_DIM = 64
ROPE_THETA = 10000.0
NORM_EPS = 1e-6
NEG_INF = -1e30
LOG2_E = math.log2(math.e)
WINDOW = 128
SW_KV_HEADS = 4
SW_GROUP = 4
SW_PACK = 2

F32 = jnp.float32
BF16 = jnp.bfloat16

V7X_VMEM_LIMIT_BYTES = 56 * 1024 * 1024

ROW_TILE = 1024
FEATURE_CHUNK = 512
FFN_CHUNK = 256
ATTN_BLOCK = 256
LANE_TILE = 128
ONES_ROWS = 16
LOOKAHEAD = 6
SWA_UNROLL = 5

TN_DIMS = (((0,), (0,)), ((), ()))


def _resident(shape):
    nd = len(shape)
    return pl.BlockSpec(shape, lambda *_: (0,) * nd, pipeline_mode=pl.Buffered(1))


def _rms_scale(x):
    return lax.rsqrt(jnp.mean(x * x, axis=-1, keepdims=True) + NORM_EPS)


def _pack_rows(x):
    return pltpu.bitcast(x, jnp.uint32)


def _unpack_rows(x):
    return pltpu.bitcast(x, BF16)


def _pack_weight(w_ref, wp_ref):
    for c in range(w_ref.shape[1] // FEATURE_CHUNK):
        cols = slice(c * FEATURE_CHUNK, (c + 1) * FEATURE_CHUNK)
        wp_ref[:, cols] = _pack_rows(w_ref[:, cols].astype(BF16))


def _project_t(h, wp_ref, cols):
    return jnp.dot(h, _unpack_rows(wp_ref[:, cols]), preferred_element_type=F32).T


def _norm_rope_t(t, gain, cos, sin):
    r = lax.rsqrt(jnp.mean(t * t, axis=0, keepdims=True) + NORM_EPS)
    tr = t * r
    tn = jnp.concatenate([tr[:, l:l + LANE_TILE] * gain
                          for l in range(0, t.shape[1], LANE_TILE)], axis=1)
    x1 = tn[: HEAD_DIM // 2]
    x2 = tn[HEAD_DIM // 2:]
    return jnp.concatenate([x1 * cos - x2 * sin, x2 * cos + x1 * sin], axis=0)


def _qkv0_kernel(x_ref, g_ref, w_ref, qg_ref, kg_ref, cos_ref, sin_ref,
                 wgu_ref, wd_ref, wo0_ref, wo1_ref,
                 qt_ref, k_ref, vt_ref, wgu_bf_ref, wd_bf_ref, wo0_bf_ref, wo1_bf_ref,
                 wp_ref, res_ref):
    d_model = x_ref.shape[-1]

    @pl.when((pl.program_id(0) == 0) & (pl.program_id(1) == 0))
    def _():
        _pack_weight(w_ref, wp_ref)

    @pl.when(pl.program_id(1) == 0)
    def _():
        wd_bf_ref[...] = wd_ref[...].astype(BF16)

    wgu_bf_ref[...] = wgu_ref[...].astype(BF16)
    wo0_bf_ref[...] = wo0_ref[...].astype(BF16)
    wo1_bf_ref[...] = wo1_ref[...].astype(BF16)

    x = x_ref[0]
    h = (x * _rms_scale(x) * g_ref[...]).astype(BF16)
    cos = cos_ref[0]
    sin = sin_ref[0]
    n_chunks = d_model // FEATURE_CHUNK

    def project(c):
        res_ref[c % 2] = _project_t(h, wp_ref, slice(c * FEATURE_CHUNK, (c + 1) * FEATURE_CHUNK))

    project(0)
    for c in range(3 * n_chunks):
        if c + 1 < 3 * n_chunks:
            project(c + 1)
        res = res_ref.at[c % 2]
        kind, cc = divmod(c, n_chunks)
        out_rows = slice(cc * FEATURE_CHUNK, (cc + 1) * FEATURE_CHUNK)
        half_rows = slice(cc * FEATURE_CHUNK // 2, (cc + 1) * FEATURE_CHUNK // 2)
        if kind == 2:
            vt_ref[0, half_rows, :] = _pack_rows(res[...].astype(BF16))
            continue
        gain_ref = qg_ref if kind == 0 else kg_ref
        parts = []
        for g in range(FEATURE_CHUNK // HEAD_DIM):
            half = g % 2
            gain = gain_ref[half * HEAD_DIM:(half + 1) * HEAD_DIM, :]
            parts.append(_norm_rope_t(res[g * HEAD_DIM:(g + 1) * HEAD_DIM, :], gain, cos, sin))
        out = jnp.concatenate(parts, axis=0)
        if kind == 0:
            qt_ref[0, out_rows, :] = out.astype(BF16)
        else:
            k_ref[0, :, out_rows] = _pack_rows(out.T.astype(BF16))


def _qkv0(x, g, w, qg, kg, cos_t, sin_t, wgu, wd, wo0, wo1):
    b, s, d = x.shape
    tm = ROW_TILE
    nt = s // tm
    grid = (b, nt)

    def per_step(a):
        return pl.BlockSpec((a.shape[0] // (b * nt), a.shape[1]), lambda i, j: (i * nt + j, 0))

    def per_batch(a):
        return pl.BlockSpec((a.shape[0] // b, a.shape[1]), lambda i, j: (i, 0))

    def as_bf16(a):
        return jax.ShapeDtypeStruct(a.shape, BF16)
    feat = pl.BlockSpec((1, d, tm), lambda i, j: (i, 0, j))
    feat_packed = pl.BlockSpec((1, d // 2, tm), lambda i, j: (i, 0, j))
    tok = pl.BlockSpec((1, tm, d), lambda i, j: (i, j, 0))
    tok_packed = pl.BlockSpec((1, tm // 2, d), lambda i, j: (i, j, 0))
    rope = pl.BlockSpec((1, HEAD_DIM // 2, tm), lambda i, j: (i, 0, j))
    return pl.pallas_call(
        _qkv0_kernel,
        grid=grid,
        in_specs=[tok, _resident((1, d)), _resident(w.shape), _resident(qg.shape),
                  _resident(kg.shape), rope, rope,
                  per_step(wgu), per_batch(wd), per_step(wo0), per_step(wo1)],
        out_specs=[feat, tok_packed, feat_packed,
                   per_step(wgu), per_batch(wd), per_step(wo0), per_step(wo1)],
        out_shape=[jax.ShapeDtypeStruct((b, d, s), BF16),
                   jax.ShapeDtypeStruct((b, s // 2, d), jnp.uint32),
                   jax.ShapeDtypeStruct((b, d // 2, s), jnp.uint32),
                   as_bf16(wgu), as_bf16(wd), as_bf16(wo0), as_bf16(wo1)],
        scratch_shapes=[pltpu.VMEM((d // 2, w.shape[1]), jnp.uint32),
                        pltpu.VMEM((2, FEATURE_CHUNK, tm), F32)],
        compiler_params=pltpu.CompilerParams(
            dimension_semantics=("arbitrary", "arbitrary"),
            vmem_limit_bytes=V7X_VMEM_LIMIT_BYTES),
        name="qkv0_proj",
    )(x, g, w, qg, kg, cos_t, sin_t, wgu, wd, wo0, wo1)


def _diff_attn_kernel(lam_ref, qt_ref, k_ref, vt_ref, sg_ref, ot_ref,
                      qz_ref, m_ref, acc_ref, s_ref, *, lambda_init):
    bq = qt_ref.shape[-1]
    hd2 = 2 * HEAD_DIM
    n_heads = qt_ref.shape[1] // hd2
    i = pl.program_id(1)

    lp = lam_ref[...]
    lam = (jnp.exp(jnp.sum(lp[0:1] * lp[1:2], axis=-1, keepdims=True))
           - jnp.exp(jnp.sum(lp[2:3] * lp[3:4], axis=-1, keepdims=True)) + lambda_init)

    zero = jnp.zeros((HEAD_DIM, bq), BF16)
    for h in range(n_heads):
        q0 = qt_ref[0, h * hd2:h * hd2 + HEAD_DIM, :]
        q1 = qt_ref[0, h * hd2 + HEAD_DIM:(h + 1) * hd2, :]
        qz_ref[2 * h] = jnp.concatenate([q0, zero], axis=0)
        qz_ref[2 * h + 1] = jnp.concatenate([zero, q1], axis=0)

    def block(k0, bk, first):
        ones = jnp.ones((ONES_ROWS, bk), BF16)
        k0_packed = pl.multiple_of(lax.shift_right_logical(k0, 1), bk // 2)
        if first:
            kidx = lax.broadcasted_iota(jnp.int32, (bk, bq), 0)
            qidx = lax.broadcasted_iota(jnp.int32, (bk, bq), 1)
            keep = kidx <= qidx + (bk - bq)

        def scores(hc):
            h = hc // 2
            kblk = _unpack_rows(k_ref[0, pl.ds(k0_packed, bk // 2), h * hd2:(h + 1) * hd2])
            return jnp.dot(kblk, qz_ref[hc], preferred_element_type=F32)

        n_slots = s_ref.shape[0]

        def park(hc):
            s_ref[hc % n_slots, :bk, :] = scores(hc)

        for hc in range(LOOKAHEAD):
            park(hc)
        for hc in range(2 * n_heads):
            h = hc // 2
            if hc + LOOKAHEAD < 2 * n_heads:
                park(hc + LOOKAHEAD)
            s = s_ref[hc % n_slots, :bk, :]
            vblk = _unpack_rows(vt_ref[0, h * HEAD_DIM:(h + 1) * HEAD_DIM, pl.ds(k0, bk)])
            vext = jnp.concatenate([vblk, ones], axis=0)
            if first:
                s = jnp.where(keep, s, NEG_INF)
                m_new = jnp.max(s, axis=0, keepdims=True)
                p = jnp.exp2(s - m_new)
                acc_ref[hc] = jnp.dot(vext, p.astype(BF16), preferred_element_type=F32)
            else:
                m_old = m_ref[hc]
                m_new = jnp.maximum(m_old, jnp.max(s, axis=0, keepdims=True))
                alpha = jnp.exp2(m_old - m_new)
                p = jnp.exp2(s - m_new)
                acc_ref[hc] = alpha * acc_ref[hc] + jnp.dot(
                    vext, p.astype(BF16), preferred_element_type=F32)
            m_ref[hc] = m_new

    @pl.when((i & 1) == 0)
    def _():
        block(pl.multiple_of(i * bq, bq), bq, True)

    @pl.when((i & 1) == 1)
    def _():
        block(pl.multiple_of((i - 1) * bq, 2 * bq), 2 * bq, True)

    def body(j, carry):
        block(pl.multiple_of(j * 2 * bq, 2 * bq), 2 * bq, False)
        return carry

    lax.fori_loop(0, lax.shift_right_logical(i, 1), body, 0)

    for h in range(n_heads):
        a0 = acc_ref[2 * h]
        a1 = acc_ref[2 * h + 1]
        inv0 = 1.0 / a0[hd2:hd2 + 1]
        inv1 = lam / a1[hd2:hd2 + 1]
        o = a0[:hd2] * inv0 - a1[:hd2] * inv1
        r = lax.rsqrt(jnp.mean(o * o, axis=0, keepdims=True) + NORM_EPS)
        ot_ref[0, h * hd2:(h + 1) * hd2, :] = (
            o * r * sg_ref[...]).astype(BF16)


def _diff_attn(lam_p, qt, k, vt, sg, lambda_init):
    b, d, s = qt.shape
    hd2 = 2 * HEAD_DIM
    bq = ATTN_BLOCK
    n_half = d // HEAD_DIM
    return pl.pallas_call(
        functools.partial(_diff_attn_kernel, lambda_init=lambda_init),
        grid=(b, s // bq),
        in_specs=[_resident(lam_p.shape),
                  pl.BlockSpec((1, d, bq), lambda bi, i: (bi, 0, i)),
                  pl.BlockSpec((1, s // 2, d), lambda bi, i: (bi, 0, 0)),
                  pl.BlockSpec((1, d // 2, s), lambda bi, i: (bi, 0, 0)),
                  _resident(sg.shape)],
        out_specs=pl.BlockSpec((1, d, bq), lambda bi, i: (bi, 0, i)),
        out_shape=jax.ShapeDtypeStruct((b, d, s), BF16),
        scratch_shapes=[pltpu.VMEM((n_half, hd2, bq), BF16),
                        pltpu.VMEM((n_half, 1, bq), F32),
                        pltpu.VMEM((n_half, hd2 + ONES_ROWS, bq), F32),
                        pltpu.VMEM((LOOKAHEAD + 1, 2 * bq, bq), F32)],
        compiler_params=pltpu.CompilerParams(
            dimension_semantics=("parallel", "arbitrary"),
            vmem_limit_bytes=V7X_VMEM_LIMIT_BYTES),
        name="diff_attn",
    )(lam_p, qt, k, vt, sg)


def _proj_ffn_kernel(x_ref, at_ref, wo_ref, g_ref, wg_ref, wu_ref, wd_ref, o_ref, hid_ref):
    x = x_ref[0] + lax.dot_general(at_ref[0], wo_ref[...], TN_DIMS,
                                   preferred_element_type=F32)
    h = (x * _rms_scale(x) * g_ref[...]).astype(BF16)
    n_chunks = hid_ref.shape[-1] // FFN_CHUNK
    for c in range(n_chunks):
        cols = slice(c * FFN_CHUNK, (c + 1) * FFN_CHUNK)
        gate = jnp.dot(h, wg_ref[:, cols], preferred_element_type=F32)
        up = jnp.dot(h, wu_ref[:, cols], preferred_element_type=F32)
        hid_ref[:, c * FFN_CHUNK:(c + 1) * FFN_CHUNK] = (
            gate * jax.nn.sigmoid(gate) * up).astype(BF16)
    o_ref[0] = x + jnp.dot(hid_ref[...], wd_ref[...], preferred_element_type=F32)


def _proj_ffn(x, at, wo, g, wgu, wd, layer):
    b, s, d = x.shape
    tm = ROW_TILE
    hidden = wd.shape[1]
    tok = pl.BlockSpec((1, tm, d), lambda i, j: (i, j, 0))
    feat = pl.BlockSpec((1, d, tm), lambda i, j: (i, 0, j))
    once = pl.Buffered(1)
    return pl.pallas_call(
        _proj_ffn_kernel,
        grid=(b, s // tm),
        in_specs=[tok, feat, _resident(wo.shape), _resident((1, d)),
                  pl.BlockSpec((None, d, hidden), lambda i, j: (layer, 0, 0), pipeline_mode=once),
                  pl.BlockSpec((None, d, hidden), lambda i, j: (layer, 0, 1), pipeline_mode=once),
                  pl.BlockSpec((None, hidden, d), lambda i, j: (layer, 0, 0), pipeline_mode=once)],
        out_specs=tok,
        out_shape=jax.ShapeDtypeStruct((b, s, d), F32),
        scratch_shapes=[pltpu.VMEM((tm, hidden), BF16)],
        compiler_params=pltpu.CompilerParams(
            dimension_semantics=("parallel", "parallel"),
            vmem_limit_bytes=V7X_VMEM_LIMIT_BYTES),
        name="proj_ffn",
    )(x, at, wo, g, wgu, wgu, wd)


def _qkv1_kernel(x_ref, ga_ref, gkv_ref, wq_ref, wkv_ref, qg_ref, kg_ref, cos_ref, sin_ref,
                 qt_ref, k_ref, vt_ref, wqp_ref, wkvp_ref, kv_ref, res_ref):
    d_model = x_ref.shape[-1]
    kv_dim = k_ref.shape[-1]

    @pl.when((pl.program_id(0) == 0) & (pl.program_id(1) == 0))
    def _():
        _pack_weight(wq_ref, wqp_ref)
        _pack_weight(wkv_ref, wkvp_ref)

    x = x_ref[0]
    xn = x * _rms_scale(x)
    h_a = (xn * ga_ref[...]).astype(BF16)
    h_kv = (xn * gkv_ref[...]).astype(BF16)
    cos = cos_ref[0]
    sin = sin_ref[0]

    n_chunks = d_model // FEATURE_CHUNK

    def project_q(c):
        res_ref[c % 2] = _project_t(h_a, wqp_ref,
                                    slice(c * FEATURE_CHUNK, (c + 1) * FEATURE_CHUNK))

    kv_ref[...] = _project_t(h_kv, wkvp_ref, slice(0, 2 * kv_dim))
    project_q(0)
    kparts = [_norm_rope_t(kv_ref[g * HEAD_DIM:(g + 1) * HEAD_DIM, :], kg_ref[...], cos, sin)
              for g in range(kv_dim // HEAD_DIM)]
    k_ref[0] = _pack_rows(jnp.concatenate(kparts, axis=0).T.astype(BF16))
    vt_ref[0] = _pack_rows(kv_ref[kv_dim:, :].astype(BF16))

    for c in range(n_chunks):
        if c + 1 < n_chunks:
            project_q(c + 1)
        res = res_ref.at[c % 2]
        rows = slice(c * FEATURE_CHUNK, (c + 1) * FEATURE_CHUNK)
        parts = [_norm_rope_t(res[g * HEAD_DIM:(g + 1) * HEAD_DIM, :], qg_ref[...], cos, sin)
                 for g in range(FEATURE_CHUNK // HEAD_DIM)]
        qt_ref[0, rows, :] = jnp.concatenate(parts, axis=0).astype(BF16)


def _qkv1(x, ga, gkv, wq, wkv, qg, kg, cos_t, sin_t):
    b, s, d = x.shape
    tm = ROW_TILE
    kv_dim = wkv.shape[1] // 2
    tok = pl.BlockSpec((1, tm, d), lambda i, j: (i, j, 0))
    rope = pl.BlockSpec((1, HEAD_DIM // 2, tm), lambda i, j: (i, 0, j))
    return pl.pallas_call(
        _qkv1_kernel,
        grid=(b, s // tm),
        in_specs=[tok, _resident((1, d)), _resident((1, d)), _resident(wq.shape),
                  _resident(wkv.shape), _resident(qg.shape), _resident(kg.shape), rope, rope],
        out_specs=[pl.BlockSpec((1, d, tm), lambda i, j: (i, 0, j)),
                   pl.BlockSpec((1, tm // 2, kv_dim), lambda i, j: (i, j, 0)),
                   pl.BlockSpec((1, kv_dim // 2, tm), lambda i, j: (i, 0, j))],
        out_shape=[jax.ShapeDtypeStruct((b, d, s), BF16),
                   jax.ShapeDtypeStruct((b, s // 2, kv_dim), jnp.uint32),
                   jax.ShapeDtypeStruct((b, kv_dim // 2, s), jnp.uint32)],
        scratch_shapes=[pltpu.VMEM((d // 2, wq.shape[1]), jnp.uint32),
                        pltpu.VMEM((d // 2, wkv.shape[1]), jnp.uint32),
                        pltpu.VMEM((2 * kv_dim, tm), F32),
                        pltpu.VMEM((2, FEATURE_CHUNK, tm), F32)],
        compiler_params=pltpu.CompilerParams(
            dimension_semantics=("arbitrary", "arbitrary"),
            vmem_limit_bytes=V7X_VMEM_LIMIT_BYTES),
        name="qkv1_proj",
    )(x, ga, gkv, wq, wkv, qg, kg, cos_t, sin_t)


def _swa_kernel(qt_ref, k_ref, vt_ref, sink_ref, ot_ref, bias_ref, s_ref):
    w = WINDOW
    s_len = qt_ref.shape[-1]
    kv_dim = k_ref.shape[-1]
    gw = SW_PACK * w

    n_kv = kv_dim // HEAD_DIM
    n_blocks = s_len // w

    kidx = lax.broadcasted_iota(jnp.int32, (2 * w, gw), 0)
    qidx = lax.broadcasted_iota(jnp.int32, (2 * w, gw), 1) & (w - 1)
    bias_ref[:2 * w, :] = jnp.where((kidx > qidx) & (kidx <= qidx + w), 0.0, NEG_INF).astype(F32)

    def window(n, first):
        nk = w if first else 2 * w
        k0 = 0 if first else pl.multiple_of((n - 1) * w, w)
        q0 = 0 if first else pl.multiple_of(n * w, w)
        return nk, k0, q0

    def heads(j, part):
        return [SW_GROUP * j + SW_PACK * part + u for u in range(SW_PACK)]

    def scores(n, j, part, first):
        nk, k0, q0 = window(n, first)
        k0_packed = k0 if first else pl.multiple_of((n - 1) * (w // 2), w // 2)
        kwin = _unpack_rows(k_ref[0, pl.ds(k0_packed, nk // 2), :])
        qcat = jnp.concatenate(
            [qt_ref[0, h * HEAD_DIM:(h + 1) * HEAD_DIM, pl.ds(q0, w)]
             for h in heads(j, part)], axis=1)
        pieces = []
        if j > 0:
            pieces.append(jnp.zeros((j * HEAD_DIM, gw), BF16))
        pieces.append(qcat)
        if j + 1 < n_kv:
            pieces.append(jnp.zeros(((n_kv - j - 1) * HEAD_DIM, gw), BF16))
        qz = jnp.concatenate(pieces, axis=0) if len(pieces) > 1 else qcat
        return jnp.dot(kwin, qz, preferred_element_type=F32)

    def finish(n, j, part, first, s):
        nk, k0, q0 = window(n, first)
        s = s + bias_ref[2 * w - nk:2 * w, :]
        sink = sink_ref[j:j + 1, part * gw:(part + 1) * gw]
        m = jnp.maximum(jnp.max(s, axis=0, keepdims=True), sink)
        e = jnp.exp2(s - m).astype(BF16)
        vwin = _unpack_rows(
            vt_ref[0, j * HEAD_DIM // 2:(j + 1) * HEAD_DIM // 2, pl.ds(k0, nk)])
        vext = jnp.concatenate([vwin, jnp.ones((ONES_ROWS, nk), BF16)], axis=0)
        o = jnp.dot(vext, e, preferred_element_type=F32)
        den = o[HEAD_DIM:HEAD_DIM + 1] + jnp.exp2(sink - m)
        o = o[:HEAD_DIM] * (1.0 / den)
        for u, h in enumerate(heads(j, part)):
            ot_ref[0, h * HEAD_DIM:(h + 1) * HEAD_DIM, pl.ds(q0, w)] = (
                o[:, u * w:(u + 1) * w].astype(BF16))

    def run(chains):
        n_slots = s_ref.shape[0]

        def park(t):
            nk = window(chains[t][0], chains[t][3])[0]
            s_ref[t % n_slots, :nk, :] = scores(*chains[t])

        for t in range(min(LOOKAHEAD, len(chains))):
            park(t)
        for t, ch in enumerate(chains):
            if t + LOOKAHEAD < len(chains):
                park(t + LOOKAHEAD)
            nk = window(ch[0], ch[3])[0]
            finish(*ch, s_ref[t % n_slots, :nk, :])

    parts = range(SW_GROUP // SW_PACK)
    run([(0, j, part, True) for j in range(n_kv) for part in parts])

    def body(t, carry):
        n0 = 1 + t * SWA_UNROLL
        run([(n0 + u, j, part, False)
             for u in range(SWA_UNROLL) for j in range(n_kv) for part in parts])
        return carry

    lax.fori_loop(0, (n_blocks - 1) // SWA_UNROLL, body, 0)


def _swa(qt, k, vt, sink_rows):
    b, d, s = qt.shape
    kv_dim = k.shape[-1]
    assert (s // WINDOW - 1) % SWA_UNROLL == 0
    return pl.pallas_call(
        _swa_kernel,
        grid=(b,),
        in_specs=[pl.BlockSpec((1, d, s), lambda i: (i, 0, 0)),
                  pl.BlockSpec((1, s // 2, kv_dim), lambda i: (i, 0, 0)),
                  pl.BlockSpec((1, kv_dim // 2, s), lambda i: (i, 0, 0)),
                  _resident(sink_rows.shape)],
        out_specs=pl.BlockSpec((1, d, s), lambda i: (i, 0, 0)),
        out_shape=jax.ShapeDtypeStruct((b, d, s), BF16),
        scratch_shapes=[pltpu.VMEM((2 * WINDOW + 8, SW_PACK * WINDOW), F32),
                        pltpu.VMEM((LOOKAHEAD + 1, 2 * WINDOW, SW_PACK * WINDOW), F32)],
        compiler_params=pltpu.CompilerParams(
            dimension_semantics=("parallel",),
            vmem_limit_bytes=V7X_VMEM_LIMIT_BYTES),
        name="swa_attn",
    )(qt, k, vt, sink_rows)


def _lane_bcast(col, n):
    return jnp.broadcast_to(col.astype(F32).reshape(-1, 1), (col.size, n))


def kernel(x, positions, attn_norm, ffn_norm, w_gate_up, w_down, da_w_qkv, da_q_norm, da_k_norm,
           da_lambda, da_subln, da_w_o, kv_norm, w_kv, k_norm, sw_w_q, sw_q_norm, sw_sinks, sw_w_o):
    b, s, d = x.shape
    scale = 1.0 / math.sqrt(HEAD_DIM)

    inv = 1.0 / (ROPE_THETA ** (jnp.arange(0, HEAD_DIM, 2, dtype=F32) / HEAD_DIM))
    ang_t = positions.astype(F32)[:, None, :] * inv[None, :, None]
    cos_t = jnp.cos(ang_t)
    sin_t = jnp.sin(ang_t)

    lambda_init = 0.8 - 0.6 * math.exp(-0.3 * 0)
    n_layers, _, two_h = w_gate_up.shape
    qt, k, vt, wgu, wd, wo0, wo1 = _qkv0(
        x, attn_norm[0].reshape(1, d), da_w_qkv[0],
        _lane_bcast(da_q_norm[0] * (scale * LOG2_E), LANE_TILE),
        _lane_bcast(da_k_norm[0], LANE_TILE), cos_t, sin_t,
        w_gate_up.reshape(n_layers * d, two_h), w_down.reshape(n_layers * (two_h // 2), d),
        da_w_o[0], sw_w_o[0])
    wgu = wgu.reshape(n_layers, d, two_h)
    wd = wd.reshape(n_layers, two_h // 2, d)
    at = _diff_attn(da_lambda[0].astype(F32), qt, k, vt,
                    _lane_bcast(da_subln[0] * (1.0 - lambda_init), ATTN_BLOCK), lambda_init)
    x = _proj_ffn(x, at, wo0, ffn_norm[0].reshape(1, d), wgu, wd, 0)

    qt, k, vt = _qkv1(x, attn_norm[1].reshape(1, d), kv_norm.reshape(1, d),
                      sw_w_q[0], w_kv,
                      _lane_bcast(sw_q_norm[0] * (scale * LOG2_E), LANE_TILE),
                      _lane_bcast(k_norm, LANE_TILE), cos_t, sin_t)
    sink_rows = jnp.repeat((sw_sinks[0].astype(F32) * LOG2_E).reshape(SW_KV_HEADS, SW_GROUP),
                           WINDOW, axis=1)
    at = _swa(qt, k, vt, sink_rows)
    x = _proj_ffn(x, at, wo1, ffn_norm[1].reshape(1, d), wgu, wd, 1)
    return x
```

```python
import functools
import math

import jax
import jax.numpy as jnp
from jax import lax
from jax.experimental import pallas as pl
from jax.experimental.pallas import tpu as pltpu

HEAD_DIM = 64
ROPE_THETA = 10000.0
NORM_EPS = 1e-6
NEG_INF = -1e30
LOG2_E = math.log2(math.e)
WINDOW = 128
SW_KV_HEADS = 4
SW_GROUP = 4
SW_PACK = 2

F32 = jnp.float32
BF16 = jnp.bfloat16

V7X_VMEM_LIMIT_BYTES = 56 * 1024 * 1024

ROW_TILE = 1024
FEATURE_CHUNK = 512
FFN_CHUNK = 256
ATTN_BLOCK = 256
LANE_TILE = 128
ONES_ROWS = 16
LOOKAHEAD = 6
SWA_UNROLL = 15

TN_DIMS = (((0,), (0,)), ((), ()))


def _resident(shape):
    nd = len(shape)
    return pl.BlockSpec(shape, lambda *_: (0,) * nd, pipeline_mode=pl.Buffered(1))


def _rms_scale(x):
    return lax.rsqrt(jnp.mean(x * x, axis=-1, keepdims=True) + NORM_EPS)


def _pack_rows(x):
    return pltpu.bitcast(x, jnp.uint32)


def _unpack_rows(x):
    return pltpu.bitcast(x, BF16)


def _pack_weight(w_ref, wp_ref):
    for c in range(w_ref.shape[1] // FEATURE_CHUNK):
        cols = slice(c * FEATURE_CHUNK, (c + 1) * FEATURE_CHUNK)
        wp_ref[:, cols] = _pack_rows(w_ref[:, cols].astype(BF16))


def _project_t(h, wp_ref, cols):
    return jnp.dot(h, _unpack_rows(wp_ref[:, cols]), preferred_element_type=F32).T


def _norm_rope_t(t, gain, cos, sin):
    r = lax.rsqrt(jnp.mean(t * t, axis=0, keepdims=True) + NORM_EPS)
    tr = t * r
    tn = jnp.concatenate([tr[:, l:l + LANE_TILE] * gain
                          for l in range(0, t.shape[1], LANE_TILE)], axis=1)
    x1 = tn[: HEAD_DIM // 2]
    x2 = tn[HEAD_DIM // 2:]
    return jnp.concatenate([x1 * cos - x2 * sin, x2 * cos + x1 * sin], axis=0)


def _qkv0_kernel(x_ref, g_ref, w_ref, qg_ref, kg_ref, cos_ref, sin_ref,
                 wgu_ref, wd_ref, wo0_ref, wo1_ref,
                 qt_ref, k_ref, vt_ref, wgu_bf_ref, wd_bf_ref, wo0_bf_ref, wo1_bf_ref,
                 wp_ref, res_ref):
    d_model = x_ref.shape[-1]

    @pl.when((pl.program_id(0) == 0) & (pl.program_id(1) == 0))
    def _():
        _pack_weight(w_ref, wp_ref)

    @pl.when(pl.program_id(1) == 0)
    def _():
        wd_bf_ref[...] = wd_ref[...].astype(BF16)

    wgu_bf_ref[...] = wgu_ref[...].astype(BF16)
    wo0_bf_ref[...] = wo0_ref[...].astype(BF16)
    wo1_bf_ref[...] = wo1_ref[...].astype(BF16)

    x = x_ref[0]
    h = (x * _rms_scale(x) * g_ref[...]).astype(BF16)
    cos = cos_ref[0]
    sin = sin_ref[0]
    n_chunks = d_model // FEATURE_CHUNK

    def project(c):
        res_ref[c % 2] = _project_t(h, wp_ref, slice(c * FEATURE_CHUNK, (c + 1) * FEATURE_CHUNK))

    project(0)
    for c in range(3 * n_chunks):
        if c + 1 < 3 * n_chunks:
            project(c + 1)
        res = res_ref.at[c % 2]
        kind, cc = divmod(c, n_chunks)
        out_rows = slice(cc * FEATURE_CHUNK, (cc + 1) * FEATURE_CHUNK)
        half_rows = slice(cc * FEATURE_CHUNK // 2, (cc + 1) * FEATURE_CHUNK // 2)
        if kind == 2:
            vt_ref[0, half_rows, :] = _pack_rows(res[...].astype(BF16))
            continue
        gain_ref = qg_ref if kind == 0 else kg_ref
        parts = []
        for g in range(FEATURE_CHUNK // HEAD_DIM):
            half = g % 2
            gain = gain_ref[half * HEAD_DIM:(half + 1) * HEAD_DIM, :]
            parts.append(_norm_rope_t(res[g * HEAD_DIM:(g + 1) * HEAD_DIM, :], gain, cos, sin))
        out = jnp.concatenate(parts, axis=0)
        if kind == 0:
            qt_ref[0, out_rows, :] = out.astype(BF16)
        else:
            k_ref[0, :, out_rows] = _pack_rows(out.T.astype(BF16))


def _qkv0(x, g, w, qg, kg, cos_t, sin_t, wgu, wd, wo0, wo1):
    b, s, d = x.shape
    tm = ROW_TILE
    nt = s // tm
    grid = (b, nt)

    def per_step(a):
        return pl.BlockSpec((a.shape[0] // (b * nt), a.shape[1]), lambda i, j: (i * nt + j, 0))

    def per_batch(a):
        return pl.BlockSpec((a.shape[0] // b, a.shape[1]), lambda i, j: (i, 0))

    def as_bf16(a):
        return jax.ShapeDtypeStruct(a.shape, BF16)
    feat = pl.BlockSpec((1, d, tm), lambda i, j: (i, 0, j))
    feat_packed = pl.BlockSpec((1, d // 2, tm), lambda i, j: (i, 0, j))
    tok = pl.BlockSpec((1, tm, d), lambda i, j: (i, j, 0))
    tok_packed = pl.BlockSpec((1, tm // 2, d), lambda i, j: (i, j, 0))
    rope = pl.BlockSpec((1, HEAD_DIM // 2, tm), lambda i, j: (i, 0, j))
    return pl.pallas_call(
        _qkv0_kernel,
        grid=grid,
        in_specs=[tok, _resident((1, d)), _resident(w.shape), _resident(qg.shape),
                  _resident(kg.shape), rope, rope,
                  per_step(wgu), per_batch(wd), per_step(wo0), per_step(wo1)],
        out_specs=[feat, tok_packed, feat_packed,
                   per_step(wgu), per_batch(wd), per_step(wo0), per_step(wo1)],
        out_shape=[jax.ShapeDtypeStruct((b, d, s), BF16),
                   jax.ShapeDtypeStruct((b, s // 2, d), jnp.uint32),
                   jax.ShapeDtypeStruct((b, d // 2, s), jnp.uint32),
                   as_bf16(wgu), as_bf16(wd), as_bf16(wo0), as_bf16(wo1)],
        scratch_shapes=[pltpu.VMEM((d // 2, w.shape[1]), jnp.uint32),
                        pltpu.VMEM((2, FEATURE_CHUNK, tm), F32)],
        compiler_params=pltpu.CompilerParams(
            dimension_semantics=("arbitrary", "arbitrary"),
            vmem_limit_bytes=V7X_VMEM_LIMIT_BYTES),
        name="qkv0_proj",
    )(x, g, w, qg, kg, cos_t, sin_t, wgu, wd, wo0, wo1)


def _diff_attn_kernel(lam_ref, qt_ref, k_ref, vt_ref, sg_ref, ot_ref,
                      qz_ref, m_ref, acc_ref, s_ref, *, lambda_init):
    bq = qt_ref.shape[-1]
    hd2 = 2 * HEAD_DIM
    n_heads = qt_ref.shape[1] // hd2
    i = pl.program_id(1)

    lp = lam_ref[...]
    lam = (jnp.exp(jnp.sum(lp[0:1] * lp[1:2], axis=-1, keepdims=True))
           - jnp.exp(jnp.sum(lp[2:3] * lp[3:4], axis=-1, keepdims=True)) + lambda_init)

    zero = jnp.zeros((HEAD_DIM, bq), BF16)
    for h in range(n_heads):
        q0 = qt_ref[0, h * hd2:h * hd2 + HEAD_DIM, :]
        q1 = qt_ref[0, h * hd2 + HEAD_DIM:(h + 1) * hd2, :]
        qz_ref[2 * h] = jnp.concatenate([q0, zero], axis=0)
        qz_ref[2 * h + 1] = jnp.concatenate([zero, q1], axis=0)

    def block(k0, bk, first):
        ones = jnp.ones((ONES_ROWS, bk), BF16)
        k0_packed = pl.multiple_of(lax.shift_right_logical(k0, 1), bk // 2)
        if first:
            kidx = lax.broadcasted_iota(jnp.int32, (bk, bq), 0)
            qidx = lax.broadcasted_iota(jnp.int32, (bk, bq), 1)
            keep = kidx <= qidx + (bk - bq)

        def scores(hc):
            h = hc // 2
            kblk = _unpack_rows(k_ref[0, pl.ds(k0_packed, bk // 2), h * hd2:(h + 1) * hd2])
            return jnp.dot(kblk, qz_ref[hc], preferred_element_type=F32)

        n_slots = s_ref.shape[0]

        def park(hc):
            s_ref[hc % n_slots, :bk, :] = scores(hc)

        for hc in range(LOOKAHEAD):
            park(hc)
        for hc in range(2 * n_heads):
            h = hc // 2
            if hc + LOOKAHEAD < 2 * n_heads:
                park(hc + LOOKAHEAD)
            s = s_ref[hc % n_slots, :bk, :]
            vblk = _unpack_rows(vt_ref[0, h * HEAD_DIM:(h + 1) * HEAD_DIM, pl.ds(k0, bk)])
            vext = jnp.concatenate([vblk, ones], axis=0)
            if first:
                s = jnp.where(keep, s, NEG_INF)
                m_new = jnp.max(s, axis=0, keepdims=True)
                p = jnp.exp2(s - m_new)
                acc_ref[hc] = jnp.dot(vext, p.astype(BF16), preferred_element_type=F32)
            else:
                m_old = m_ref[hc]
                m_new = jnp.maximum(m_old, jnp.max(s, axis=0, keepdims=True))
                alpha = jnp.exp2(m_old - m_new)
                p = jnp.exp2(s - m_new)
                acc_ref[hc] = alpha * acc_ref[hc] + jnp.dot(
                    vext, p.astype(BF16), preferred_element_type=F32)
            m_ref[hc] = m_new

    @pl.when((i & 1) == 0)
    def _():
        block(pl.multiple_of(i * bq, bq), bq, True)

    @pl.when((i & 1) == 1)
    def _():
        block(pl.multiple_of((i - 1) * bq, 2 * bq), 2 * bq, True)

    def body(j, carry):
        block(pl.multiple_of(j * 2 * bq, 2 * bq), 2 * bq, False)
        return carry

    lax.fori_loop(0, lax.shift_right_logical(i, 1), body, 0)

    for h in range(n_heads):
        a0 = acc_ref[2 * h]
        a1 = acc_ref[2 * h + 1]
        inv0 = 1.0 / a0[hd2:hd2 + 1]
        inv1 = lam / a1[hd2:hd2 + 1]
        o = a0[:hd2] * inv0 - a1[:hd2] * inv1
        r = lax.rsqrt(jnp.mean(o * o, axis=0, keepdims=True) + NORM_EPS)
        ot_ref[0, h * hd2:(h + 1) * hd2, :] = (
            o * r * sg_ref[...]).astype(BF16)


def _diff_attn(lam_p, qt, k, vt, sg, lambda_init):
    b, d, s = qt.shape
    hd2 = 2 * HEAD_DIM
    bq = ATTN_BLOCK
    n_half = d // HEAD_DIM
    return pl.pallas_call(
        functools.partial(_diff_attn_kernel, lambda_init=lambda_init),
        grid=(b, s // bq),
        in_specs=[_resident(lam_p.shape),
                  pl.BlockSpec((1, d, bq), lambda bi, i: (bi, 0, i)),
                  pl.BlockSpec((1, s // 2, d), lambda bi, i: (bi, 0, 0)),
                  pl.BlockSpec((1, d // 2, s), lambda bi, i: (bi, 0, 0)),
                  _resident(sg.shape)],
        out_specs=pl.BlockSpec((1, d, bq), lambda bi, i: (bi, 0, i)),
        out_shape=jax.ShapeDtypeStruct((b, d, s), BF16),
        scratch_shapes=[pltpu.VMEM((n_half, hd2, bq), BF16),
                        pltpu.VMEM((n_half, 1, bq), F32),
                        pltpu.VMEM((n_half, hd2 + ONES_ROWS, bq), F32),
                        pltpu.VMEM((LOOKAHEAD + 1, 2 * bq, bq), F32)],
        compiler_params=pltpu.CompilerParams(
            dimension_semantics=("parallel", "arbitrary"),
            vmem_limit_bytes=V7X_VMEM_LIMIT_BYTES),
        name="diff_attn",
    )(lam_p, qt, k, vt, sg)


def _proj_ffn_kernel(x_ref, at_ref, wo_ref, g_ref, wg_ref, wu_ref, wd_ref, o_ref, hid_ref):
    x = x_ref[0] + lax.dot_general(at_ref[0], wo_ref[...], TN_DIMS,
                                   preferred_element_type=F32)
    h = (x * _rms_scale(x) * g_ref[...]).astype(BF16)
    n_chunks = hid_ref.shape[-1] // FFN_CHUNK
    for c in range(n_chunks):
        cols = slice(c * FFN_CHUNK, (c + 1) * FFN_CHUNK)
        gate = jnp.dot(h, wg_ref[:, cols], preferred_element_type=F32)
        up = jnp.dot(h, wu_ref[:, cols], preferred_element_type=F32)
        hid_ref[:, c * FFN_CHUNK:(c + 1) * FFN_CHUNK] = (
            gate * jax.nn.sigmoid(gate) * up).astype(BF16)
    o_ref[0] = x + jnp.dot(hid_ref[...], wd_ref[...], preferred_element_type=F32)


def _proj_ffn(x, at, wo, g, wgu, wd, layer):
    b, s, d = x.shape
    tm = ROW_TILE
    hidden = wd.shape[1]
    tok = pl.BlockSpec((1, tm, d), lambda i, j: (i, j, 0))
    feat = pl.BlockSpec((1, d, tm), lambda i, j: (i, 0, j))
    once = pl.Buffered(1)
    return pl.pallas_call(
        _proj_ffn_kernel,
        grid=(b, s // tm),
        in_specs=[tok, feat, _resident(wo.shape), _resident((1, d)),
                  pl.BlockSpec((None, d, hidden), lambda i, j: (layer, 0, 0), pipeline_mode=once),
                  pl.BlockSpec((None, d, hidden), lambda i, j: (layer, 0, 1), pipeline_mode=once),
                  pl.BlockSpec((None, hidden, d), lambda i, j: (layer, 0, 0), pipeline_mode=once)],
        out_specs=tok,
        out_shape=jax.ShapeDtypeStruct((b, s, d), F32),
        scratch_shapes=[pltpu.VMEM((tm, hidden), BF16)],
        compiler_params=pltpu.CompilerParams(
            dimension_semantics=("parallel", "parallel"),
            vmem_limit_bytes=V7X_VMEM_LIMIT_BYTES),
        name="proj_ffn",
    )(x, at, wo, g, wgu, wgu, wd)


def _qkv1_kernel(x_ref, ga_ref, gkv_ref, wq_ref, wkv_ref, qg_ref, kg_ref, cos_ref, sin_ref,
                 qt_ref, k_ref, vt_ref, wqp_ref, wkvp_ref, kv_ref, res_ref):
    d_model = x_ref.shape[-1]
    kv_dim = k_ref.shape[-1]

    @pl.when((pl.program_id(0) == 0) & (pl.program_id(1) == 0))
    def _():
        _pack_weight(wq_ref, wqp_ref)
        _pack_weight(wkv_ref, wkvp_ref)

    x = x_ref[0]
    xn = x * _rms_scale(x)
    h_a = (xn * ga_ref[...]).astype(BF16)
    h_kv = (xn * gkv_ref[...]).astype(BF16)
    cos = cos_ref[0]
    sin = sin_ref[0]

    n_chunks = d_model // FEATURE_CHUNK

    def project_q(c):
        res_ref[c % 2] = _project_t(h_a, wqp_ref,
                                    slice(c * FEATURE_CHUNK, (c + 1) * FEATURE_CHUNK))

    kv_ref[...] = _project_t(h_kv, wkvp_ref, slice(0, 2 * kv_dim))
    project_q(0)
    kparts = [_norm_rope_t(kv_ref[g * HEAD_DIM:(g + 1) * HEAD_DIM, :], kg_ref[...], cos, sin)
              for g in range(kv_dim // HEAD_DIM)]
    k_ref[0] = _pack_rows(jnp.concatenate(kparts, axis=0).T.astype(BF16))
    vt_ref[0] = _pack_rows(kv_ref[kv_dim:, :].astype(BF16))

    for c in range(n_chunks):
        if c + 1 < n_chunks:
            project_q(c + 1)
        res = res_ref.at[c % 2]
        rows = slice(c * FEATURE_CHUNK, (c + 1) * FEATURE_CHUNK)
        parts = [_norm_rope_t(res[g * HEAD_DIM:(g + 1) * HEAD_DIM, :], qg_ref[...], cos, sin)
                 for g in range(FEATURE_CHUNK // HEAD_DIM)]
        qt_ref[0, rows, :] = jnp.concatenate(parts, axis=0).astype(BF16)


def _qkv1(x, ga, gkv, wq, wkv, qg, kg, cos_t, sin_t):
    b, s, d = x.shape
    tm = ROW_TILE
    kv_dim = wkv.shape[1] // 2
    tok = pl.BlockSpec((1, tm, d), lambda i, j: (i, j, 0))
    rope = pl.BlockSpec((1, HEAD_DIM // 2, tm), lambda i, j: (i, 0, j))
    return pl.pallas_call(
        _qkv1_kernel,
        grid=(b, s // tm),
        in_specs=[tok, _resident((1, d)), _resident((1, d)), _resident(wq.shape),
                  _resident(wkv.shape), _resident(qg.shape), _resident(kg.shape), rope, rope],
        out_specs=[pl.BlockSpec((1, d, tm), lambda i, j: (i, 0, j)),
                   pl.BlockSpec((1, tm // 2, kv_dim), lambda i, j: (i, j, 0)),
                   pl.BlockSpec((1, kv_dim // 2, tm), lambda i, j: (i, 0, j))],
        out_shape=[jax.ShapeDtypeStruct((b, d, s), BF16),
                   jax.ShapeDtypeStruct((b, s // 2, kv_dim), jnp.uint32),
                   jax.ShapeDtypeStruct((b, kv_dim // 2, s), jnp.uint32)],
        scratch_shapes=[pltpu.VMEM((d // 2, wq.shape[1]), jnp.uint32),
                        pltpu.VMEM((d // 2, wkv.shape[1]), jnp.uint32),
                        pltpu.VMEM((2 * kv_dim, tm), F32),
                        pltpu.VMEM((2, FEATURE_CHUNK, tm), F32)],
        compiler_params=pltpu.CompilerParams(
            dimension_semantics=("arbitrary", "arbitrary"),
            vmem_limit_bytes=V7X_VMEM_LIMIT_BYTES),
        name="qkv1_proj",
    )(x, ga, gkv, wq, wkv, qg, kg, cos_t, sin_t)


def _swa_kernel(qt_ref, k_ref, vt_ref, sink_ref, ot_ref, bias_ref, s_ref):
    w = WINDOW
    s_len = qt_ref.shape[-1]
    kv_dim = k_ref.shape[-1]
    gw = SW_PACK * w

    n_kv = kv_dim // HEAD_DIM
    n_blocks = s_len // w

    kidx = lax.broadcasted_iota(jnp.int32, (2 * w, gw), 0)
    qidx = lax.broadcasted_iota(jnp.int32, (2 * w, gw), 1) & (w - 1)
    bias_ref[...] = jnp.where((kidx > qidx) & (kidx <= qidx + w), 0.0, NEG_INF).astype(F32)

    def window(n, first):
        nk = w if first else 2 * w
        k0 = 0 if first else pl.multiple_of((n - 1) * w, w)
        q0 = 0 if first else pl.multiple_of(n * w, w)
        return nk, k0, q0

    def heads(j, part):
        return [SW_GROUP * j + SW_PACK * part + u for u in range(SW_PACK)]

    def scores(n, j, part, first):
        nk, k0, q0 = window(n, first)
        k0_packed = k0 if first else pl.multiple_of((n - 1) * (w // 2), w // 2)
        kwin = _unpack_rows(k_ref[0, pl.ds(k0_packed, nk // 2), :])
        qcat = jnp.concatenate(
            [qt_ref[0, h * HEAD_DIM:(h + 1) * HEAD_DIM, pl.ds(q0, w)]
             for h in heads(j, part)], axis=1)
        pieces = []
        if j > 0:
            pieces.append(jnp.zeros((j * HEAD_DIM, gw), BF16))
        pieces.append(qcat)
        if j + 1 < n_kv:
            pieces.append(jnp.zeros(((n_kv - j - 1) * HEAD_DIM, gw), BF16))
        qz = jnp.concatenate(pieces, axis=0) if len(pieces) > 1 else qcat
        return jnp.dot(kwin, qz, preferred_element_type=F32)

    def finish(n, j, part, first, s):
        nk, k0, q0 = window(n, first)
        s = s + bias_ref[2 * w - nk:, :]
        sink = sink_ref[j:j + 1, part * gw:(part + 1) * gw]
        m = jnp.maximum(jnp.max(s, axis=0, keepdims=True), sink)
        e = jnp.exp2(s - m).astype(BF16)
        vwin = _unpack_rows(
            vt_ref[0, j * HEAD_DIM // 2:(j + 1) * HEAD_DIM // 2, pl.ds(k0, nk)])
        vext = jnp.concatenate([vwin, jnp.ones((ONES_ROWS, nk), BF16)], axis=0)
        o = jnp.dot(vext, e, preferred_element_type=F32)
        den = o[HEAD_DIM:HEAD_DIM + 1] + jnp.exp2(sink - m)
        o = o[:HEAD_DIM] * (1.0 / den)
        for u, h in enumerate(heads(j, part)):
            ot_ref[0, h * HEAD_DIM:(h + 1) * HEAD_DIM, pl.ds(q0, w)] = (
                o[:, u * w:(u + 1) * w].astype(BF16))

    def run(chains):
        n_slots = s_ref.shape[0]

        def park(t):
            nk = window(chains[t][0], chains[t][3])[0]
            s_ref[t % n_slots, :nk, :] = scores(*chains[t])

        for t in range(min(LOOKAHEAD, len(chains))):
            park(t)
        for t, ch in enumerate(chains):
            if t + LOOKAHEAD < len(chains):
                park(t + LOOKAHEAD)
            nk = window(ch[0], ch[3])[0]
            finish(*ch, s_ref[t % n_slots, :nk, :])

    parts = range(SW_GROUP // SW_PACK)
    run([(0, j, part, True) for j in range(n_kv) for part in parts])

    def body(t, carry):
        n0 = 1 + t * SWA_UNROLL
        run([(n0 + u, j, part, False)
             for u in range(SWA_UNROLL) for j in range(n_kv) for part in parts])
        return carry

    lax.fori_loop(0, (n_blocks - 1) // SWA_UNROLL, body, 0)


def _swa(qt, k, vt, sink_rows):
    b, d, s = qt.shape
    kv_dim = k.shape[-1]
    assert (s // WINDOW - 1) % SWA_UNROLL == 0
    return pl.pallas_call(
        _swa_kernel,
        grid=(b,),
        in_specs=[pl.BlockSpec((1, d, s), lambda i: (i, 0, 0)),
                  pl.BlockSpec((1, s // 2, kv_dim), lambda i: (i, 0, 0)),
                  pl.BlockSpec((1, kv_dim // 2, s), lambda i: (i, 0, 0)),
                  _resident(sink_rows.shape)],
        out_specs=pl.BlockSpec((1, d, s), lambda i: (i, 0, 0)),
        out_shape=jax.ShapeDtypeStruct((b, d, s), BF16),
        scratch_shapes=[pltpu.VMEM((2 * WINDOW, SW_PACK * WINDOW), F32),
                        pltpu.VMEM((LOOKAHEAD + 1, 2 * WINDOW, SW_PACK * WINDOW), F32)],
        compiler_params=pltpu.CompilerParams(
            dimension_semantics=("parallel",),
            vmem_limit_bytes=V7X_VMEM_LIMIT_BYTES),
        name="swa_attn",
    )(qt, k, vt, sink_rows)


def _lane_bcast(col, n):
    return jnp.broadcast_to(col.astype(F32).reshape(-1, 1), (col.size, n))


def kernel(x, positions, attn_norm, ffn_norm, w_gate_up, w_down, da_w_qkv, da_q_norm, da_k_norm,
           da_lambda, da_subln, da_w_o, kv_norm, w_kv, k_norm, sw_w_q, sw_q_norm, sw_sinks, sw_w_o):
    b, s, d = x.shape
    scale = 1.0 / math.sqrt(HEAD_DIM)

    inv = 1.0 / (ROPE_THETA ** (jnp.arange(0, HEAD_DIM, 2, dtype=F32) / HEAD_DIM))
    ang_t = positions.astype(F32)[:, None, :] * inv[None, :, None]
    cos_t = jnp.cos(ang_t)
    sin_t = jnp.sin(ang_t)

    lambda_init = 0.8 - 0.6 * math.exp(-0.3 * 0)
    n_layers, _, two_h = w_gate_up.shape
    qt, k, vt, wgu, wd, wo0, wo1 = _qkv0(
        x, attn_norm[0].reshape(1, d), da_w_qkv[0],
        _lane_bcast(da_q_norm[0] * (scale * LOG2_E), LANE_TILE),
        _lane_bcast(da_k_norm[0], LANE_TILE), cos_t, sin_t,
        w_gate_up.reshape(n_layers * d, two_h), w_down.reshape(n_layers * (two_h // 2), d),
        da_w_o[0], sw_w_o[0])
    wgu = wgu.reshape(n_layers, d, two_h)
    wd = wd.reshape(n_layers, two_h // 2, d)
    at = _diff_attn(da_lambda[0].astype(F32), qt, k, vt,
                    _lane_bcast(da_subln[0] * (1.0 - lambda_init), ATTN_BLOCK), lambda_init)
    x = _proj_ffn(x, at, wo0, ffn_norm[0].reshape(1, d), wgu, wd, 0)

    qt, k, vt = _qkv1(x, attn_norm[1].reshape(1, d), kv_norm.reshape(1, d),
                      sw_w_q[0], w_kv,
                      _lane_bcast(sw_q_norm[0] * (scale * LOG2_E), LANE_TILE),
                      _lane_bcast(k_norm, LANE_TILE), cos_t, sin_t)
    sink_rows = jnp.repeat((sw_sinks[0].astype(F32) * LOG2_E).reshape(SW_KV_HEADS, SW_GROUP),
                           WINDOW, axis=1)
    at = _swa(qt, k, vt, sink_rows)
    x = _proj_ffn(x, at, wo1, ffn_norm[1].reshape(1, d), wgu, wd, 1)
    return x
```

```python
import functools
import math

import jax
import jax.numpy as jnp
from jax import lax
from jax.experimental import pallas as pl
from jax.experimental.pallas import tpu as pltpu

HEAD_DIM = 64
ROPE_THETA = 10000.0
NORM_EPS = 1e-6
NEG_INF = -1e30
LOG2_E = math.log2(math.e)
WINDOW = 128
SW_KV_HEADS = 4
SW_GROUP = 4
SW_PACK = 4

F32 = jnp.float32
BF16 = jnp.bfloat16

V7X_VMEM_LIMIT_BYTES = 56 * 1024 * 1024

ROW_TILE = 1024
FEATURE_CHUNK = 512
FFN_CHUNK = 256
ATTN_BLOCK = 256
LANE_TILE = 128
ONES_ROWS = 16
LOOKAHEAD = 6
SWA_UNROLL = 15

TN_DIMS = (((0,), (0,)), ((), ()))


def _resident(shape):
    nd = len(shape)
    return pl.BlockSpec(shape, lambda *_: (0,) * nd, pipeline_mode=pl.Buffered(1))


def _rms_scale(x):
    return lax.rsqrt(jnp.mean(x * x, axis=-1, keepdims=True) + NORM_EPS)


def _pack_rows(x):
    return pltpu.bitcast(x, jnp.uint32)


def _unpack_rows(x):
    return pltpu.bitcast(x, BF16)


def _pack_weight(w_ref, wp_ref):
    for c in range(w_ref.shape[1] // FEATURE_CHUNK):
        cols = slice(c * FEATURE_CHUNK, (c + 1) * FEATURE_CHUNK)
        wp_ref[:, cols] = _pack_rows(w_ref[:, cols].astype(BF16))


def _project_t(h, wp_ref, cols):
    return jnp.dot(h, _unpack_rows(wp_ref[:, cols]), preferred_element_type=F32).T


def _norm_rope_t(t, gain, cos, sin):
    r = lax.rsqrt(jnp.mean(t * t, axis=0, keepdims=True) + NORM_EPS)
    tr = t * r
    tn = jnp.concatenate([tr[:, l:l + LANE_TILE] * gain
                          for l in range(0, t.shape[1], LANE_TILE)], axis=1)
    x1 = tn[: HEAD_DIM // 2]
    x2 = tn[HEAD_DIM // 2:]
    return jnp.concatenate([x1 * cos - x2 * sin, x2 * cos + x1 * sin], axis=0)


def _qkv0_kernel(x_ref, g_ref, w_ref, qg_ref, kg_ref, cos_ref, sin_ref,
                 wgu_ref, wd_ref, wo0_ref, wo1_ref,
                 qt_ref, k_ref, vt_ref, wgu_bf_ref, wd_bf_ref, wo0_bf_ref, wo1_bf_ref,
                 wp_ref, res_ref):
    d_model = x_ref.shape[-1]

    @pl.when((pl.program_id(0) == 0) & (pl.program_id(1) == 0))
    def _():
        _pack_weight(w_ref, wp_ref)

    @pl.when(pl.program_id(1) == 0)
    def _():
        wd_bf_ref[...] = wd_ref[...].astype(BF16)

    wgu_bf_ref[...] = wgu_ref[...].astype(BF16)
    wo0_bf_ref[...] = wo0_ref[...].astype(BF16)
    wo1_bf_ref[...] = wo1_ref[...].astype(BF16)

    x = x_ref[0]
    h = (x * _rms_scale(x) * g_ref[...]).astype(BF16)
    cos = cos_ref[0]
    sin = sin_ref[0]
    n_chunks = d_model // FEATURE_CHUNK

    def project(c):
        res_ref[c % 2] = _project_t(h, wp_ref, slice(c * FEATURE_CHUNK, (c + 1) * FEATURE_CHUNK))

    project(0)
    for c in range(3 * n_chunks):
        if c + 1 < 3 * n_chunks:
            project(c + 1)
        res = res_ref.at[c % 2]
        kind, cc = divmod(c, n_chunks)
        out_rows = slice(cc * FEATURE_CHUNK, (cc + 1) * FEATURE_CHUNK)
        half_rows = slice(cc * FEATURE_CHUNK // 2, (cc + 1) * FEATURE_CHUNK // 2)
        if kind == 2:
            vt_ref[0, half_rows, :] = _pack_rows(res[...].astype(BF16))
            continue
        gain_ref = qg_ref if kind == 0 else kg_ref
        parts = []
        for g in range(FEATURE_CHUNK // HEAD_DIM):
            half = g % 2
            gain = gain_ref[half * HEAD_DIM:(half + 1) * HEAD_DIM, :]
            parts.append(_norm_rope_t(res[g * HEAD_DIM:(g + 1) * HEAD_DIM, :], gain, cos, sin))
        out = jnp.concatenate(parts, axis=0)
        if kind == 0:
            qt_ref[0, out_rows, :] = out.astype(BF16)
        else:
            k_ref[0, :, out_rows] = _pack_rows(out.T.astype(BF16))


def _qkv0(x, g, w, qg, kg, cos_t, sin_t, wgu, wd, wo0, wo1):
    b, s, d = x.shape
    tm = ROW_TILE
    nt = s // tm
    grid = (b, nt)

    def per_step(a):
        return pl.BlockSpec((a.shape[0] // (b * nt), a.shape[1]), lambda i, j: (i * nt + j, 0))

    def per_batch(a):
        return pl.BlockSpec((a.shape[0] // b, a.shape[1]), lambda i, j: (i, 0))

    def as_bf16(a):
        return jax.ShapeDtypeStruct(a.shape, BF16)
    feat = pl.BlockSpec((1, d, tm), lambda i, j: (i, 0, j))
    feat_packed = pl.BlockSpec((1, d // 2, tm), lambda i, j: (i, 0, j))
    tok = pl.BlockSpec((1, tm, d), lambda i, j: (i, j, 0))
    tok_packed = pl.BlockSpec((1, tm // 2, d), lambda i, j: (i, j, 0))
    rope = pl.BlockSpec((1, HEAD_DIM // 2, tm), lambda i, j: (i, 0, j))
    return pl.pallas_call(
        _qkv0_kernel,
        grid=grid,
        in_specs=[tok, _resident((1, d)), _resident(w.shape), _resident(qg.shape),
                  _resident(kg.shape), rope, rope,
                  per_step(wgu), per_batch(wd), per_step(wo0), per_step(wo1)],
        out_specs=[feat, tok_packed, feat_packed,
                   per_step(wgu), per_batch(wd), per_step(wo0), per_step(wo1)],
        out_shape=[jax.ShapeDtypeStruct((b, d, s), BF16),
                   jax.ShapeDtypeStruct((b, s // 2, d), jnp.uint32),
                   jax.ShapeDtypeStruct((b, d // 2, s), jnp.uint32),
                   as_bf16(wgu), as_bf16(wd), as_bf16(wo0), as_bf16(wo1)],
        scratch_shapes=[pltpu.VMEM((d // 2, w.shape[1]), jnp.uint32),
                        pltpu.VMEM((2, FEATURE_CHUNK, tm), F32)],
        compiler_params=pltpu.CompilerParams(
            dimension_semantics=("arbitrary", "arbitrary"),
            vmem_limit_bytes=V7X_VMEM_LIMIT_BYTES),
        name="qkv0_proj",
    )(x, g, w, qg, kg, cos_t, sin_t, wgu, wd, wo0, wo1)


def _diff_attn_kernel(lam_ref, qt_ref, k_ref, vt_ref, sg_ref, ot_ref,
                      qz_ref, m_ref, acc_ref, s_ref, *, lambda_init):
    bq = qt_ref.shape[-1]
    hd2 = 2 * HEAD_DIM
    n_heads = qt_ref.shape[1] // hd2
    i = pl.program_id(1)

    lp = lam_ref[...]
    lam = (jnp.exp(jnp.sum(lp[0:1] * lp[1:2], axis=-1, keepdims=True))
           - jnp.exp(jnp.sum(lp[2:3] * lp[3:4], axis=-1, keepdims=True)) + lambda_init)

    zero = jnp.zeros((HEAD_DIM, bq), BF16)
    for h in range(n_heads):
        q0 = qt_ref[0, h * hd2:h * hd2 + HEAD_DIM, :]
        q1 = qt_ref[0, h * hd2 + HEAD_DIM:(h + 1) * hd2, :]
        qz_ref[2 * h] = jnp.concatenate([q0, zero], axis=0)
        qz_ref[2 * h + 1] = jnp.concatenate([zero, q1], axis=0)

    def block(k0, bk, first):
        ones = jnp.ones((ONES_ROWS, bk), BF16)
        k0_packed = pl.multiple_of(lax.shift_right_logical(k0, 1), bk // 2)
        if first:
            kidx = lax.broadcasted_iota(jnp.int32, (bk, bq), 0)
            qidx = lax.broadcasted_iota(jnp.int32, (bk, bq), 1)
            keep = kidx <= qidx + (bk - bq)

        def scores(hc):
            h = hc // 2
            kblk = _unpack_rows(k_ref[0, pl.ds(k0_packed, bk // 2), h * hd2:(h + 1) * hd2])
            return jnp.dot(kblk, qz_ref[hc], preferred_element_type=F32)

        n_slots = s_ref.shape[0]

        def park(hc):
            s_ref[hc % n_slots, :bk, :] = scores(hc)

        for hc in range(LOOKAHEAD):
            park(hc)
        for hc in range(2 * n_heads):
            h = hc // 2
            if hc + LOOKAHEAD < 2 * n_heads:
                park(hc + LOOKAHEAD)
            s = s_ref[hc % n_slots, :bk, :]
            vblk = _unpack_rows(vt_ref[0, h * HEAD_DIM:(h + 1) * HEAD_DIM, pl.ds(k0, bk)])
            vext = jnp.concatenate([vblk, ones], axis=0)
            if first:
                s = jnp.where(keep, s, NEG_INF)
                m_new = jnp.max(s, axis=0, keepdims=True)
                p = jnp.exp2(s - m_new)
                acc_ref[hc] = jnp.dot(vext, p.astype(BF16), preferred_element_type=F32)
            else:
                m_old = m_ref[hc]
                m_new = jnp.maximum(m_old, jnp.max(s, axis=0, keepdims=True))
                alpha = jnp.exp2(m_old - m_new)
                p = jnp.exp2(s - m_new)
                acc_ref[hc] = alpha * acc_ref[hc] + jnp.dot(
                    vext, p.astype(BF16), preferred_element_type=F32)
            m_ref[hc] = m_new

    @pl.when((i & 1) == 0)
    def _():
        block(pl.multiple_of(i * bq, bq), bq, True)

    @pl.when((i & 1) == 1)
    def _():
        block(pl.multiple_of((i - 1) * bq, 2 * bq), 2 * bq, True)

    def body(j, carry):
        block(pl.multiple_of(j * 2 * bq, 2 * bq), 2 * bq, False)
        return carry

    lax.fori_loop(0, lax.shift_right_logical(i, 1), body, 0)

    for h in range(n_heads):
        a0 = acc_ref[2 * h]
        a1 = acc_ref[2 * h + 1]
        inv0 = 1.0 / a0[hd2:hd2 + 1]
        inv1 = lam / a1[hd2:hd2 + 1]
        o = a0[:hd2] * inv0 - a1[:hd2] * inv1
        r = lax.rsqrt(jnp.mean(o * o, axis=0, keepdims=True) + NORM_EPS)
        ot_ref[0, h * hd2:(h + 1) * hd2, :] = (
            o * r * sg_ref[...]).astype(BF16)


def _diff_attn(lam_p, qt, k, vt, sg, lambda_init):
    b, d, s = qt.shape
    hd2 = 2 * HEAD_DIM
    bq = ATTN_BLOCK
    n_half = d // HEAD_DIM
    return pl.pallas_call(
        functools.partial(_diff_attn_kernel, lambda_init=lambda_init),
        grid=(b, s // bq),
        in_specs=[_resident(lam_p.shape),
                  pl.BlockSpec((1, d, bq), lambda bi, i: (bi, 0, i)),
                  pl.BlockSpec((1, s // 2, d), lambda bi, i: (bi, 0, 0)),
                  pl.BlockSpec((1, d // 2, s), lambda bi, i: (bi, 0, 0)),
                  _resident(sg.shape)],
        out_specs=pl.BlockSpec((1, d, bq), lambda bi, i: (bi, 0, i)),
        out_shape=jax.ShapeDtypeStruct((b, d, s), BF16),
        scratch_shapes=[pltpu.VMEM((n_half, hd2, bq), BF16),
                        pltpu.VMEM((n_half, 1, bq), F32),
                        pltpu.VMEM((n_half, hd2 + ONES_ROWS, bq), F32),
                        pltpu.VMEM((LOOKAHEAD + 1, 2 * bq, bq), F32)],
        compiler_params=pltpu.CompilerParams(
            dimension_semantics=("parallel", "arbitrary"),
            vmem_limit_bytes=V7X_VMEM_LIMIT_BYTES),
        name="diff_attn",
    )(lam_p, qt, k, vt, sg)


def _proj_ffn_kernel(x_ref, at_ref, wo_ref, g_ref, wg_ref, wu_ref, wd_ref, o_ref, hid_ref):
    x = x_ref[0] + lax.dot_general(at_ref[0], wo_ref[...], TN_DIMS,
                                   preferred_element_type=F32)
    h = (x * _rms_scale(x) * g_ref[...]).astype(BF16)
    n_chunks = hid_ref.shape[-1] // FFN_CHUNK
    for c in range(n_chunks):
        cols = slice(c * FFN_CHUNK, (c + 1) * FFN_CHUNK)
        gate = jnp.dot(h, wg_ref[:, cols], preferred_element_type=F32)
        up = jnp.dot(h, wu_ref[:, cols], preferred_element_type=F32)
        hid_ref[:, c * FFN_CHUNK:(c + 1) * FFN_CHUNK] = (
            gate * jax.nn.sigmoid(gate) * up).astype(BF16)
    o_ref[0] = x + jnp.dot(hid_ref[...], wd_ref[...], preferred_element_type=F32)


def _proj_ffn(x, at, wo, g, wgu, wd, layer):
    b, s, d = x.shape
    tm = ROW_TILE
    hidden = wd.shape[1]
    tok = pl.BlockSpec((1, tm, d), lambda i, j: (i, j, 0))
    feat = pl.BlockSpec((1, d, tm), lambda i, j: (i, 0, j))
    once = pl.Buffered(1)
    return pl.pallas_call(
        _proj_ffn_kernel,
        grid=(b, s // tm),
        in_specs=[tok, feat, _resident(wo.shape), _resident((1, d)),
                  pl.BlockSpec((None, d, hidden), lambda i, j: (layer, 0, 0), pipeline_mode=once),
                  pl.BlockSpec((None, d, hidden), lambda i, j: (layer, 0, 1), pipeline_mode=once),
                  pl.BlockSpec((None, hidden, d), lambda i, j: (layer, 0, 0), pipeline_mode=once)],
        out_specs=tok,
        out_shape=jax.ShapeDtypeStruct((b, s, d), F32),
        scratch_shapes=[pltpu.VMEM((tm, hidden), BF16)],
        compiler_params=pltpu.CompilerParams(
            dimension_semantics=("parallel", "parallel"),
            vmem_limit_bytes=V7X_VMEM_LIMIT_BYTES),
        name="proj_ffn",
    )(x, at, wo, g, wgu, wgu, wd)


def _qkv1_kernel(x_ref, ga_ref, gkv_ref, wq_ref, wkv_ref, qg_ref, kg_ref, cos_ref, sin_ref,
                 qt_ref, k_ref, vt_ref, wqp_ref, wkvp_ref, kv_ref, res_ref):
    d_model = x_ref.shape[-1]
    kv_dim = k_ref.shape[-1]

    @pl.when((pl.program_id(0) == 0) & (pl.program_id(1) == 0))
    def _():
        _pack_weight(wq_ref, wqp_ref)
        _pack_weight(wkv_ref, wkvp_ref)

    x = x_ref[0]
    xn = x * _rms_scale(x)
    h_a = (xn * ga_ref[...]).astype(BF16)
    h_kv = (xn * gkv_ref[...]).astype(BF16)
    cos = cos_ref[0]
    sin = sin_ref[0]

    n_chunks = d_model // FEATURE_CHUNK

    def project_q(c):
        res_ref[c % 2] = _project_t(h_a, wqp_ref,
                                    slice(c * FEATURE_CHUNK, (c + 1) * FEATURE_CHUNK))

    kv_ref[...] = _project_t(h_kv, wkvp_ref, slice(0, 2 * kv_dim))
    project_q(0)
    kparts = [_norm_rope_t(kv_ref[g * HEAD_DIM:(g + 1) * HEAD_DIM, :], kg_ref[...], cos, sin)
              for g in range(kv_dim // HEAD_DIM)]
    k_ref[0] = _pack_rows(jnp.concatenate(kparts, axis=0).T.astype(BF16))
    vt_ref[0] = _pack_rows(kv_ref[kv_dim:, :].astype(BF16))

    for c in range(n_chunks):
        if c + 1 < n_chunks:
            project_q(c + 1)
        res = res_ref.at[c % 2]
        rows = slice(c * FEATURE_CHUNK, (c + 1) * FEATURE_CHUNK)
        parts = [_norm_rope_t(res[g * HEAD_DIM:(g + 1) * HEAD_DIM, :], qg_ref[...], cos, sin)
                 for g in range(FEATURE_CHUNK // HEAD_DIM)]
        qt_ref[0, rows, :] = jnp.concatenate(parts, axis=0).astype(BF16)


def _qkv1(x, ga, gkv, wq, wkv, qg, kg, cos_t, sin_t):
    b, s, d = x.shape
    tm = ROW_TILE
    kv_dim = wkv.shape[1] // 2
    tok = pl.BlockSpec((1, tm, d), lambda i, j: (i, j, 0))
    rope = pl.BlockSpec((1, HEAD_DIM // 2, tm), lambda i, j: (i, 0, j))
    return pl.pallas_call(
        _qkv1_kernel,
        grid=(b, s // tm),
        in_specs=[tok, _resident((1, d)), _resident((1, d)), _resident(wq.shape),
                  _resident(wkv.shape), _resident(qg.shape), _resident(kg.shape), rope, rope],
        out_specs=[pl.BlockSpec((1, d, tm), lambda i, j: (i, 0, j)),
                   pl.BlockSpec((1, tm // 2, kv_dim), lambda i, j: (i, j, 0)),
                   pl.BlockSpec((1, kv_dim // 2, tm), lambda i, j: (i, 0, j))],
        out_shape=[jax.ShapeDtypeStruct((b, d, s), BF16),
                   jax.ShapeDtypeStruct((b, s // 2, kv_dim), jnp.uint32),
                   jax.ShapeDtypeStruct((b, kv_dim // 2, s), jnp.uint32)],
        scratch_shapes=[pltpu.VMEM((d // 2, wq.shape[1]), jnp.uint32),
                        pltpu.VMEM((d // 2, wkv.shape[1]), jnp.uint32),
                        pltpu.VMEM((2 * kv_dim, tm), F32),
                        pltpu.VMEM((2, FEATURE_CHUNK, tm), F32)],
        compiler_params=pltpu.CompilerParams(
            dimension_semantics=("arbitrary", "arbitrary"),
            vmem_limit_bytes=V7X_VMEM_LIMIT_BYTES),
        name="qkv1_proj",
    )(x, ga, gkv, wq, wkv, qg, kg, cos_t, sin_t)


def _swa_kernel(qt_ref, k_ref, vt_ref, sink_ref, ot_ref, bias_ref, s_ref):
    w = WINDOW
    s_len = qt_ref.shape[-1]
    kv_dim = k_ref.shape[-1]
    gw = SW_PACK * w

    n_kv = kv_dim // HEAD_DIM
    n_blocks = s_len // w

    kidx = lax.broadcasted_iota(jnp.int32, (2 * w, gw), 0)
    qidx = lax.broadcasted_iota(jnp.int32, (2 * w, gw), 1) & (w - 1)
    bias_ref[...] = jnp.where((kidx > qidx) & (kidx <= qidx + w), 0.0, NEG_INF).astype(F32)

    def window(n, first):
        nk = w if first else 2 * w
        k0 = 0 if first else pl.multiple_of((n - 1) * w, w)
        q0 = 0 if first else pl.multiple_of(n * w, w)
        return nk, k0, q0

    def heads(j, part):
        return [SW_GROUP * j + SW_PACK * part + u for u in range(SW_PACK)]

    def scores(n, j, part, first):
        nk, k0, q0 = window(n, first)
        k0_packed = k0 if first else pl.multiple_of((n - 1) * (w // 2), w // 2)
        kwin = _unpack_rows(k_ref[0, pl.ds(k0_packed, nk // 2), :])
        qcat = jnp.concatenate(
            [qt_ref[0, h * HEAD_DIM:(h + 1) * HEAD_DIM, pl.ds(q0, w)]
             for h in heads(j, part)], axis=1)
        pieces = []
        if j > 0:
            pieces.append(jnp.zeros((j * HEAD_DIM, gw), BF16))
        pieces.append(qcat)
        if j + 1 < n_kv:
            pieces.append(jnp.zeros(((n_kv - j - 1) * HEAD_DIM, gw), BF16))
        qz = jnp.concatenate(pieces, axis=0) if len(pieces) > 1 else qcat
        return jnp.dot(kwin, qz, preferred_element_type=F32)

    def finish(n, j, part, first, s):
        nk, k0, q0 = window(n, first)
        s = s + bias_ref[2 * w - nk:, :]
        sink = sink_ref[j:j + 1, part * gw:(part + 1) * gw]
        m = jnp.maximum(jnp.max(s, axis=0, keepdims=True), sink)
        e = jnp.exp2(s - m).astype(BF16)
        vwin = _unpack_rows(
            vt_ref[0, j * HEAD_DIM // 2:(j + 1) * HEAD_DIM // 2, pl.ds(k0, nk)])
        vext = jnp.concatenate([vwin, jnp.ones((ONES_ROWS, nk), BF16)], axis=0)
        o = jnp.dot(vext, e, preferred_element_type=F32)
        den = o[HEAD_DIM:HEAD_DIM + 1] + jnp.exp2(sink - m)
        o = o[:HEAD_DIM] * (1.0 / den)
        for u, h in enumerate(heads(j, part)):
            ot_ref[0, h * HEAD_DIM:(h + 1) * HEAD_DIM, pl.ds(q0, w)] = (
                o[:, u * w:(u + 1) * w].astype(BF16))

    def run(chains):
        n_slots = s_ref.shape[0]

        def park(t):
            nk = window(chains[t][0], chains[t][3])[0]
            s_ref[t % n_slots, :nk, :] = scores(*chains[t])

        for t in range(min(LOOKAHEAD, len(chains))):
            park(t)
        for t, ch in enumerate(chains):
            if t + LOOKAHEAD < len(chains):
                park(t + LOOKAHEAD)
            nk = window(ch[0], ch[3])[0]
            finish(*ch, s_ref[t % n_slots, :nk, :])

    parts = range(SW_GROUP // SW_PACK)
    run([(0, j, part, True) for j in range(n_kv) for part in parts])

    def body(t, carry):
        n0 = 1 + t * SWA_UNROLL
        run([(n0 + u, j, part, False)
             for u in range(SWA_UNROLL) for j in range(n_kv) for part in parts])
        return carry

    lax.fori_loop(0, (n_blocks - 1) // SWA_UNROLL, body, 0)


def _swa(qt, k, vt, sink_rows):
    b, d, s = qt.shape
    kv_dim = k.shape[-1]
    assert (s // WINDOW - 1) % SWA_UNROLL == 0
    return pl.pallas_call(
        _swa_kernel,
        grid=(b,),
        in_specs=[pl.BlockSpec((1, d, s), lambda i: (i, 0, 0)),
                  pl.BlockSpec((1, s // 2, kv_dim), lambda i: (i, 0, 0)),
                  pl.BlockSpec((1, kv_dim // 2, s), lambda i: (i, 0, 0)),
                  _resident(sink_rows.shape)],
        out_specs=pl.BlockSpec((1, d, s), lambda i: (i, 0, 0)),
        out_shape=jax.ShapeDtypeStruct((b, d, s), BF16),
        scratch_shapes=[pltpu.VMEM((2 * WINDOW, SW_PACK * WINDOW), F32),
                        pltpu.VMEM((LOOKAHEAD + 1, 2 * WINDOW, SW_PACK * WINDOW), F32)],
        compiler_params=pltpu.CompilerParams(
            dimension_semantics=("parallel",),
            vmem_limit_bytes=V7X_VMEM_LIMIT_BYTES),
        name="swa_attn",
    )(qt, k, vt, sink_rows)


def _lane_bcast(col, n):
    return jnp.broadcast_to(col.astype(F32).reshape(-1, 1), (col.size, n))


def kernel(x, positions, attn_norm, ffn_norm, w_gate_up, w_down, da_w_qkv, da_q_norm, da_k_norm,
           da_lambda, da_subln, da_w_o, kv_norm, w_kv, k_norm, sw_w_q, sw_q_norm, sw_sinks, sw_w_o):
    b, s, d = x.shape
    scale = 1.0 / math.sqrt(HEAD_DIM)

    inv = 1.0 / (ROPE_THETA ** (jnp.arange(0, HEAD_DIM, 2, dtype=F32) / HEAD_DIM))
    ang_t = positions.astype(F32)[:, None, :] * inv[None, :, None]
    cos_t = jnp.cos(ang_t)
    sin_t = jnp.sin(ang_t)

    lambda_init = 0.8 - 0.6 * math.exp(-0.3 * 0)
    n_layers, _, two_h = w_gate_up.shape
    qt, k, vt, wgu, wd, wo0, wo1 = _qkv0(
        x, attn_norm[0].reshape(1, d), da_w_qkv[0],
        _lane_bcast(da_q_norm[0] * (scale * LOG2_E), LANE_TILE),
        _lane_bcast(da_k_norm[0], LANE_TILE), cos_t, sin_t,
        w_gate_up.reshape(n_layers * d, two_h), w_down.reshape(n_layers * (two_h // 2), d),
        da_w_o[0], sw_w_o[0])
    wgu = wgu.reshape(n_layers, d, two_h)
    wd = wd.reshape(n_layers, two_h // 2, d)
    at = _diff_attn(da_lambda[0].astype(F32), qt, k, vt,
                    _lane_bcast(da_subln[0] * (1.0 - lambda_init), ATTN_BLOCK), lambda_init)
    x = _proj_ffn(x, at, wo0, ffn_norm[0].reshape(1, d), wgu, wd, 0)

    qt, k, vt = _qkv1(x, attn_norm[1].reshape(1, d), kv_norm.reshape(1, d),
                      sw_w_q[0], w_kv,
                      _lane_bcast(sw_q_norm[0] * (scale * LOG2_E), LANE_TILE),
                      _lane_bcast(k_norm, LANE_TILE), cos_t, sin_t)
    sink_rows = jnp.repeat((sw_sinks[0].astype(F32) * LOG2_E).reshape(SW_KV_HEADS, SW_GROUP),
                           WINDOW, axis=1)
    at = _swa(qt, k, vt, sink_rows)
    x = _proj_ffn(x, at, wo1, ffn_norm[1].reshape(1, d), wgu, wd, 1)
    return x
```

```python
import functools
import math

import jax
import jax.numpy as jnp
from jax import lax
from jax.experimental import pallas as pl
from jax.experimental.pallas import tpu as pltpu

HEAD_DIM = 64
ROPE_THETA = 10000.0
NORM_EPS = 1e-6
NEG_INF = -1e30
LOG2_E = math.log2(math.e)
WINDOW = 128
SW_KV_HEADS = 4
SW_GROUP = 4
SW_PACK = 2

F32 = jnp.float32
BF16 = jnp.bfloat16

V7X_VMEM_LIMIT_BYTES = 56 * 1024 * 1024

ROW_TILE = 1024
FEATURE_CHUNK = 512
FFN_CHUNK = 256
ATTN_BLOCK = 256
LANE_TILE = 128
ONES_ROWS = 16
LOOKAHEAD = 6
SWA_UNROLL = 15

TN_DIMS = (((0,), (0,)), ((), ()))


def _resident(shape):
    nd = len(shape)
    return pl.BlockSpec(shape, lambda *_: (0,) * nd, pipeline_mode=pl.Buffered(1))


def _rms_scale(x):
    return lax.rsqrt(jnp.mean(x * x, axis=-1, keepdims=True) + NORM_EPS)


def _pack_rows(x):
    return pltpu.bitcast(x, jnp.uint32)


def _unpack_rows(x):
    return pltpu.bitcast(x, BF16)


def _pack_weight(w_ref, wp_ref):
    for c in range(w_ref.shape[1] // FEATURE_CHUNK):
        cols = slice(c * FEATURE_CHUNK, (c + 1) * FEATURE_CHUNK)
        wp_ref[:, cols] = _pack_rows(w_ref[:, cols].astype(BF16))


def _project_t(h, wp_ref, cols):
    return jnp.dot(h, _unpack_rows(wp_ref[:, cols]), preferred_element_type=F32).T


def _norm_rope_t(t, gain, cos, sin):
    r = lax.rsqrt(jnp.mean(t * t, axis=0, keepdims=True) + NORM_EPS)
    tr = t * r
    tn = jnp.concatenate([tr[:, l:l + LANE_TILE] * gain
                          for l in range(0, t.shape[1], LANE_TILE)], axis=1)
    x1 = tn[: HEAD_DIM // 2]
    x2 = tn[HEAD_DIM // 2:]
    return jnp.concatenate([x1 * cos - x2 * sin, x2 * cos + x1 * sin], axis=0)


def _qkv0_kernel(x_ref, g_ref, w_ref, qg_ref, kg_ref, cos_ref, sin_ref,
                 wgu_ref, wd_ref, wo0_ref, wo1_ref,
                 qt_ref, k_ref, vt_ref, wgu_bf_ref, wd_bf_ref, wo0_bf_ref, wo1_bf_ref,
                 wp_ref, res_ref):
    d_model = x_ref.shape[-1]

    @pl.when((pl.program_id(0) == 0) & (pl.program_id(1) == 0))
    def _():
        _pack_weight(w_ref, wp_ref)

    @pl.when(pl.program_id(1) == 0)
    def _():
        wd_bf_ref[...] = wd_ref[...].astype(BF16)

    wgu_bf_ref[...] = wgu_ref[...].astype(BF16)
    wo0_bf_ref[...] = wo0_ref[...].astype(BF16)
    wo1_bf_ref[...] = wo1_ref[...].astype(BF16)

    x = x_ref[0]
    h = (x * _rms_scale(x) * g_ref[...]).astype(BF16)
    cos = cos_ref[0]
    sin = sin_ref[0]
    n_chunks = d_model // FEATURE_CHUNK

    def project(c):
        res_ref[c % 2] = _project_t(h, wp_ref, slice(c * FEATURE_CHUNK, (c + 1) * FEATURE_CHUNK))

    project(0)
    for c in range(3 * n_chunks):
        if c + 1 < 3 * n_chunks:
            project(c + 1)
        res = res_ref.at[c % 2]
        kind, cc = divmod(c, n_chunks)
        out_rows = slice(cc * FEATURE_CHUNK, (cc + 1) * FEATURE_CHUNK)
        half_rows = slice(cc * FEATURE_CHUNK // 2, (cc + 1) * FEATURE_CHUNK // 2)
        if kind == 2:
            vt_ref[0, half_rows, :] = _pack_rows(res[...].astype(BF16))
            continue
        gain_ref = qg_ref if kind == 0 else kg_ref
        parts = []
        for g in range(FEATURE_CHUNK // HEAD_DIM):
            half = g % 2
            gain = gain_ref[half * HEAD_DIM:(half + 1) * HEAD_DIM, :]
            parts.append(_norm_rope_t(res[g * HEAD_DIM:(g + 1) * HEAD_DIM, :], gain, cos, sin))
        out = jnp.concatenate(parts, axis=0)
        if kind == 0:
            qt_ref[0, out_rows, :] = out.astype(BF16)
        else:
            k_ref[0, :, out_rows] = _pack_rows(out.T.astype(BF16))


def _qkv0(x, g, w, qg, kg, cos_t, sin_t, wgu, wd, wo0, wo1):
    b, s, d = x.shape
    tm = ROW_TILE
    nt = s // tm
    grid = (b, nt)

    def per_step(a):
        return pl.BlockSpec((a.shape[0] // (b * nt), a.shape[1]), lambda i, j: (i * nt + j, 0))

    def per_batch(a):
        return pl.BlockSpec((a.shape[0] // b, a.shape[1]), lambda i, j: (i, 0))

    def as_bf16(a):
        return jax.ShapeDtypeStruct(a.shape, BF16)
    feat = pl.BlockSpec((1, d, tm), lambda i, j: (i, 0, j))
    feat_packed = pl.BlockSpec((1, d // 2, tm), lambda i, j: (i, 0, j))
    tok = pl.BlockSpec((1, tm, d), lambda i, j: (i, j, 0))
    tok_packed = pl.BlockSpec((1, tm // 2, d), lambda i, j: (i, j, 0))
    rope = pl.BlockSpec((1, HEAD_DIM // 2, tm), lambda i, j: (i, 0, j))
    return pl.pallas_call(
        _qkv0_kernel,
        grid=grid,
        in_specs=[tok, _resident((1, d)), _resident(w.shape), _resident(qg.shape),
                  _resident(kg.shape), rope, rope,
                  per_step(wgu), per_batch(wd), per_step(wo0), per_step(wo1)],
        out_specs=[feat, tok_packed, feat_packed,
                   per_step(wgu), per_batch(wd), per_step(wo0), per_step(wo1)],
        out_shape=[jax.ShapeDtypeStruct((b, d, s), BF16),
                   jax.ShapeDtypeStruct((b, s // 2, d), jnp.uint32),
                   jax.ShapeDtypeStruct((b, d // 2, s), jnp.uint32),
                   as_bf16(wgu), as_bf16(wd), as_bf16(wo0), as_bf16(wo1)],
        scratch_shapes=[pltpu.VMEM((d // 2, w.shape[1]), jnp.uint32),
                        pltpu.VMEM((2, FEATURE_CHUNK, tm), F32)],
        compiler_params=pltpu.CompilerParams(
            dimension_semantics=("arbitrary", "arbitrary"),
            vmem_limit_bytes=V7X_VMEM_LIMIT_BYTES),
        name="qkv0_proj",
    )(x, g, w, qg, kg, cos_t, sin_t, wgu, wd, wo0, wo1)


def _diff_attn_kernel(lam_ref, qt_ref, k_ref, vt_ref, sg_ref, ot_ref,
                      qz_ref, m_ref, acc_ref, s_ref, *, lambda_init):
    bq = qt_ref.shape[-1]
    hd2 = 2 * HEAD_DIM
    n_heads = qt_ref.shape[1] // hd2
    i = pl.program_id(1)

    lp = lam_ref[...]
    lam = (jnp.exp(jnp.sum(lp[0:1] * lp[1:2], axis=-1, keepdims=True))
           - jnp.exp(jnp.sum(lp[2:3] * lp[3:4], axis=-1, keepdims=True)) + lambda_init)

    zero = jnp.zeros((HEAD_DIM, bq), BF16)
    for h in range(n_heads):
        q0 = qt_ref[0, h * hd2:h * hd2 + HEAD_DIM, :]
        q1 = qt_ref[0, h * hd2 + HEAD_DIM:(h + 1) * hd2, :]
        qz_ref[2 * h] = jnp.concatenate([q0, zero], axis=0)
        qz_ref[2 * h + 1] = jnp.concatenate([zero, q1], axis=0)

    def block(k0, bk, first):
        ones = jnp.ones((ONES_ROWS, bk), BF16)
        k0_packed = pl.multiple_of(lax.shift_right_logical(k0, 1), bk // 2)
        if first:
            kidx = lax.broadcasted_iota(jnp.int32, (bk, bq), 0)
            qidx = lax.broadcasted_iota(jnp.int32, (bk, bq), 1)
            keep = kidx <= qidx + (bk - bq)

        def scores(hc):
            h = hc // 2
            kblk = _unpack_rows(k_ref[0, pl.ds(k0_packed, bk // 2), h * hd2:(h + 1) * hd2])
            return jnp.dot(kblk, qz_ref[hc], preferred_element_type=F32)

        n_slots = s_ref.shape[0]

        def park(hc):
            s_ref[hc % n_slots, :bk, :] = scores(hc)

        for hc in range(LOOKAHEAD):
            park(hc)
        for hc in range(2 * n_heads):
            h = hc // 2
            if hc + LOOKAHEAD < 2 * n_heads:
                park(hc + LOOKAHEAD)
            s = s_ref[hc % n_slots, :bk, :]
            vblk = _unpack_rows(vt_ref[0, h * HEAD_DIM:(h + 1) * HEAD_DIM, pl.ds(k0, bk)])
            vext = jnp.concatenate([vblk, ones], axis=0)
            if first:
                s = jnp.where(keep, s, NEG_INF)
                m_new = jnp.max(s, axis=0, keepdims=True)
                p = jnp.exp2(s - m_new)
                acc_ref[hc] = jnp.dot(vext, p.astype(BF16), preferred_element_type=F32)
            else:
                m_old = m_ref[hc]
                m_new = jnp.maximum(m_old, jnp.max(s, axis=0, keepdims=True))
                alpha = jnp.exp2(m_old - m_new)
                p = jnp.exp2(s - m_new)
                acc_ref[hc] = alpha * acc_ref[hc] + jnp.dot(
                    vext, p.astype(BF16), preferred_element_type=F32)
            m_ref[hc] = m_new

    @pl.when((i & 1) == 0)
    def _():
        block(pl.multiple_of(i * bq, bq), bq, True)

    @pl.when((i & 1) == 1)
    def _():
        block(pl.multiple_of((i - 1) * bq, 2 * bq), 2 * bq, True)

    def body(j, carry):
        block(pl.multiple_of(j * 2 * bq, 2 * bq), 2 * bq, False)
        return carry

    lax.fori_loop(0, lax.shift_right_logical(i, 1), body, 0)

    for h in range(n_heads):
        a0 = acc_ref[2 * h]
        a1 = acc_ref[2 * h + 1]
        l0 = a0[hd2:hd2 + 1]
        inv0 = 1.0 / l0
        rho = lam * l0 / a1[hd2:hd2 + 1]
        u = a0[:hd2] - a1[:hd2] * rho
        ms = jnp.mean(u * u, axis=0, keepdims=True)
        col = inv0 * lax.rsqrt(inv0 * inv0 * ms + NORM_EPS)
        ot_ref[0, h * hd2:(h + 1) * hd2, :] = (
            u * col * sg_ref[...]).astype(BF16)


def _diff_attn(lam_p, qt, k, vt, sg, lambda_init):
    b, d, s = qt.shape
    hd2 = 2 * HEAD_DIM
    bq = ATTN_BLOCK
    n_half = d // HEAD_DIM
    return pl.pallas_call(
        functools.partial(_diff_attn_kernel, lambda_init=lambda_init),
        grid=(b, s // bq),
        in_specs=[_resident(lam_p.shape),
                  pl.BlockSpec((1, d, bq), lambda bi, i: (bi, 0, i)),
                  pl.BlockSpec((1, s // 2, d), lambda bi, i: (bi, 0, 0)),
                  pl.BlockSpec((1, d // 2, s), lambda bi, i: (bi, 0, 0)),
                  _resident(sg.shape)],
        out_specs=pl.BlockSpec((1, d, bq), lambda bi, i: (bi, 0, i)),
        out_shape=jax.ShapeDtypeStruct((b, d, s), BF16),
        scratch_shapes=[pltpu.VMEM((n_half, hd2, bq), BF16),
                        pltpu.VMEM((n_half, 1, bq), F32),
                        pltpu.VMEM((n_half, hd2 + ONES_ROWS, bq), F32),
                        pltpu.VMEM((LOOKAHEAD + 1, 2 * bq, bq), F32)],
        compiler_params=pltpu.CompilerParams(
            dimension_semantics=("parallel", "arbitrary"),
            vmem_limit_bytes=V7X_VMEM_LIMIT_BYTES),
        name="diff_attn",
    )(lam_p, qt, k, vt, sg)


def _proj_ffn_kernel(x_ref, at_ref, wo_ref, g_ref, wg_ref, wu_ref, wd_ref, o_ref, hid_ref):
    x = x_ref[0] + lax.dot_general(at_ref[0], wo_ref[...], TN_DIMS,
                                   preferred_element_type=F32)
    h = (x * _rms_scale(x) * g_ref[...]).astype(BF16)
    n_chunks = hid_ref.shape[-1] // FFN_CHUNK
    for c in range(n_chunks):
        cols = slice(c * FFN_CHUNK, (c + 1) * FFN_CHUNK)
        gate = jnp.dot(h, wg_ref[:, cols], preferred_element_type=F32)
        up = jnp.dot(h, wu_ref[:, cols], preferred_element_type=F32)
        hid_ref[:, c * FFN_CHUNK:(c + 1) * FFN_CHUNK] = (
            gate * jax.nn.sigmoid(gate) * up).astype(BF16)
    o_ref[0] = x + jnp.dot(hid_ref[...], wd_ref[...], preferred_element_type=F32)


def _proj_ffn(x, at, wo, g, wgu, wd, layer):
    b, s, d = x.shape
    tm = ROW_TILE
    hidden = wd.shape[1]
    tok = pl.BlockSpec((1, tm, d), lambda i, j: (i, j, 0))
    feat = pl.BlockSpec((1, d, tm), lambda i, j: (i, 0, j))
    once = pl.Buffered(1)
    return pl.pallas_call(
        _proj_ffn_kernel,
        grid=(b, s // tm),
        in_specs=[tok, feat, _resident(wo.shape), _resident((1, d)),
                  pl.BlockSpec((None, d, hidden), lambda i, j: (layer, 0, 0), pipeline_mode=once),
                  pl.BlockSpec((None, d, hidden), lambda i, j: (layer, 0, 1), pipeline_mode=once),
                  pl.BlockSpec((None, hidden, d), lambda i, j: (layer, 0, 0), pipeline_mode=once)],
        out_specs=tok,
        out_shape=jax.ShapeDtypeStruct((b, s, d), F32),
        scratch_shapes=[pltpu.VMEM((tm, hidden), BF16)],
        compiler_params=pltpu.CompilerParams(
            dimension_semantics=("parallel", "parallel"),
            vmem_limit_bytes=V7X_VMEM_LIMIT_BYTES),
        name="proj_ffn",
    )(x, at, wo, g, wgu, wgu, wd)


def _qkv1_kernel(x_ref, ga_ref, gkv_ref, wq_ref, wkv_ref, qg_ref, kg_ref, cos_ref, sin_ref,
                 qt_ref, k_ref, vt_ref, wqp_ref, wkvp_ref, kv_ref, res_ref):
    d_model = x_ref.shape[-1]
    kv_dim = k_ref.shape[-1]

    @pl.when((pl.program_id(0) == 0) & (pl.program_id(1) == 0))
    def _():
        _pack_weight(wq_ref, wqp_ref)
        _pack_weight(wkv_ref, wkvp_ref)

    x = x_ref[0]
    xn = x * _rms_scale(x)
    h_a = (xn * ga_ref[...]).astype(BF16)
    h_kv = (xn * gkv_ref[...]).astype(BF16)
    cos = cos_ref[0]
    sin = sin_ref[0]

    n_chunks = d_model // FEATURE_CHUNK

    def project_q(c):
        res_ref[c % 2] = _project_t(h_a, wqp_ref,
                                    slice(c * FEATURE_CHUNK, (c + 1) * FEATURE_CHUNK))

    kv_ref[...] = _project_t(h_kv, wkvp_ref, slice(0, 2 * kv_dim))
    project_q(0)
    kparts = [_norm_rope_t(kv_ref[g * HEAD_DIM:(g + 1) * HEAD_DIM, :], kg_ref[...], cos, sin)
              for g in range(kv_dim // HEAD_DIM)]
    k_ref[0] = _pack_rows(jnp.concatenate(kparts, axis=0).T.astype(BF16))
    vt_ref[0] = _pack_rows(kv_ref[kv_dim:, :].astype(BF16))

    for c in range(n_chunks):
        if c + 1 < n_chunks:
            project_q(c + 1)
        res = res_ref.at[c % 2]
        rows = slice(c * FEATURE_CHUNK, (c + 1) * FEATURE_CHUNK)
        parts = [_norm_rope_t(res[g * HEAD_DIM:(g + 1) * HEAD_DIM, :], qg_ref[...], cos, sin)
                 for g in range(FEATURE_CHUNK // HEAD_DIM)]
        qt_ref[0, rows, :] = jnp.concatenate(parts, axis=0).astype(BF16)


def _qkv1(x, ga, gkv, wq, wkv, qg, kg, cos_t, sin_t):
    b, s, d = x.shape
    tm = ROW_TILE
    kv_dim = wkv.shape[1] // 2
    tok = pl.BlockSpec((1, tm, d), lambda i, j: (i, j, 0))
    rope = pl.BlockSpec((1, HEAD_DIM // 2, tm), lambda i, j: (i, 0, j))
    return pl.pallas_call(
        _qkv1_kernel,
        grid=(b, s // tm),
        in_specs=[tok, _resident((1, d)), _resident((1, d)), _resident(wq.shape),
                  _resident(wkv.shape), _resident(qg.shape), _resident(kg.shape), rope, rope],
        out_specs=[pl.BlockSpec((1, d, tm), lambda i, j: (i, 0, j)),
                   pl.BlockSpec((1, tm // 2, kv_dim), lambda i, j: (i, j, 0)),
                   pl.BlockSpec((1, kv_dim // 2, tm), lambda i, j: (i, 0, j))],
        out_shape=[jax.ShapeDtypeStruct((b, d, s), BF16),
                   jax.ShapeDtypeStruct((b, s // 2, kv_dim), jnp.uint32),
                   jax.ShapeDtypeStruct((b, kv_dim // 2, s), jnp.uint32)],
        scratch_shapes=[pltpu.VMEM((d // 2, wq.shape[1]), jnp.uint32),
                        pltpu.VMEM((d // 2, wkv.shape[1]), jnp.uint32),
                        pltpu.VMEM((2 * kv_dim, tm), F32),
                        pltpu.VMEM((2, FEATURE_CHUNK, tm), F32)],
        compiler_params=pltpu.CompilerParams(
            dimension_semantics=("arbitrary", "arbitrary"),
            vmem_limit_bytes=V7X_VMEM_LIMIT_BYTES),
        name="qkv1_proj",
    )(x, ga, gkv, wq, wkv, qg, kg, cos_t, sin_t)


def _swa_kernel(qt_ref, k_ref, vt_ref, sink_ref, ot_ref, bias_ref, s_ref):
    w = WINDOW
    s_len = qt_ref.shape[-1]
    kv_dim = k_ref.shape[-1]
    gw = SW_PACK * w

    n_kv = kv_dim // HEAD_DIM
    n_blocks = s_len // w

    kidx = lax.broadcasted_iota(jnp.int32, (2 * w, gw), 0)
    qidx = lax.broadcasted_iota(jnp.int32, (2 * w, gw), 1) & (w - 1)
    bias_ref[...] = jnp.where((kidx > qidx) & (kidx <= qidx + w), 0.0, NEG_INF).astype(F32)

    def window(n, first):
        nk = w if first else 2 * w
        k0 = 0 if first else pl.multiple_of((n - 1) * w, w)
        q0 = 0 if first else pl.multiple_of(n * w, w)
        return nk, k0, q0

    def heads(j, part):
        return [SW_GROUP * j + SW_PACK * part + u for u in range(SW_PACK)]

    def scores(n, j, part, first):
        nk, k0, q0 = window(n, first)
        k0_packed = k0 if first else pl.multiple_of((n - 1) * (w // 2), w // 2)
        kwin = _unpack_rows(k_ref[0, pl.ds(k0_packed, nk // 2), :])
        qcat = jnp.concatenate(
            [qt_ref[0, h * HEAD_DIM:(h + 1) * HEAD_DIM, pl.ds(q0, w)]
             for h in heads(j, part)], axis=1)
        pieces = []
        if j > 0:
            pieces.append(jnp.zeros((j * HEAD_DIM, gw), BF16))
        pieces.append(qcat)
        if j + 1 < n_kv:
            pieces.append(jnp.zeros(((n_kv - j - 1) * HEAD_DIM, gw), BF16))
        qz = jnp.concatenate(pieces, axis=0) if len(pieces) > 1 else qcat
        return jnp.dot(kwin, qz, preferred_element_type=F32)

    def finish(n, j, part, first, s):
        nk, k0, q0 = window(n, first)
        s = s + bias_ref[2 * w - nk:, :]
        sink = sink_ref[j:j + 1, part * gw:(part + 1) * gw]
        m = jnp.maximum(jnp.max(s, axis=0, keepdims=True), sink)
        e = jnp.exp2(s - m).astype(BF16)
        vwin = _unpack_rows(
            vt_ref[0, j * HEAD_DIM // 2:(j + 1) * HEAD_DIM // 2, pl.ds(k0, nk)])
        vext = jnp.concatenate([vwin, jnp.ones((ONES_ROWS, nk), BF16)], axis=0)
        o = jnp.dot(vext, e, preferred_element_type=F32)
        den = o[HEAD_DIM:HEAD_DIM + 1] + jnp.exp2(sink - m)
        o = o[:HEAD_DIM] * (1.0 / den)
        for u, h in enumerate(heads(j, part)):
            ot_ref[0, h * HEAD_DIM:(h + 1) * HEAD_DIM, pl.ds(q0, w)] = (
                o[:, u * w:(u + 1) * w].astype(BF16))

    def run(chains):
        n_slots = s_ref.shape[0]

        def park(t):
            nk = window(chains[t][0], chains[t][3])[0]
            s_ref[t % n_slots, :nk, :] = scores(*chains[t])

        for t in range(min(LOOKAHEAD, len(chains))):
            park(t)
        for t, ch in enumerate(chains):
            if t + LOOKAHEAD < len(chains):
                park(t + LOOKAHEAD)
            nk = window(ch[0], ch[3])[0]
            finish(*ch, s_ref[t % n_slots, :nk, :])

    parts = range(SW_GROUP // SW_PACK)
    run([(0, j, part, True) for j in range(n_kv) for part in parts])

    def body(t, carry):
        n0 = 1 + t * SWA_UNROLL
        run([(n0 + u, j, part, False)
             for u in range(SWA_UNROLL) for j in range(n_kv) for part in parts])
        return carry

    lax.fori_loop(0, (n_blocks - 1) // SWA_UNROLL, body, 0)


def _swa(qt, k, vt, sink_rows):
    b, d, s = qt.shape
    kv_dim = k.shape[-1]
    assert (s // WINDOW - 1) % SWA_UNROLL == 0
    return pl.pallas_call(
        _swa_kernel,
        grid=(b,),
        in_specs=[pl.BlockSpec((1, d, s), lambda i: (i, 0, 0)),
                  pl.BlockSpec((1, s // 2, kv_dim), lambda i: (i, 0, 0)),
                  pl.BlockSpec((1, kv_dim // 2, s), lambda i: (i, 0, 0)),
                  _resident(sink_rows.shape)],
        out_specs=pl.BlockSpec((1, d, s), lambda i: (i, 0, 0)),
        out_shape=jax.ShapeDtypeStruct((b, d, s), BF16),
        scratch_shapes=[pltpu.VMEM((2 * WINDOW, SW_PACK * WINDOW), F32),
                        pltpu.VMEM((LOOKAHEAD + 1, 2 * WINDOW, SW_PACK * WINDOW), F32)],
        compiler_params=pltpu.CompilerParams(
            dimension_semantics=("parallel",),
            vmem_limit_bytes=V7X_VMEM_LIMIT_BYTES),
        name="swa_attn",
    )(qt, k, vt, sink_rows)


def _lane_bcast(col, n):
    return jnp.broadcast_to(col.astype(F32).reshape(-1, 1), (col.size, n))


def kernel(x, positions, attn_norm, ffn_norm, w_gate_up, w_down, da_w_qkv, da_q_norm, da_k_norm,
           da_lambda, da_subln, da_w_o, kv_norm, w_kv, k_norm, sw_w_q, sw_q_norm, sw_sinks, sw_w_o):
    b, s, d = x.shape
    scale = 1.0 / math.sqrt(HEAD_DIM)

    inv = 1.0 / (ROPE_THETA ** (jnp.arange(0, HEAD_DIM, 2, dtype=F32) / HEAD_DIM))
    ang_t = positions.astype(F32)[:, None, :] * inv[None, :, None]
    cos_t = jnp.cos(ang_t)
    sin_t = jnp.sin(ang_t)

    lambda_init = 0.8 - 0.6 * math.exp(-0.3 * 0)
    n_layers, _, two_h = w_gate_up.shape
    qt, k, vt, wgu, wd, wo0, wo1 = _qkv0(
        x, attn_norm[0].reshape(1, d), da_w_qkv[0],
        _lane_bcast(da_q_norm[0] * (scale * LOG2_E), LANE_TILE),
        _lane_bcast(da_k_norm[0], LANE_TILE), cos_t, sin_t,
        w_gate_up.reshape(n_layers * d, two_h), w_down.reshape(n_layers * (two_h // 2), d),
        da_w_o[0], sw_w_o[0])
    wgu = wgu.reshape(n_layers, d, two_h)
    wd = wd.reshape(n_layers, two_h // 2, d)
    at = _diff_attn(da_lambda[0].astype(F32), qt, k, vt,
                    _lane_bcast(da_subln[0] * (1.0 - lambda_init), ATTN_BLOCK), lambda_init)
    x = _proj_ffn(x, at, wo0, ffn_norm[0].reshape(1, d), wgu, wd, 0)

    qt, k, vt = _qkv1(x, attn_norm[1].reshape(1, d), kv_norm.reshape(1, d),
                      sw_w_q[0], w_kv,
                      _lane_bcast(sw_q_norm[0] * (scale * LOG2_E), LANE_TILE),
                      _lane_bcast(k_norm, LANE_TILE), cos_t, sin_t)
    sink_rows = jnp.repeat((sw_sinks[0].astype(F32) * LOG2_E).reshape(SW_KV_HEADS, SW_GROUP),
                           WINDOW, axis=1)
    at = _swa(qt, k, vt, sink_rows)
    x = _proj_ffn(x, at, wo1, ffn_norm[1].reshape(1, d), wgu, wd, 1)
    return x
```
